```python
import jax, jax.numpy as jnp
from jax import lax
import numpy as np

D_MODEL = 1024
BATCH = 8
SEQ = 4096
DEPTH = 2
DEC_BATCH = 128
DEC_SEQ = 4
PAST_LEN = 16384
PAGE_SIZE = 128

EPS = 1e-6
POOL_GROUPS = 4
POOL_GC = 64
POOL_W = POOL_GROUPS * POOL_GC
POOL_WINDOWS = (2, 4, 8, 16)
POOL_BUF = 15
MLA_H = 8
Q_LORA = 256
KV_LORA = 128
NOPE = 64
ROPE_DIM = 32
V_DIM = 64
MLA_W = MLA_H * V_DIM
ROPE_BASE = 10000.0
MLA_SCALE = (NOPE + ROPE_DIM) ** -0.5
Q_BLOCK = 128
GLA_H = 4
GLA_DK = 32
GLA_DV = 64
GLA_W = GLA_H * GLA_DV
GATE_RANK = 16
GATE_TAU = 16.0
GLA_CHUNK = 64
IN_WIDTHS = (POOL_W, Q_LORA, KV_LORA, ROPE_DIM, GLA_H * GLA_DK, GLA_H * GLA_DK, GLA_W, GATE_RANK, GLA_W)
IN_SPLITS = tuple(int(s) for s in np.cumsum(IN_WIDTHS)[:-1])
P_IN = sum(IN_WIDTHS)
MIX_W = POOL_W + MLA_W + GLA_W
N_MEM = 256
X_H = 4
X_HD = 128
X_W = X_H * X_HD
N_GROUPS = 4
EXP_PER_GROUP = 8
N_EXPERTS = N_GROUPS * EXP_PER_GROUP
TOP_K = 2
D_EXPERT = 256
MOE_BLOCK = 128

kernel_name = 'hybrid_pool_mla_gla_hiermoe_step'


def rmsnorm(x, g):
    xf = x.astype(jnp.float32)
    y = xf * lax.rsqrt(jnp.mean(xf * xf, axis=-1, keepdims=True) + EPS)
    return (y * g.astype(jnp.float32)).astype(x.dtype)


def rope(x, pos):
    half = ROPE_DIM // 2
    inv = ROPE_BASE ** (-jnp.arange(half, dtype=jnp.float32) * 2.0 / ROPE_DIM)
    ang = pos.astype(jnp.float32)[:, None] * inv[None, :]
    shape = (1, ang.shape[0]) + (1,) * (x.ndim - 3) + (half,)
    c = jnp.cos(ang).reshape(shape)
    s = jnp.sin(ang).reshape(shape)
    xf = x.astype(jnp.float32)
    x1, x2 = xf[..., :half], xf[..., half:]
    return jnp.concatenate([x1 * c - x2 * s, x1 * s + x2 * c], axis=-1).astype(x.dtype)


def pool_mixer(u, u_prev, pos0, w_pool, pool_scale):
    B, T, _ = u.shape
    ue = jnp.concatenate([u_prev.astype(u.dtype), u], axis=1).astype(jnp.float32)
    cs = jnp.concatenate([jnp.zeros((B, 1, POOL_W), jnp.float32), jnp.cumsum(ue, axis=1)], axis=1)
    end = cs[:, POOL_BUF + 1:POOL_BUF + 1 + T]
    cnt_pos = pos0 + jnp.arange(T) + 1
    parts = []
    for g, w in enumerate(POOL_WINDOWS):
        ch = slice(g * POOL_GC, (g + 1) * POOL_GC)
        start = cs[:, POOL_BUF + 1 - w:POOL_BUF + 1 - w + T, ch]
        cnt = jnp.minimum(cnt_pos, w).astype(jnp.float32)
        parts.append((end[..., ch] - start) / cnt[None, :, None])
    pooled = jnp.concatenate(parts, axis=-1) - u.astype(jnp.float32)
    mixed = jnp.einsum('btgc,gcd->btgd', pooled.reshape(B, T, POOL_GROUPS, POOL_GC), w_pool.astype(jnp.float32))
    y = (mixed.reshape(B, T, POOL_W) * pool_scale.astype(jnp.float32)).astype(u.dtype)
    return y, ue[:, -POOL_BUF:].astype(u.dtype)


def mla_attend(q_lat, q_pe, kc, kr, q_pos, k_pos):
    B, T, H, C = q_lat.shape
    qb = Q_BLOCK if T % Q_BLOCK == 0 else T
    n = T // qb

    def block(args):
        ql, qp, pq = args
        s = (jnp.einsum('bthc,blc->bhtl', ql, kc) + jnp.einsum('bthr,blr->bhtl', qp, kr)).astype(jnp.float32) * MLA_SCALE
        s = jnp.where(k_pos[None, None, None, :] <= pq[None, None, :, None], s, -jnp.inf)
        p = jax.nn.softmax(s, axis=-1).astype(kc.dtype)
        return jnp.einsum('bhtl,blc->bthc', p, kc)

    def split(a):
        return a.reshape((B, n, qb) + a.shape[2:]).swapaxes(0, 1)

    out = lax.map(block, (split(q_lat), split(q_pe), q_pos.reshape(n, qb)))
    return out.swapaxes(0, 1).reshape(B, T, H, C)


def gla(q, k, v, log_a, s0):
    B, T, H, DK = q.shape
    DV = v.shape[-1]
    C = GLA_CHUNK if T % GLA_CHUNK == 0 else T
    n = T // C

    def chunks(a):
        return a.astype(jnp.float32).reshape(B, n, C, H, a.shape[-1]).transpose(1, 0, 3, 2, 4)

    qc = chunks(q) * (DK ** -0.5)
    kc, vc, gc = chunks(k), chunks(v), chunks(log_a)
    causal = jnp.tril(jnp.ones((C, C), dtype=bool))[None, None, :, :, None]

    def step(S, inp):
        qb, kb, vb, gb = inp
        b = jnp.cumsum(gb, axis=2)
        o_inter = jnp.einsum('bhtk,bhkv->bhtv', qb * jnp.exp(b), S)
        decay = jnp.exp(jnp.where(causal, b[:, :, :, None, :] - b[:, :, None, :, :], -jnp.inf))
        att = jnp.einsum('bhtsk,bhsk->bhts', qb[:, :, :, None, :] * decay, kb)
        o_intra = jnp.einsum('bhts,bhsv->bhtv', att, vb)
        b_last = b[:, :, -1:, :]
        S = jnp.exp(b_last[:, :, 0, :])[..., None] * S + jnp.einsum('bhsk,bhsv->bhkv', kb * jnp.exp(b_last - b), vb)
        return S, o_inter + o_intra

    S, o = lax.scan(step, s0.astype(jnp.float32), (qc, kc, vc, gc))
    return o.transpose(1, 0, 3, 2, 4).reshape(B, T, H, DV), S


def token_mixer(h, pos0, pool_prev, gla_s0, ckv_past, kpe_past, lw):
    B, T, _ = h.shape
    pos = pos0 + jnp.arange(T, dtype=jnp.int32)
    u, cq, ckv, kpe, gq, gk, gv, ga, go = jnp.split(h @ lw['w_in'], IN_SPLITS, axis=-1)
    ya, pool_new = pool_mixer(u, pool_prev, pos0, lw['w_pool'], lw['pool_scale'])
    q = (rmsnorm(cq, lw['g_qn']) @ lw['w_uq']).reshape(B, T, MLA_H, NOPE + ROPE_DIM)
    q_lat = jnp.einsum('bthn,chn->bthc', q[..., :NOPE], lw['w_uk'])
    q_pe = rope(q[..., NOPE:], pos)
    ckv = rmsnorm(ckv, lw['g_kvn'])
    kpe = rope(kpe, pos)
    if ckv_past is None:
        keys_c, keys_r, k_pos = ckv, kpe, pos
    else:
        keys_c = jnp.concatenate([ckv_past, ckv.astype(ckv_past.dtype)], axis=1)
        keys_r = jnp.concatenate([kpe_past, kpe.astype(kpe_past.dtype)], axis=1)
        k_pos = jnp.concatenate([jnp.arange(ckv_past.shape[1], dtype=jnp.int32), pos])
    o_lat = mla_attend(q_lat, q_pe, keys_c, keys_r, pos, k_pos)
    yb = jnp.einsum('bthc,chv->bthv', o_lat, lw['w_uv']).reshape(B, T, MLA_W).astype(h.dtype)
    log_a = jax.nn.log_sigmoid((ga @ lw['w_gk2'] + lw['b_gk']).astype(jnp.float32)) / GATE_TAU
    o, s_new = gla(gq.reshape(B, T, GLA_H, GLA_DK), gk.reshape(B, T, GLA_H, GLA_DK),
                   gv.reshape(B, T, GLA_H, GLA_DV), log_a.reshape(B, T, GLA_H, GLA_DK), gla_s0)
    gate = jax.nn.silu(go.astype(jnp.float32)).reshape(B, T, GLA_H, GLA_DV)
    yc = (rmsnorm(o, lw['g_gla']) * gate).reshape(B, T, GLA_W).astype(h.dtype)
    y = jnp.concatenate([ya, yb, yc], axis=-1) @ lw['w_out']
    return y, pool_new, ckv, kpe, s_new.astype(h.dtype)


def cross_attn(h, mk, mv, w_xq, w_xo):
    B, T, _ = h.shape
    q = (h @ w_xq).reshape(B, T, X_H, X_HD)
    s = jnp.einsum('bthd,bmhd->bhtm', q, mk).astype(jnp.float32) * (X_HD ** -0.5)
    p = jax.nn.softmax(s, axis=-1).astype(mv.dtype)
    o = jnp.einsum('bhtm,bmhd->bthd', p, mv).reshape(B, T, X_W)
    return (o @ w_xo).astype(h.dtype)


def hier_moe(h, w_rg, b_rg, w_re, b_re, w_eg, w_eu, w_ed):
    B, T, D = h.shape
    n = B * T
    x2 = h.reshape(n, D)
    g_logit = (x2 @ w_rg).astype(jnp.float32) + b_rg.astype(jnp.float32)
    g_sel = jnp.argmax(g_logit, axis=-1)
    g_w = jnp.take_along_axis(jax.nn.softmax(g_logit, axis=-1), g_sel[:, None], axis=1)[:, 0]
    e_logit = ((x2 @ w_re).astype(jnp.float32) + b_re.astype(jnp.float32)).reshape(n, N_GROUPS, EXP_PER_GROUP)
    e_logit = jnp.take_along_axis(e_logit, g_sel[:, None, None], axis=1)[:, 0]
    top_p, top_i = lax.top_k(jax.nn.softmax(e_logit, axis=-1), TOP_K)
    gates = g_w[:, None] * top_p / jnp.sum(top_p, axis=-1, keepdims=True)
    expert = (g_sel[:, None] * EXP_PER_GROUP + top_i).reshape(-1).astype(jnp.int32)
    s = n * TOP_K
    order = jnp.argsort(expert)
    e_sorted = expert[order]
    counts = jnp.bincount(expert, length=N_EXPERTS)
    starts = jnp.cumsum(counts) - counts
    pcounts = (counts + MOE_BLOCK - 1) // MOE_BLOCK * MOE_BLOCK
    pends = jnp.cumsum(pcounts)
    pstarts = pends - pcounts
    dest_sorted = (pstarts[e_sorted] + jnp.arange(s) - starts[e_sorted]).astype(jnp.int32)
    dest = jnp.zeros((s,), jnp.int32).at[order].set(dest_sorted)
    n_blocks = -(-s // MOE_BLOCK) + N_EXPERTS
    rows = jnp.zeros((n_blocks * MOE_BLOCK, D), h.dtype).at[dest].set(x2[jnp.arange(s) // TOP_K])
    block_e = jnp.minimum(jnp.searchsorted(pends, jnp.arange(n_blocks) * MOE_BLOCK, side='right'), N_EXPERTS - 1)

    def expert_block(args):
        xb, e = args
        hid = jax.nn.silu(xb @ w_eg[e]) * (xb @ w_eu[e])
        return hid @ w_ed[e]

    out = lax.map(expert_block, (rows.reshape(n_blocks, MOE_BLOCK, D), block_e)).reshape(-1, D)
    y_slot = out[dest].reshape(n, TOP_K, D).astype(jnp.float32)
    y = jnp.einsum('nkd,nk->nd', y_slot, gates)
    return y.astype(h.dtype).reshape(B, T, D)


def layer(x, pos0, pool_prev, gla_s0, ckv_past, kpe_past, mem_k, mem_v, lw):
    y, pool_new, ckv_new, kpe_new, s_new = token_mixer(rmsnorm(x, lw['g_mix']), pos0, pool_prev, gla_s0, ckv_past, kpe_past, lw)
    x = x + y
    x = x + cross_attn(rmsnorm(x, lw['g_x']), mem_k, mem_v, lw['w_xq'], lw['w_xo'])
    x = x + hier_moe(rmsnorm(x, lw['g_ffn']), lw['w_rg'], lw['b_rg'], lw['w_re'], lw['b_re'], lw['w_eg'], lw['w_eu'], lw['w_ed'])
    return x, pool_new, ckv_new, kpe_new, s_new


def setup_inputs(seed: int = 0) -> dict:
    key = jax.random.key(seed)
    ks = iter(jax.random.split(key, 48))

    def nrm(shape, scale=1.0):
        return jax.random.normal(next(ks), shape, jnp.float32) * scale

    def gain(shape):
        return 1.0 + 0.02 * jax.random.normal(next(ks), shape, jnp.float32)

    n_pages = PAST_LEN // PAGE_SIZE
    n_used = DEC_BATCH * n_pages
    n_phys = n_used + n_used // 4
    L = DEPTH
    return {
        'x_prompt': nrm((BATCH, SEQ, D_MODEL)),
        'x_sample': nrm((DEC_BATCH, DEC_SEQ, D_MODEL)),
        'mem_prompt': nrm((BATCH, N_MEM, D_MODEL)),
        'cache_ckv': nrm((L, n_phys, PAGE_SIZE, KV_LORA)),
        'cache_kpe': nrm((L, n_phys, PAGE_SIZE, ROPE_DIM)),
        'page_table': jax.random.permutation(next(ks), n_phys)[:n_used].reshape(DEC_BATCH, n_pages).astype(jnp.int32),
        'state_pool': nrm((L, DEC_BATCH, POOL_BUF, POOL_W), 0.5),
        'state_gla': nrm((L, DEC_BATCH, GLA_H, GLA_DK, GLA_DV), 0.5),
        'cache_mem_k': nrm((L, DEC_BATCH, N_MEM, X_H, X_HD)),
        'cache_mem_v': nrm((L, DEC_BATCH, N_MEM, X_H, X_HD)),
        'g_mix': gain((L, D_MODEL)),
        'w_in': nrm((L, D_MODEL, P_IN), D_MODEL ** -0.5),
        'w_pool': nrm((L, POOL_GROUPS, POOL_GC, POOL_GC), POOL_GC ** -0.5),
        'pool_scale': gain((L, POOL_W)),
        'g_qn': gain((L, Q_LORA)),
        'w_uq': nrm((L, Q_LORA, MLA_H * (NOPE + ROPE_DIM)), Q_LORA ** -0.5),
        'g_kvn': gain((L, KV_LORA)),
        'w_uk': nrm((L, KV_LORA, MLA_H, NOPE), KV_LORA ** -0.5),
        'w_uv': nrm((L, KV_LORA, MLA_H, V_DIM), KV_LORA ** -0.5),
        'w_gk2': nrm((L, GATE_RANK, GLA_H * GLA_DK), GATE_RANK ** -0.5),
        'b_gk': nrm((L, GLA_H * GLA_DK), 0.1),
        'g_gla': gain((L, GLA_DV)),
        'w_out': nrm((L, MIX_W, D_MODEL), MIX_W ** -0.5),
        'g_x': gain((L, D_MODEL)),
        'g_mem': gain((L, D_MODEL)),
        'w_xq': nrm((L, D_MODEL, X_W), D_MODEL ** -0.5),
        'w_xk': nrm((L, D_MODEL, X_W), D_MODEL ** -0.5),
        'w_xv': nrm((L, D_MODEL, X_W), D_MODEL ** -0.5),
        'w_xo': nrm((L, X_W, D_MODEL), X_W ** -0.5),
        'g_ffn': gain((L, D_MODEL)),
        'w_rg': nrm((L, D_MODEL, N_GROUPS), D_MODEL ** -0.5),
        'b_rg': nrm((L, N_GROUPS), 0.01),
        'w_re': nrm((L, D_MODEL, N_EXPERTS), D_MODEL ** -0.5),
        'b_re': nrm((L, N_EXPERTS), 0.01),
        'w_eg': nrm((L, N_EXPERTS, D_MODEL, D_EXPERT), D_MODEL ** -0.5),
        'w_eu': nrm((L, N_EXPERTS, D_MODEL, D_EXPERT), D_MODEL ** -0.5),
        'w_ed': nrm((L, N_EXPERTS, D_EXPERT, D_MODEL), D_EXPERT ** -0.5),
        'g_final': gain((D_MODEL,)),
    }


def reference(x_prompt, x_sample, mem_prompt, cache_ckv, cache_kpe, page_table, state_pool, state_gla,
              cache_mem_k, cache_mem_v, g_mix, w_in, w_pool, pool_scale, g_qn, w_uq, g_kvn, w_uk, w_uv,
              w_gk2, b_gk, g_gla, w_out, g_x, g_mem, w_xq, w_xk, w_xv, w_xo, g_ffn, w_rg, b_rg, w_re, b_re,
              w_eg, w_eu, w_ed, g_final):
    B = x_prompt.shape[0]
    DB = x_sample.shape[0]
    past_len = page_table.shape[1] * cache_ckv.shape[2]
    xp, xs = x_prompt, x_sample
    ckv_p, kpe_p, pool_p, gla_p, mk_p, mv_p = [], [], [], [], [], []
    ckv_s, kpe_s, pool_s, gla_s = [], [], [], []
    for l in range(DEPTH):
        lw = {'g_mix': g_mix[l], 'w_in': w_in[l], 'w_pool': w_pool[l], 'pool_scale': pool_scale[l],
              'g_qn': g_qn[l], 'w_uq': w_uq[l], 'g_kvn': g_kvn[l], 'w_uk': w_uk[l], 'w_uv': w_uv[l],
              'w_gk2': w_gk2[l], 'b_gk': b_gk[l], 'g_gla': g_gla[l], 'w_out': w_out[l],
              'g_x': g_x[l], 'w_xq': w_xq[l], 'w_xo': w_xo[l], 'g_ffn': g_ffn[l],
              'w_rg': w_rg[l], 'b_rg': b_rg[l], 'w_re': w_re[l], 'b_re': b_re[l],
              'w_eg': w_eg[l], 'w_eu': w_eu[l], 'w_ed': w_ed[l]}
        mem_n = rmsnorm(mem_prompt, g_mem[l])
        mk = (mem_n @ w_xk[l]).reshape(B, N_MEM, X_H, X_HD)
        mv = (mem_n @ w_xv[l]).reshape(B, N_MEM, X_H, X_HD)
        pool0 = jnp.zeros((B, POOL_BUF, POOL_W), xp.dtype)
        gla0 = jnp.zeros((B, GLA_H, GLA_DK, GLA_DV), jnp.float32)
        xp, pn, cn, kn, sn = layer(xp, 0, pool0, gla0, None, None, mk, mv, lw)
        ckv_p.append(cn); kpe_p.append(kn); pool_p.append(pn); gla_p.append(sn); mk_p.append(mk); mv_p.append(mv)
        ckv_past = cache_ckv[l][page_table].reshape(DB, past_len, KV_LORA)
        kpe_past = cache_kpe[l][page_table].reshape(DB, past_len, ROPE_DIM)
        xs, pn, cn, kn, sn = layer(xs, past_len, state_pool[l], state_gla[l], ckv_past, kpe_past,
                                   cache_mem_k[l], cache_mem_v[l], lw)
        ckv_s.append(cn); kpe_s.append(kn); pool_s.append(pn); gla_s.append(sn)
    y_prompt = rmsnorm(xp, g_final)
    y_sample = rmsnorm(xs, g_final)
    return (y_prompt, y_sample,
            jnp.stack(ckv_p), jnp.stack(kpe_p), jnp.stack(pool_p), jnp.stack(gla_p), jnp.stack(mk_p), jnp.stack(mv_p),
            jnp.stack(ckv_s), jnp.stack(kpe_s), jnp.stack(pool_s), jnp.stack(gla_s))
```

```python
import functools

import numpy as np
import jax
import jax.numpy as jnp
from jax import lax
from jax.experimental import pallas as pl
from jax.experimental.pallas import tpu as pltpu

F32 = jnp.float32
BF16 = jnp.bfloat16
I32 = jnp.int32

EPS = 1e-6
D_MODEL = 1024
POOL_GROUPS, POOL_GC = 4, 64
POOL_W = POOL_GROUPS * POOL_GC
POOL_WINDOWS = (2, 4, 8, 16)
POOL_BUF = 15
MLA_H, Q_LORA, KV_LORA, NOPE, ROPE_DIM, V_DIM = 8, 256, 128, 64, 32, 64
ROPE_BASE = 10000.0
MLA_SCALE = (NOPE + ROPE_DIM) ** -0.5
GLA_H, GLA_DK, GLA_DV, GATE_RANK, GATE_TAU = 4, 32, 64, 16, 16.0
GLA_W = GLA_H * GLA_DV
N_MEM, X_H, X_HD = 256, 4, 128
X_W = X_H * X_HD
N_GROUPS, EXP_PER_GROUP, N_EXPERTS, TOP_K, D_EXPERT = 4, 8, 32, 2, 256
QK_PAD = 256
IN_PAD = 1536
EXPERT_LANE0 = 32
MOE_BLOCK = 256
NEG = -1e30
VMEM_LIMIT = 56 * 1024 * 1024


def _pick(n, prefs):
    for p in prefs:
        if n % p == 0:
            return p
    raise ValueError(f"no tile in {prefs} divides {n}")


def _cparams(n_axes):
    return pltpu.CompilerParams(dimension_semantics=("arbitrary",) * n_axes, vmem_limit_bytes=VMEM_LIMIT)


def _rms(x, g):
    ms = jnp.mean(x * x, axis=-1, keepdims=True)
    return x * lax.rsqrt(ms + EPS) * g


def _dot(a, b):
    return jnp.dot(a, b, preferred_element_type=F32)


def _dot_nt(a, b):
    return lax.dot_general(a, b, (((1,), (1,)), ((), ())), preferred_element_type=F32)


def _dot_tn(a, b):
    return lax.dot_general(a, b, (((0,), (0,)), ((), ())), preferred_element_type=F32)


def _split_bf16(x):
    hi = x.astype(BF16)
    lo = (x - hi.astype(F32)).astype(BF16)
    return hi, lo


def _swap16(x):
    w = x.shape[-1]
    lane = lax.broadcasted_iota(I32, x.shape, x.ndim - 1)
    first = (lane & 31) < 16
    return jnp.where(first, pltpu.roll(x, w - 16, x.ndim - 1), pltpu.roll(x, 16, x.ndim - 1))


def _silu(x):
    return x / (1.0 + jnp.exp(-x))


def _inproj_kernel(x_ref, gmix_ref, win_ref, cos_ref, sin_ref, gqn_ref, wuq_ref, wuk_ref, gkvn_ref, wgk_ref, bgk_ref,
                   u_ref, q_ref, kvb_ref, ckv_ref, kpe_ref, gq_ref, gk_ref, gv_ref, go_ref, glog_ref):
    h = _rms(x_ref[...], gmix_ref[...]).astype(BF16)
    y = _dot(h, win_ref[...])
    u_ref[...] = y[:, 0:256]
    gq_ref[...] = y[:, 768:896]
    gk_ref[...] = y[:, 896:1024]
    gv_ref[...] = y[:, 1024:1280]
    go_ref[...] = y[:, 1280:1536]
    cos = cos_ref[...]
    sin = sin_ref[...]
    ckv = _rms(y[:, 512:640], gkvn_ref[...])
    grp = y[:, 640:768]
    grp_r = grp * cos[:, 0:128] + _swap16(grp) * sin[:, 0:128]
    ckv_ref[...] = ckv
    kpe_ref[...] = grp_r[:, 0:ROPE_DIM]
    lane = lax.broadcasted_iota(I32, grp_r.shape, 1)
    kvb_ref[:, 0:128] = ckv.astype(BF16)
    kvb_ref[:, 128:256] = jnp.where(lane < ROPE_DIM, grp_r, 0.0).astype(BF16)
    gl = _dot(grp.astype(BF16), wgk_ref[...]) + bgk_ref[...]
    glog_ref[...] = (jnp.minimum(gl, 0.0) - jnp.log(1.0 + jnp.exp(-jnp.abs(gl)))) * (1.0 / GATE_TAU)
    cqn = _rms(y[:, 256:512], gqn_ref[...]).astype(BF16)
    q = _dot(cqn, wuq_ref[...])
    qlat = _dot(q[:, 0:512].astype(BF16), wuk_ref[...])
    qr = q[:, 512:768]
    qr = qr * cos + _swap16(qr) * sin
    for hh in range(MLA_H):
        q_ref[hh, :, 0:128] = qlat[:, 128 * hh:128 * hh + 128].astype(BF16)
        col = qr[:, 128 * (hh // 4):128 * (hh // 4) + 128]
        sh = (128 - 32 * (hh % 4)) % 128
        if sh:
            col = pltpu.roll(col, sh, 1)
        q_ref[hh, :, 128:256] = jnp.where(lane < ROPE_DIM, col, 0.0).astype(BF16)


def _inproj(x, cos_t, sin_t, w):
    n = x.shape[0]
    tm = _pick(n, (256, 128, 64, 32, 16))
    tab_tiles = cos_t.shape[0] // tm
    assert cos_t.shape[0] % tm == 0
    row = lambda i: (i, 0)
    tab = lambda i: (i % tab_tiles, 0)
    full = lambda i: (0, 0)
    wspec = lambda a: pl.BlockSpec(a.shape, full)
    outs = [
        jax.ShapeDtypeStruct((n, 256), F32),
        jax.ShapeDtypeStruct((MLA_H, n, QK_PAD), BF16),
        jax.ShapeDtypeStruct((n, QK_PAD), BF16),
        jax.ShapeDtypeStruct((n, KV_LORA), F32),
        jax.ShapeDtypeStruct((n, ROPE_DIM), F32),
        jax.ShapeDtypeStruct((n, 128), F32),
        jax.ShapeDtypeStruct((n, 128), F32),
        jax.ShapeDtypeStruct((n, 256), F32),
        jax.ShapeDtypeStruct((n, 256), F32),
        jax.ShapeDtypeStruct((n, 128), F32),
    ]
    out_specs = [
        pl.BlockSpec((tm, 256), row),
        pl.BlockSpec((MLA_H, tm, QK_PAD), lambda i: (0, i, 0)),
        pl.BlockSpec((tm, QK_PAD), row),
        pl.BlockSpec((tm, KV_LORA), row),
        pl.BlockSpec((tm, ROPE_DIM), row),
        pl.BlockSpec((tm, 128), row),
        pl.BlockSpec((tm, 128), row),
        pl.BlockSpec((tm, 256), row),
        pl.BlockSpec((tm, 256), row),
        pl.BlockSpec((tm, 128), row),
    ]
    ins = [x, w["g_mix"], w["w_in"], cos_t, sin_t, w["g_qn"], w["w_uq"], w["w_ukbd"], w["g_kvn"], w["w_gk2"], w["b_gk"]]
    in_specs = [pl.BlockSpec((tm, D_MODEL), row), wspec(w["g_mix"]), wspec(w["w_in"]),
                pl.BlockSpec((tm, 256), tab), pl.BlockSpec((tm, 256), tab),
                wspec(w["g_qn"]), wspec(w["w_uq"]), wspec(w["w_ukbd"]), wspec(w["g_kvn"]), wspec(w["w_gk2"]), wspec(w["b_gk"])]
    return pl.pallas_call(_inproj_kernel, out_shape=outs, grid=(n // tm,), in_specs=in_specs, out_specs=out_specs,
                          compiler_params=_cparams(1), name="inproj")(*ins)


def _pool_mix(sums, u, cnts, wbd, scale):
    cols = []
    for c in range(2):
        wa, wb = POOL_WINDOWS[2 * c], POOL_WINDOWS[2 * c + 1]
        sa, sb = sums[(c, wa)], sums[(c, wb)]
        lane = lax.broadcasted_iota(I32, sa.shape, 1)
        pooled = jnp.where(lane < POOL_GC, sa / cnts[wa], sb / cnts[wb]) - u[:, 128 * c:128 * c + 128]
        cols.append(pooled)
    pooled = jnp.concatenate(cols, axis=1).astype(BF16)
    return (_dot(pooled, wbd) * scale).astype(BF16)


def _pool_kernel(u_ref, wbd_ref, scale_ref, ya_ref, st_ref, ue_ref, *, tp, nt):
    j = pl.program_id(1)

    @pl.when(j == 0)
    def _():
        ue_ref[0:16, :] = jnp.zeros((16, POOL_W), F32)

    @pl.when(j > 0)
    def _():
        ue_ref[0:16, :] = ue_ref[tp:tp + 16, :]

    u = u_ref[...]
    ue_ref[16:16 + tp, :] = u
    t = j * tp + lax.broadcasted_iota(I32, (tp, 1), 0)
    cnts = {w: jnp.minimum(t + 1, w).astype(F32) for w in POOL_WINDOWS}
    sums = {}
    for c in range(2):
        wa, wb = POOL_WINDOWS[2 * c], POOL_WINDOWS[2 * c + 1]
        acc = None
        for k in range(wb):
            sl = ue_ref[16 - k:16 - k + tp, 128 * c:128 * c + 128]
            acc = sl if acc is None else acc + sl
            if k + 1 == wa:
                sums[(c, wa)] = acc
        sums[(c, wb)] = acc
    ya_ref[...] = _pool_mix(sums, u, cnts, wbd_ref[...], scale_ref[...])

    @pl.when(j == nt - 1)
    def _():
        st_ref[0] = ue_ref[tp + 1:tp + 16, :]


def _pool_prompt(u, bsz, w):
    n = u.shape[0]
    t = n // bsz
    tp = _pick(t, (512, 256, 128, 64, 32, 16))
    nt = t // tp
    kern = functools.partial(_pool_kernel, tp=tp, nt=nt)
    return pl.pallas_call(
        kern,
        out_shape=[jax.ShapeDtypeStruct((n, POOL_W), BF16), jax.ShapeDtypeStruct((bsz, POOL_BUF, POOL_W), F32)],
        grid=(bsz, nt),
        in_specs=[pl.BlockSpec((tp, POOL_W), lambda b, j: (b * nt + j, 0)),
                  pl.BlockSpec((POOL_W, POOL_W), lambda b, j: (0, 0)),
                  pl.BlockSpec((1, POOL_W), lambda b, j: (0, 0))],
        out_specs=[pl.BlockSpec((tp, POOL_W), lambda b, j: (b * nt + j, 0)),
                   pl.BlockSpec((1, POOL_BUF, POOL_W), lambda b, j: (b, 0, 0))],
        scratch_shapes=[pltpu.VMEM((tp + 16, POOL_W), F32)],
        compiler_params=_cparams(2), name="pool_prompt")(u, w["w_poolbd"], w["pool_scale"])


def _pool_step_kernel(ue_ref, wbd_ref, scale_ref, ya_ref, st_ref, *, tt):
    cnts = {w: jnp.float32(w) for w in POOL_WINDOWS}
    for t in range(tt):
        sums = {}
        for c in range(2):
            wa, wb = POOL_WINDOWS[2 * c], POOL_WINDOWS[2 * c + 1]
            acc = None
            for k in range(wb):
                sl = ue_ref[POOL_BUF + t - k, :, 128 * c:128 * c + 128]
                acc = sl if acc is None else acc + sl
                if k + 1 == wa:
                    sums[(c, wa)] = acc
            sums[(c, wb)] = acc
        ya_ref[t] = _pool_mix(sums, ue_ref[POOL_BUF + t], cnts, wbd_ref[...], scale_ref[...])
    for r in range(POOL_BUF):
        st_ref[r] = ue_ref[tt + r]


def _pool_sample(ue_tm, w):
    rows, bsz, _ = ue_tm.shape
    tt = rows - POOL_BUF
    kern = functools.partial(_pool_step_kernel, tt=tt)
    full3 = lambda i: (0, 0, 0)
    return pl.pallas_call(
        kern,
        out_shape=[jax.ShapeDtypeStruct((tt, bsz, POOL_W), BF16), jax.ShapeDtypeStruct((POOL_BUF, bsz, POOL_W), F32)],
        grid=(1,),
        in_specs=[pl.BlockSpec(ue_tm.shape, full3), pl.BlockSpec((POOL_W, POOL_W), lambda i: (0, 0)),
                  pl.BlockSpec((1, POOL_W), lambda i: (0, 0))],
        out_specs=[pl.BlockSpec((tt, bsz, POOL_W), full3), pl.BlockSpec((POOL_BUF, bsz, POOL_W), full3)],
        compiler_params=_cparams(1), name="pool_sample")(ue_tm, w["w_poolbd"], w["pool_scale"])


def _mla_prompt_kernel(qi_ref, kj_ref, q_ref, k_ref, o_ref, m_ref, l_ref, acc_ref, *, tq, tk):
    p = pl.program_id(1)
    qi = qi_ref[p]
    kj = kj_ref[p]
    rows = MLA_H * tq

    @pl.when(kj == 0)
    def _():
        m_ref[...] = jnp.full((rows, 1), NEG, F32)
        l_ref[...] = jnp.zeros((rows, 1), F32)
        acc_ref[...] = jnp.zeros((rows, KV_LORA), F32)

    def step(masked):
        q = q_ref[...].reshape(rows, QK_PAD)
        k = k_ref[...]
        s = _dot_nt(q, k) * MLA_SCALE
        if masked:
            r = lax.broadcasted_iota(I32, (rows, tk), 0)
            c = lax.broadcasted_iota(I32, (rows, tk), 1)
            qpos = qi * tq + (r & (tq - 1))
            s = jnp.where(kj * tk + c <= qpos, s, NEG)
        m_old = m_ref[...]
        m_new = jnp.maximum(m_old, jnp.max(s, axis=-1, keepdims=True))
        alpha = jnp.exp(m_old - m_new)
        pr = jnp.exp(s - m_new)
        l_ref[...] = alpha * l_ref[...] + jnp.sum(pr, axis=-1, keepdims=True)
        acc_ref[...] = alpha * acc_ref[...] + _dot(pr.astype(BF16), k[:, 0:KV_LORA])
        m_ref[...] = m_new

    crosses = (kj + 1) * tk - 1 > qi * tq

    @pl.when(crosses)
    def _():
        step(True)

    @pl.when(jnp.logical_not(crosses))
    def _():
        step(False)

    @pl.when(kj == ((qi + 1) * tq - 1) // tk)
    def _():
        o = acc_ref[...] / l_ref[...]
        for h in range(MLA_H):
            o_ref[:, KV_LORA * h:KV_LORA * (h + 1)] = o[h * tq:(h + 1) * tq].astype(BF16)


def _mla_prompt(q_hm, kvb, bsz):
    n = kvb.shape[0]
    t = n // bsz
    tq = _pick(t, (256, 128, 64, 32, 16))
    tk = _pick(t, (512, 256, 128, 64, 32, 16))
    assert tq & (tq - 1) == 0
    nq, nk = t // tq, t // tk
    pairs = [(i, j) for i in range(nq) for j in range(((i + 1) * tq - 1) // tk + 1)]
    qi = jnp.asarray(np.array([p[0] for p in pairs], np.int32))
    kj = jnp.asarray(np.array([p[1] for p in pairs], np.int32))
    rows = MLA_H * tq
    kern = functools.partial(_mla_prompt_kernel, tq=tq, tk=tk)
    grid_spec = pltpu.PrefetchScalarGridSpec(
        num_scalar_prefetch=2, grid=(bsz, len(pairs)),
        in_specs=[pl.BlockSpec((MLA_H, tq, QK_PAD), lambda b, p, qi, kj: (0, b * nq + qi[p], 0)),
                  pl.BlockSpec((tk, QK_PAD), lambda b, p, qi, kj: (b * nk + kj[p], 0))],
        out_specs=pl.BlockSpec((tq, MLA_H * KV_LORA), lambda b, p, qi, kj: (b * nq + qi[p], 0)),
        scratch_shapes=[pltpu.VMEM((rows, 1), F32), pltpu.VMEM((rows, 1), F32), pltpu.VMEM((rows, KV_LORA), F32)])
    return pl.pallas_call(kern, out_shape=jax.ShapeDtypeStruct((n, MLA_H * KV_LORA), BF16), grid_spec=grid_spec,
                          compiler_params=_cparams(2), name="mla_prompt")(qi, kj, q_hm, kvb)


def _mla_decode_kernel(pt_ref, q_ref, knew_ref, cckv_ref, ckpe_ref, o_ref,
                       cbuf, pbuf, csem, psem, m_ref, l_ref, acc_ref, *, layer, ng, gp, page, tt, total):
    s_idx = pl.program_id(0)
    grp = s_idx % ng
    slot = s_idx % 2

    def copies(step, slot_):
        out = []
        for g in range(gp):
            pg = pt_ref[step * gp + g]
            out.append(pltpu.make_async_copy(cckv_ref.at[layer, pg], cbuf.at[slot_, pl.ds(g * page, page)], csem.at[slot_]))
            out.append(pltpu.make_async_copy(ckpe_ref.at[layer, pg], pbuf.at[slot_, pl.ds(g * page, page)], psem.at[slot_]))
        return out

    @pl.when(s_idx == 0)
    def _():
        for c in copies(0, 0):
            c.start()

    @pl.when(s_idx + 1 < total)
    def _():
        for c in copies(s_idx + 1, 1 - slot):
            c.start()

    @pl.when(grp == 0)
    def _():
        m_ref[...] = jnp.full(m_ref.shape, NEG, F32)
        l_ref[...] = jnp.zeros(l_ref.shape, F32)
        acc_ref[...] = jnp.zeros(acc_ref.shape, F32)

    for c in copies(s_idx, slot):
        c.wait()

    q = q_ref[0]

    def update(s, v):
        m_old = m_ref[...]
        m_new = jnp.maximum(m_old, jnp.max(s, axis=-1, keepdims=True))
        alpha = jnp.exp(m_old - m_new)
        pr = jnp.exp(s - m_new)
        l_ref[...] = alpha * l_ref[...] + jnp.sum(pr, axis=-1, keepdims=True)
        acc_ref[...] = alpha * acc_ref[...] + _dot(pr.astype(BF16), v)
        m_ref[...] = m_new

    kc = cbuf[slot].astype(BF16)
    kr = pbuf[slot].astype(BF16)
    s = (_dot_nt(q[:, 0:KV_LORA], kc) + _dot_nt(q[:, KV_LORA:KV_LORA + ROPE_DIM], kr)) * MLA_SCALE
    update(s, kc)

    @pl.when(grp == ng - 1)
    def _():
        kn = knew_ref[0]
        sn = _dot_nt(q, kn) * MLA_SCALE
        r = lax.broadcasted_iota(I32, sn.shape, 0)
        c = lax.broadcasted_iota(I32, sn.shape, 1)
        sn = jnp.where(c <= (r % tt), sn, NEG)
        update(sn, kn[:, 0:KV_LORA])
        o_ref[0] = acc_ref[...] / l_ref[...]


def _mla_decode(q_s, knew, cache_ckv, cache_kpe, page_table, layer):
    dbsz, rows, _ = q_s.shape
    tt = rows // MLA_H
    n_pages = page_table.shape[1]
    page = cache_ckv.shape[2]
    gp = _pick(n_pages, (16, 8, 4, 2, 1))
    ng = n_pages // gp
    total = dbsz * ng
    kern = functools.partial(_mla_decode_kernel, layer=layer, ng=ng, gp=gp, page=page, tt=tt, total=total)
    grid_spec = pltpu.PrefetchScalarGridSpec(
        num_scalar_prefetch=1, grid=(total,),
        in_specs=[pl.BlockSpec((1, rows, QK_PAD), lambda s, pt: (s // ng, 0, 0)),
                  pl.BlockSpec((1, 16, QK_PAD), lambda s, pt: (s // ng, 0, 0)),
                  pl.BlockSpec(memory_space=pl.ANY), pl.BlockSpec(memory_space=pl.ANY)],
        out_specs=pl.BlockSpec((1, rows, KV_LORA), lambda s, pt: (s // ng, 0, 0)),
        scratch_shapes=[pltpu.VMEM((2, gp * page, KV_LORA), F32), pltpu.VMEM((2, gp * page, ROPE_DIM), F32),
                        pltpu.SemaphoreType.DMA((2,)), pltpu.SemaphoreType.DMA((2,)),
                        pltpu.VMEM((rows, 1), F32), pltpu.VMEM((rows, 1), F32), pltpu.VMEM((rows, KV_LORA), F32)])
    return pl.pallas_call(kern, out_shape=jax.ShapeDtypeStruct((dbsz, rows, KV_LORA), F32), grid_spec=grid_spec,
                          compiler_params=_cparams(1), name="mla_decode")(
        page_table.reshape(-1), q_s, knew, cache_ckv, cache_kpe)


def _gla_constants(c):
    t = np.arange(c)
    blocks = [(t[None, :] <= t[:, None]), (t[None, :] > t[:, None])]
    masks = []
    m = c // 2
    while m >= 1:
        bd = (t // (2 * m)) * 2 * m + m
        upper = t >= bd
        a = upper[:, None] & (t[None, :] >= bd[:, None]) & (t[None, :] <= t[:, None])
        b = (~upper)[:, None] & (t[None, :] > t[:, None]) & (t[None, :] <= bd[:, None] - 1)
        blocks += [a, b]
        same = (t[:, None] // (2 * m)) == (t[None, :] // (2 * m))
        masks.append(same & upper[:, None] & (~upper)[None, :])
        m //= 2
    masks.append(t[:, None] == t[None, :])
    sel = np.concatenate(blocks, axis=0).astype(np.float32)
    masks = np.concatenate([np.tile(mm, (1, GLA_H)) for mm in masks], axis=0).astype(np.float32)
    r = np.arange(GLA_H * c)
    kmask = ((r[:, None] // c) == (np.arange(128)[None, :] // GLA_DK)).astype(np.float32)
    vmask = ((r[:, None] // c) == (np.arange(GLA_W)[None, :] // GLA_DV)).astype(np.float32)
    smask = ((np.arange(128)[:, None] // GLA_DK) == (np.arange(GLA_W)[None, :] // GLA_DV)).astype(np.float32)
    last = np.zeros((c, GLA_W), np.float32)
    last[c - 1, :] = 1.0
    gsum = ((np.arange(GLA_W)[:, None] // GLA_DV) == (np.arange(GLA_W)[None, :] // GLA_DV)).astype(np.float32)
    return sel, masks, kmask, vmask, smask, last, gsum


def _gla_kernel(*refs, c, nsub, nsteps, has_init):
    if has_init:
        (q_ref, k_ref, v_ref, go_ref, g_ref, s0_ref, sel_ref, masks_ref, kmask_ref, vmask_ref, smask_ref, last_ref,
         gsum_ref, ggla_ref, y_ref, sout_ref, s_ref) = refs
    else:
        (q_ref, k_ref, v_ref, go_ref, g_ref, sel_ref, masks_ref, kmask_ref, vmask_ref, smask_ref, last_ref,
         gsum_ref, ggla_ref, y_ref, sout_ref, s_ref) = refs
        s0_ref = None
    j = pl.program_id(1)
    nlev = int(np.log2(c))
    smask = smask_ref[...]

    @pl.when(j == 0)
    def _():
        if has_init:
            s0 = s0_ref[0]
            s_ref[...] = jnp.concatenate([s0] * GLA_H, axis=1) * smask
        else:
            s_ref[...] = jnp.zeros(s_ref.shape, F32)

    def chunk(i, carry):
        sl = pl.ds(pl.multiple_of(i * c, c), c)
        q = q_ref[sl, :] * (GLA_DK ** -0.5)
        k = k_ref[sl, :]
        v = v_ref[sl, :].astype(BF16)
        g_hi, g_lo = _split_bf16(g_ref[sl, :])
        sel = sel_ref[...]
        e_all = jnp.exp(_dot(sel, g_hi) + _dot(sel, g_lo))
        eb = e_all[0:c]
        s_old = s_ref[...]
        o = _dot((q * eb).astype(BF16), s_old.astype(BF16))
        kmask = kmask_ref[...]
        att = None
        for lev in range(nlev + 1):
            if lev < nlev:
                ql = (q * e_all[(2 + 2 * lev) * c:(3 + 2 * lev) * c]).astype(BF16)
                kl = k * e_all[(3 + 2 * lev) * c:(4 + 2 * lev) * c]
            else:
                ql, kl = q.astype(BF16), k
            kb = (jnp.concatenate([kl] * GLA_H, axis=0) * kmask).astype(BF16)
            part = _dot_nt(ql, kb) * masks_ref[lev * c:(lev + 1) * c, :]
            att = part if att is None else att + part
        vb = (jnp.concatenate([v] * GLA_H, axis=0) * vmask_ref[...].astype(BF16))
        o = o + _dot(att.astype(BF16), vb)
        kk = (k * e_all[c:2 * c]).astype(BF16)
        eb_hi, eb_lo = _split_bf16(eb)
        last = last_ref[...].astype(BF16)
        decay = _dot_tn(eb_hi, last) + _dot_tn(eb_lo, last)
        s_ref[...] = (decay * s_old + _dot_tn(kk, v)) * smask
        o2_hi, o2_lo = _split_bf16(o * o)
        gsum = gsum_ref[...]
        ms = (_dot(o2_hi, gsum) + _dot(o2_lo, gsum)) * (1.0 / GLA_DV)
        yc = o * lax.rsqrt(ms + EPS) * ggla_ref[...] * _silu(go_ref[sl, :])
        y_ref[sl, :] = yc.astype(BF16)
        return carry

    lax.fori_loop(0, nsub, chunk, 0)

    @pl.when(j == nsteps - 1)
    def _():
        s = s_ref[...]
        acc = s[:, 0:GLA_DV]
        for h in range(1, GLA_H):
            acc = acc + s[:, GLA_DV * h:GLA_DV * (h + 1)]
        sout_ref[0] = acc


def _gla(gq, gk, gv, go, glog, s0, bsz, w, chunk):
    n = gq.shape[0]
    t = n // bsz
    c = min(chunk, t)
    assert t % c == 0 and c & (c - 1) == 0 and c >= 8
    nsub = _pick(t // c, (8, 4, 2, 1))
    tc = c * nsub
    nsteps = t // tc
    consts = [jnp.asarray(a) for a in _gla_constants(c)]
    consts[0] = consts[0].astype(BF16)
    consts[6] = consts[6].astype(BF16)
    has_init = s0 is not None
    kern = functools.partial(_gla_kernel, c=c, nsub=nsub, nsteps=nsteps, has_init=has_init)
    row = lambda b, j: (b * nsteps + j, 0)
    full = lambda b, j: (0, 0)
    ins = [gq, gk, gv, go, glog]
    in_specs = [pl.BlockSpec((tc, 128), row), pl.BlockSpec((tc, 128), row), pl.BlockSpec((tc, 256), row),
                pl.BlockSpec((tc, 256), row), pl.BlockSpec((tc, 128), row)]
    if has_init:
        ins.append(s0)
        in_specs.append(pl.BlockSpec((1, 128, GLA_DV), lambda b, j: (b, 0, 0)))
    ins += consts + [w["g_gla"]]
    in_specs += [pl.BlockSpec(a.shape, full) for a in consts] + [pl.BlockSpec(w["g_gla"].shape, full)]
    return pl.pallas_call(
        kern,
        out_shape=[jax.ShapeDtypeStruct((n, GLA_W), BF16), jax.ShapeDtypeStruct((bsz, 128, GLA_DV), F32)],
        grid=(bsz, nsteps), in_specs=in_specs,
        out_specs=[pl.BlockSpec((tc, GLA_W), row), pl.BlockSpec((1, 128, GLA_DV), lambda b, j: (b, 0, 0))],
        scratch_shapes=[pltpu.VMEM((128, GLA_W), F32)],
        compiler_params=_cparams(2), name="gla")(*ins)


def _memkv_kernel(m_ref, g_ref, wk_ref, wv_ref, mk_ref, mv_ref):
    h = _rms(m_ref[...], g_ref[...]).astype(BF16)
    mk_ref[...] = _dot(h, wk_ref[...])
    mv_ref[...] = _dot(h, wv_ref[...])


def _memkv(mem, w):
    n = mem.shape[0]
    tm = _pick(n, (256, 128, 64, 32, 16, 8))
    row = lambda i: (i, 0)
    full = lambda i: (0, 0)
    return pl.pallas_call(
        _memkv_kernel,
        out_shape=[jax.ShapeDtypeStruct((n, X_W), F32)] * 2, grid=(n // tm,),
        in_specs=[pl.BlockSpec((tm, D_MODEL), row), pl.BlockSpec((1, D_MODEL), full),
                  pl.BlockSpec((D_MODEL, X_W), full), pl.BlockSpec((D_MODEL, X_W), full)],
        out_specs=[pl.BlockSpec((tm, X_W), row)] * 2,
        compiler_params=_cparams(1), name="memkv")(mem, w["g_mem"], w["w_xk"], w["w_xv"])


def _post_kernel(x_ref, ya_ref, ol_ref, yc_ref, mk_ref, mv_ref, wuv_ref, woa_ref, wob_ref, woc_ref, gx_ref, wxq_ref, wxo_ref,
                 gffn_ref, wrh_ref, wrl_ref, br_ref, ltri_ref,
                 x2_ref, h3_ref, info_ref, cnt_ref, carry_ref, *, nb, rpb):
    first = jnp.logical_and(pl.program_id(0) == 0, pl.program_id(1) == 0)

    @pl.when(first)
    def _():
        carry_ref[...] = jnp.zeros(carry_ref.shape, F32)

    tm = x_ref.shape[0]
    yb = _dot(ol_ref[...], wuv_ref[...]).astype(BF16)
    x1 = x_ref[...] + _dot(ya_ref[...], woa_ref[...]) + _dot(yb, wob_ref[...]) + _dot(yc_ref[...], woc_ref[...])
    q = _dot(_rms(x1, gx_ref[...]).astype(BF16), wxq_ref[...])
    kk = mk_ref[...].reshape(nb * N_MEM, X_W).astype(BF16)
    vv = mv_ref[...].reshape(nb * N_MEM, X_W).astype(BF16)
    outs = []
    for h in range(X_H):
        sl = slice(X_HD * h, X_HD * (h + 1))
        s = _dot_nt(q[:, sl].astype(BF16), kk[:, sl]) * (X_HD ** -0.5)
        if nb > 1:
            r = lax.broadcasted_iota(I32, s.shape, 0)
            c = lax.broadcasted_iota(I32, s.shape, 1)
            s = jnp.where(r // rpb == c // N_MEM, s, NEG)
        e = jnp.exp(s - jnp.max(s, axis=-1, keepdims=True))
        p = e / jnp.sum(e, axis=-1, keepdims=True)
        outs.append(_dot(p.astype(BF16), vv[:, sl]))
    o = jnp.concatenate(outs, axis=1).astype(BF16)
    x2 = x1 + _dot(o, wxo_ref[...])
    x2_ref[...] = x2
    h3 = _rms(x2, gffn_ref[...])
    h3_ref[...] = h3
    h_hi, h_lo = _split_bf16(h3)
    logit = _dot(h_hi, wrh_ref[...]) + _dot(h_lo, wrh_ref[...]) + _dot(h_hi, wrl_ref[...]) + br_ref[...]
    lane = lax.broadcasted_iota(I32, logit.shape, 1)
    gl = jnp.where(lane < N_GROUPS, logit, NEG)
    gmax = jnp.max(gl, axis=-1, keepdims=True)
    gsel = jnp.min(jnp.where(gl == gmax, lane, 1 << 20), axis=-1, keepdims=True)
    gw = 1.0 / jnp.sum(jnp.exp(gl - gmax), axis=-1, keepdims=True)
    emask = jnp.logical_and(lane >= EXPERT_LANE0, (lane - EXPERT_LANE0) // EXP_PER_GROUP == gsel)
    el = jnp.where(emask, logit, NEG)
    pe = jnp.where(emask, jnp.exp(el - jnp.max(el, axis=-1, keepdims=True)), 0.0)
    prob = pe / jnp.sum(pe, axis=-1, keepdims=True)
    prob = jnp.where(emask, prob, -1.0)
    p1 = jnp.max(prob, axis=-1, keepdims=True)
    i1 = jnp.min(jnp.where(prob == p1, lane, 1 << 20), axis=-1, keepdims=True)
    prob2 = jnp.where(lane == i1, -1.0, prob)
    p2 = jnp.max(prob2, axis=-1, keepdims=True)
    i2 = jnp.min(jnp.where(prob2 == p2, lane, 1 << 20), axis=-1, keepdims=True)
    gate1 = gw * p1 / (p1 + p2)
    gate2 = gw * p2 / (p1 + p2)
    oh1 = (lane == i1).astype(F32)
    oh2 = (lane == i2).astype(F32)
    both = oh1 + oh2
    before = _dot(ltri_ref[...], both.astype(BF16)) + carry_ref[...]
    rank1 = jnp.sum(oh1 * before, axis=-1, keepdims=True)
    rank2 = jnp.sum(oh2 * before, axis=-1, keepdims=True)
    carry = carry_ref[...] + jnp.sum(both, axis=0, keepdims=True)
    carry_ref[...] = carry
    cnt_ref[...] = carry
    info = jnp.where(lane == 0, (i1 - EXPERT_LANE0).astype(F32), 0.0)
    info = jnp.where(lane == 1, (i2 - EXPERT_LANE0).astype(F32), info)
    info = jnp.where(lane == 2, gate1, info)
    info = jnp.where(lane == 3, gate2, info)
    info = jnp.where(lane == 4, rank1, info)
    info = jnp.where(lane == 5, rank2, info)
    info_ref[...] = info


def _post(x, ya, olat, yc, mk, mv, bsz, w, kv_seq0=0):
    n = x.shape[0]
    t = n // bsz
    if t >= 16:
        tm = _pick(t, (256, 128, 64, 32, 16))
        nb, rpb = 1, tm
        grid = (bsz, t // tm)
        row = lambda b, j: (b * (t // tm) + j, 0)
        kvm = lambda b, j: (kv_seq0 + b, 0, 0)
    else:
        nb = _pick(bsz, (8, 4, 2, 1))
        tm, rpb = nb * t, t
        assert tm % 16 == 0 and kv_seq0 % nb == 0
        grid = (bsz // nb, 1)
        row = lambda b, j: (b, 0)
        kvm = lambda b, j: (kv_seq0 // nb + b, 0, 0)
    ltri = jnp.asarray(np.tril(np.ones((tm, tm), np.float32), -1)).astype(BF16)
    full = lambda b, j: (0, 0)
    wnames = ["w_uvbd", "wo_a", "wo_b", "wo_c", "g_x", "w_xq", "w_xo", "g_ffn", "w_r_hi", "w_r_lo", "b_r"]
    ins = [x, ya, olat, yc, mk, mv] + [w[k] for k in wnames] + [ltri]
    in_specs = [pl.BlockSpec((tm, D_MODEL), row), pl.BlockSpec((tm, POOL_W), row), pl.BlockSpec((tm, MLA_H * KV_LORA), row),
                pl.BlockSpec((tm, GLA_W), row), pl.BlockSpec((nb, N_MEM, X_W), kvm), pl.BlockSpec((nb, N_MEM, X_W), kvm)]
    in_specs += [pl.BlockSpec(w[k].shape, full) for k in wnames] + [pl.BlockSpec((tm, tm), full)]
    kern = functools.partial(_post_kernel, nb=nb, rpb=rpb)
    return pl.pallas_call(
        kern,
        out_shape=[jax.ShapeDtypeStruct((n, D_MODEL), F32), jax.ShapeDtypeStruct((n, D_MODEL), F32),
                   jax.ShapeDtypeStruct((n, 128), F32), jax.ShapeDtypeStruct((1, 128), F32)],
        grid=grid, in_specs=in_specs,
        out_specs=[pl.BlockSpec((tm, D_MODEL), row), pl.BlockSpec((tm, D_MODEL), row), pl.BlockSpec((tm, 128), row),
                   pl.BlockSpec((1, 128), full)],
        scratch_shapes=[pltpu.VMEM((1, 128), F32)],
        compiler_params=_cparams(2), name="post")(*ins)


def _dispatch_kernel(pstart_ref, eid_ref, rank_ref, h_ref, init_ref, out_ref, sem, *, tm):
    del init_ref
    base = pl.program_id(0) * tm

    def copy(n, k):
        s = 2 * n + k
        d = pstart_ref[eid_ref[s]] + rank_ref[s]
        return pltpu.make_async_copy(h_ref.at[pl.ds(base + n, 1)], out_ref.at[pl.ds(d, 1)], sem)

    def start(n, c):
        copy(n, 0).start()
        copy(n, 1).start()
        return c

    def wait(n, c):
        copy(n, 0).wait()
        copy(n, 1).wait()
        return c

    lax.fori_loop(0, tm, start, 0)
    lax.fori_loop(0, tm, wait, 0)


def _dispatch(h3, eid, rank, pstart, n_rows):
    n = h3.shape[0]
    tm = _pick(n, (256, 128, 64, 32, 16, 8))
    init = jnp.zeros((n_rows, D_MODEL), F32)
    grid_spec = pltpu.PrefetchScalarGridSpec(
        num_scalar_prefetch=1, grid=(n // tm,),
        in_specs=[pl.BlockSpec((2 * tm,), lambda i, ps: (i,), memory_space=pltpu.SMEM),
                  pl.BlockSpec((2 * tm,), lambda i, ps: (i,), memory_space=pltpu.SMEM),
                  pl.BlockSpec(memory_space=pl.ANY), pl.BlockSpec(memory_space=pl.ANY)],
        out_specs=pl.BlockSpec(memory_space=pl.ANY),
        scratch_shapes=[pltpu.SemaphoreType.DMA(())])
    return pl.pallas_call(functools.partial(_dispatch_kernel, tm=tm),
                          out_shape=jax.ShapeDtypeStruct((n_rows, D_MODEL), F32), grid_spec=grid_spec,
                          input_output_aliases={4: 0}, compiler_params=_cparams(1), name="moe_dispatch")(
        pstart, eid, rank, h3, init)


def _expert_kernel(be_ref, nu_ref, x_ref, wg_ref, wu_ref, wd_ref, o_ref):
    j = pl.program_id(0)

    @pl.when(j < nu_ref[0])
    def _():
        xb = x_ref[...].astype(BF16)
        hid = _silu(_dot(xb, wg_ref[...])) * _dot(xb, wu_ref[...])
        o_ref[...] = _dot(hid.astype(BF16), wd_ref[...])

    @pl.when(j >= nu_ref[0])
    def _():
        o_ref[...] = jnp.zeros(o_ref.shape, F32)


def _experts(rows, block_e, n_used, w):
    n_rows = rows.shape[0]
    nblk = n_rows // MOE_BLOCK
    grid_spec = pltpu.PrefetchScalarGridSpec(
        num_scalar_prefetch=2, grid=(nblk,),
        in_specs=[pl.BlockSpec((MOE_BLOCK, D_MODEL), lambda j, be, nu: (j, 0)),
                  pl.BlockSpec((None, D_MODEL, D_EXPERT), lambda j, be, nu: (be[j], 0, 0)),
                  pl.BlockSpec((None, D_MODEL, D_EXPERT), lambda j, be, nu: (be[j], 0, 0)),
                  pl.BlockSpec((None, D_EXPERT, D_MODEL), lambda j, be, nu: (be[j], 0, 0))],
        out_specs=pl.BlockSpec((MOE_BLOCK, D_MODEL), lambda j, be, nu: (j, 0)))
    return pl.pallas_call(_expert_kernel, out_shape=jax.ShapeDtypeStruct((n_rows, D_MODEL), F32), grid_spec=grid_spec,
                          compiler_params=_cparams(1), name="moe_experts")(block_e, n_used, rows, w["w_eg"], w["w_eu"], w["w_ed"])


def _combine_kernel(pstart_ref, eid_ref, rank_ref, x_ref, info_ref, g_ref, y_hbm, o_ref, buf, sem, *, tm, final):
    def copy(n, k):
        s = 2 * n + k
        d = pstart_ref[eid_ref[s]] + rank_ref[s]
        return pltpu.make_async_copy(y_hbm.at[pl.ds(d, 1)], buf.at[k, pl.ds(n, 1)], sem)

    def start(n, c):
        copy(n, 0).start()
        copy(n, 1).start()
        return c

    def wait(n, c):
        copy(n, 0).wait()
        copy(n, 1).wait()
        return c

    lax.fori_loop(0, tm, start, 0)
    lax.fori_loop(0, tm, wait, 0)
    info = info_ref[...]
    y = x_ref[...] + info[:, 2:3] * buf[0] + info[:, 3:4] * buf[1]
    if final:
        y = _rms(y, g_ref[...])
    o_ref[...] = y


def _combine(x2, info, eid, rank, pstart, y_rows, g_final, final):
    n = x2.shape[0]
    tm = _pick(n, (256, 128, 64, 32, 16, 8))
    grid_spec = pltpu.PrefetchScalarGridSpec(
        num_scalar_prefetch=1, grid=(n // tm,),
        in_specs=[pl.BlockSpec((2 * tm,), lambda i, ps: (i,), memory_space=pltpu.SMEM),
                  pl.BlockSpec((2 * tm,), lambda i, ps: (i,), memory_space=pltpu.SMEM),
                  pl.BlockSpec((tm, D_MODEL), lambda i, ps: (i, 0)),
                  pl.BlockSpec((tm, 128), lambda i, ps: (i, 0)),
                  pl.BlockSpec((1, D_MODEL), lambda i, ps: (0, 0)),
                  pl.BlockSpec(memory_space=pl.ANY)],
        out_specs=pl.BlockSpec((tm, D_MODEL), lambda i, ps: (i, 0)),
        scratch_shapes=[pltpu.VMEM((2, tm, D_MODEL), F32), pltpu.SemaphoreType.DMA(())])
    return pl.pallas_call(functools.partial(_combine_kernel, tm=tm, final=final),
                          out_shape=jax.ShapeDtypeStruct((n, D_MODEL), F32), grid_spec=grid_spec,
                          compiler_params=_cparams(1), name="moe_combine")(pstart, eid, rank, x2, info, g_final, y_rows)


def _moe(x2, h3, info, counts, w, g_final, final):
    n = x2.shape[0]
    nblk = -(-(n * TOP_K) // MOE_BLOCK) + N_EXPERTS
    eid = info[:, 0:2].astype(I32).reshape(-1)
    rank = info[:, 4:6].astype(I32).reshape(-1)
    cnt = counts[0, EXPERT_LANE0:EXPERT_LANE0 + N_EXPERTS].astype(I32)
    pcnt = (cnt + MOE_BLOCK - 1) // MOE_BLOCK * MOE_BLOCK
    pend = jnp.cumsum(pcnt)
    pstart = (pend - pcnt).astype(I32)
    block_e = jnp.minimum(jnp.searchsorted(pend, jnp.arange(nblk, dtype=I32) * MOE_BLOCK, side="right"), N_EXPERTS - 1).astype(I32)
    n_used = (pend[-1:] // MOE_BLOCK).astype(I32)
    rows = _dispatch(h3, eid, rank, pstart, nblk * MOE_BLOCK)
    y_rows = _experts(rows, block_e, n_used, w)
    return _combine(x2, info, eid, rank, pstart, y_rows, g_final, final)


def _rope_tables(pos):
    half = ROPE_DIM // 2
    inv = ROPE_BASE ** (-jnp.arange(half, dtype=F32) * 2.0 / ROPE_DIM)
    ang = pos.astype(F32)[:, None] * inv[None, :]
    c, s = jnp.cos(ang), jnp.sin(ang)
    cos32 = jnp.concatenate([c, c], axis=1)
    sin32 = jnp.concatenate([-s, s], axis=1)
    return jnp.tile(cos32, (1, MLA_H)), jnp.tile(sin32, (1, MLA_H))


def _prep_layer(l, p):
    w_in = p["w_in"][l]
    zeros80 = jnp.zeros((D_MODEL, 80), F32)
    w_in_p = jnp.concatenate([w_in[:, 0:640], w_in[:, 640:672], w_in[:, 1184:1200], zeros80,
                              w_in[:, 672:1184], w_in[:, 1200:1456]], axis=1).astype(BF16)
    assert w_in_p.shape[1] == IN_PAD
    w_uq = p["w_uq"][l].reshape(Q_LORA, MLA_H, NOPE + ROPE_DIM)
    w_uq_p = jnp.concatenate([w_uq[:, :, :NOPE].reshape(Q_LORA, MLA_H * NOPE),
                              w_uq[:, :, NOPE:].reshape(Q_LORA, MLA_H * ROPE_DIM)], axis=1).astype(BF16)
    eye_h = jnp.eye(MLA_H, dtype=F32)
    w_ukbd = jnp.einsum("hnc,hg->hngc", p["w_uk"][l].transpose(1, 2, 0), eye_h).reshape(MLA_H * NOPE, MLA_H * KV_LORA).astype(BF16)
    w_uvbd = jnp.einsum("hcv,hg->hcgv", p["w_uv"][l].transpose(1, 0, 2), eye_h).reshape(MLA_H * KV_LORA, MLA_H * V_DIM).astype(BF16)
    w_gk2 = jnp.zeros((128, 128), F32).at[ROPE_DIM:ROPE_DIM + GATE_RANK, :].set(p["w_gk2"][l]).astype(BF16)
    w_poolbd = jnp.einsum("gcd,gk->gckd", p["w_pool"][l], jnp.eye(POOL_GROUPS, dtype=F32)).reshape(POOL_W, POOL_W).astype(BF16)
    w_out = p["w_out"][l].astype(BF16)
    w_r = jnp.zeros((D_MODEL, 128), F32).at[:, 0:N_GROUPS].set(p["w_rg"][l]).at[:, EXPERT_LANE0:EXPERT_LANE0 + N_EXPERTS].set(p["w_re"][l])
    w_r_hi = w_r.astype(BF16)
    w_r_lo = (w_r - w_r_hi.astype(F32)).astype(BF16)
    b_r = jnp.zeros((1, 128), F32).at[0, 0:N_GROUPS].set(p["b_rg"][l]).at[0, EXPERT_LANE0:EXPERT_LANE0 + N_EXPERTS].set(p["b_re"][l])
    return {
        "g_mix": p["g_mix"][l][None, :], "w_in": w_in_p, "g_qn": p["g_qn"][l][None, :], "w_uq": w_uq_p, "w_ukbd": w_ukbd,
        "g_kvn": p["g_kvn"][l][None, :], "w_gk2": w_gk2, "b_gk": p["b_gk"][l][None, :],
        "w_poolbd": w_poolbd, "pool_scale": p["pool_scale"][l][None, :],
        "g_gla": jnp.tile(p["g_gla"][l], GLA_H)[None, :],
        "w_uvbd": w_uvbd, "wo_a": w_out[0:256], "wo_b": w_out[256:768], "wo_c": w_out[768:1024],
        "g_x": p["g_x"][l][None, :], "w_xq": p["w_xq"][l].astype(BF16), "w_xo": p["w_xo"][l].astype(BF16),
        "g_mem": p["g_mem"][l][None, :], "w_xk": p["w_xk"][l].astype(BF16), "w_xv": p["w_xv"][l].astype(BF16),
        "g_ffn": p["g_ffn"][l][None, :], "w_r_hi": w_r_hi, "w_r_lo": w_r_lo, "b_r": b_r,
        "w_eg": p["w_eg"][l].astype(BF16), "w_eu": p["w_eu"][l].astype(BF16), "w_ed": p["w_ed"][l].astype(BF16),
    }


GLA_CHUNK = 64


def kernel(x_prompt, x_sample, mem_prompt, cache_ckv, cache_kpe, page_table, state_pool, state_gla, cache_mem_k, cache_mem_v, g_mix, w_in, w_pool, pool_scale, g_qn, w_uq, g_kvn, w_uk, w_uv, w_gk2, b_gk, g_gla, w_out, g_x, g_mem, w_xq, w_xk, w_xv, w_xo, g_ffn, w_rg, b_rg, w_re, b_re, w_eg, w_eu, w_ed, g_final):
    params = dict(g_mix=g_mix, w_in=w_in, w_pool=w_pool, pool_scale=pool_scale, g_qn=g_qn, w_uq=w_uq, g_kvn=g_kvn, w_uk=w_uk,
                  w_uv=w_uv, w_gk2=w_gk2, b_gk=b_gk, g_gla=g_gla, w_out=w_out, g_x=g_x, g_mem=g_mem, w_xq=w_xq, w_xk=w_xk,
                  w_xv=w_xv, w_xo=w_xo, g_ffn=g_ffn, w_rg=w_rg, b_rg=b_rg, w_re=w_re, b_re=b_re, w_eg=w_eg, w_eu=w_eu, w_ed=w_ed)
    depth = w_in.shape[0]
    bsz, seq, _ = x_prompt.shape
    dbsz, dseq, _ = x_sample.shape
    past_len = page_table.shape[1] * cache_ckv.shape[2]
    gfin = g_final[None, :]

    cos_p, sin_p = _rope_tables(jnp.arange(seq, dtype=I32))
    cos_s, sin_s = _rope_tables(past_len + (jnp.arange(dbsz * dseq, dtype=I32) % dseq))

    xp = x_prompt.reshape(bsz * seq, D_MODEL)
    xs = x_sample.reshape(dbsz * dseq, D_MODEL)
    mem = mem_prompt.reshape(bsz * N_MEM, D_MODEL)
    gpad = 8 - dseq
    assert 0 <= gpad < 8

    outs = {k: [] for k in ("ckv_p", "kpe_p", "pool_p", "gla_p", "mk_p", "mv_p", "ckv_s", "kpe_s", "pool_s", "gla_s")}
    for l in range(depth):
        w = _prep_layer(l, params)
        final = l == depth - 1
        mk, mv = _memkv(mem, w)
        u, q_hm, kvb, ckv, kpe, gq, gk, gv, go, glog = _inproj(xp, cos_p, sin_p, w)
        ya, pool_new = _pool_prompt(u, bsz, w)
        olat = _mla_prompt(q_hm, kvb, bsz)
        yc, s_new = _gla(gq, gk, gv, go, glog, None, bsz, w, GLA_CHUNK)
        x2, h3, info, counts = _post(xp, ya, olat, yc, mk.reshape(bsz, N_MEM, X_W), mv.reshape(bsz, N_MEM, X_W), bsz, w)
        xp = _moe(x2, h3, info, counts, w, gfin, final)
        outs["ckv_p"].append(ckv.reshape(bsz, seq, KV_LORA))
        outs["kpe_p"].append(kpe.reshape(bsz, seq, ROPE_DIM))
        outs["pool_p"].append(pool_new)
        outs["gla_p"].append(s_new.reshape(bsz, GLA_H, GLA_DK, GLA_DV))
        outs["mk_p"].append(mk.reshape(bsz, N_MEM, X_H, X_HD))
        outs["mv_p"].append(mv.reshape(bsz, N_MEM, X_H, X_HD))
        u, q_hm, kvb, ckv, kpe, gq, gk, gv, go, glog = _inproj(xs, cos_s, sin_s, w)
        ue_tm = jnp.concatenate([state_pool[l], u.reshape(dbsz, dseq, POOL_W)], axis=1).transpose(1, 0, 2)
        ya_tm, st_tm = _pool_sample(ue_tm, w)
        ya = ya_tm.transpose(1, 0, 2).reshape(dbsz * dseq, POOL_W)
        q_s = q_hm.reshape(MLA_H, dbsz, dseq, QK_PAD).transpose(1, 0, 2, 3).reshape(dbsz, MLA_H * dseq, QK_PAD)
        knew = jnp.pad(kvb.reshape(dbsz, dseq, QK_PAD), ((0, 0), (0, 16 - dseq), (0, 0)))
        o_s = _mla_decode(q_s, knew, cache_ckv, cache_kpe, page_table, l)
        olat = o_s.reshape(dbsz, MLA_H, dseq, KV_LORA).transpose(0, 2, 1, 3).reshape(dbsz * dseq, MLA_H * KV_LORA).astype(BF16)

        def pad8(a):
            return jnp.pad(a.reshape(dbsz, dseq, -1), ((0, 0), (0, gpad), (0, 0))).reshape(dbsz * 8, -1)

        yc8, s_new = _gla(pad8(gq), pad8(gk), pad8(gv), pad8(go), pad8(glog),
                          state_gla[l].reshape(dbsz, GLA_H * GLA_DK, GLA_DV), dbsz, w, 8)
        yc = yc8.reshape(dbsz, 8, GLA_W)[:, :dseq].reshape(dbsz * dseq, GLA_W)
        x2, h3, info, counts = _post(xs, ya, olat, yc, cache_mem_k.reshape(depth * dbsz, N_MEM, X_W),
                                     cache_mem_v.reshape(depth * dbsz, N_MEM, X_W), dbsz, w, kv_seq0=l * dbsz)
        xs = _moe(x2, h3, info, counts, w, gfin, final)
        outs["ckv_s"].append(ckv.reshape(dbsz, dseq, KV_LORA))
        outs["kpe_s"].append(kpe.reshape(dbsz, dseq, ROPE_DIM))
        outs["pool_s"].append(st_tm.transpose(1, 0, 2))
        outs["gla_s"].append(s_new.reshape(dbsz, GLA_H, GLA_DK, GLA_DV))

    st = lambda k: jnp.stack(outs[k])
    return (xp.reshape(bsz, seq, D_MODEL), xs.reshape(dbsz, dseq, D_MODEL),
            st("ckv_p"), st("kpe_p"), st("pool_p"), st("gla_p"), st("mk_p"), st("mv_p"),
            st("ckv_s"), st("kpe_s"), st("pool_s"), st("gla_s"))
```

```python
import functools

import numpy as np
import jax
import jax.numpy as jnp
from jax import lax
from jax.experimental import pallas as pl
from jax.experimental.pallas import tpu as pltpu

F32 = jnp.float32
BF16 = jnp.bfloat16
I32 = jnp.int32

EPS = 1e-6
D_MODEL = 1024
POOL_GROUPS, POOL_GC = 4, 64
POOL_W = POOL_GROUPS * POOL_GC
POOL_WINDOWS = (2, 4, 8, 16)
POOL_BUF = 15
MLA_H, Q_LORA, KV_LORA, NOPE, ROPE_DIM, V_DIM = 8, 256, 128, 64, 32, 64
ROPE_BASE = 10000.0
MLA_SCALE = (NOPE + ROPE_DIM) ** -0.5
Q_PRESCALE = MLA_SCALE * 1.4426950408889634
GLA_H, GLA_DK, GLA_DV, GATE_RANK, GATE_TAU = 4, 32, 64, 16, 16.0
GLA_W = GLA_H * GLA_DV
N_MEM, X_H, X_HD = 256, 4, 128
X_W = X_H * X_HD
N_GROUPS, EXP_PER_GROUP, N_EXPERTS, TOP_K, D_EXPERT = 4, 8, 32, 2, 256
QK_PAD = 256
ONES_LANE = KV_LORA + ROPE_DIM
IN_PAD = 1536
EXPERT_LANE0 = 32
MOE_BLOCK = 256
NEG = -1e30
VMEM_LIMIT = 56 * 1024 * 1024


def _pick(n, prefs):
    for p in prefs:
        if n % p == 0:
            return p
    raise ValueError(f"no tile in {prefs} divides {n}")


def _cparams(n_axes):
    return pltpu.CompilerParams(dimension_semantics=("arbitrary",) * n_axes, vmem_limit_bytes=VMEM_LIMIT)


def _rms(x, g):
    ms = jnp.mean(x * x, axis=-1, keepdims=True)
    return x * lax.rsqrt(ms + EPS) * g


def _dot(a, b):
    return jnp.dot(a, b, preferred_element_type=F32)


def _dot_nt(a, b):
    return lax.dot_general(a, b, (((1,), (1,)), ((), ())), preferred_element_type=F32)


def _dot_tn(a, b):
    return lax.dot_general(a, b, (((0,), (0,)), ((), ())), preferred_element_type=F32)


def _split_bf16(x):
    hi = x.astype(BF16)
    lo = (x - hi.astype(F32)).astype(BF16)
    return hi, lo


def _swap16(x):
    w = x.shape[-1]
    lane = lax.broadcasted_iota(I32, x.shape, x.ndim - 1)
    first = (lane & 31) < 16
    return jnp.where(first, pltpu.roll(x, w - 16, x.ndim - 1), pltpu.roll(x, 16, x.ndim - 1))


def _silu(x):
    return x / (1.0 + jnp.exp(-x))


def _inproj_kernel(x_ref, gmix_ref, win_ref, cos_ref, sin_ref, gqn_ref, wuq_ref, wuk_ref, gkvn_ref, wgk_ref, bgk_ref,
                   u_ref, q_ref, kvb_ref, ckv_ref, kpe_ref, gq_ref, gk_ref, gv_ref, go_ref, glog_ref):
    h = _rms(x_ref[...], gmix_ref[...]).astype(BF16)
    y = _dot(h, win_ref[...])
    u_ref[...] = y[:, 0:256]
    gq_ref[...] = y[:, 768:896]
    gk_ref[...] = y[:, 896:1024]
    gv_ref[...] = y[:, 1024:1280]
    go_ref[...] = y[:, 1280:1536]
    cos = cos_ref[...]
    sin = sin_ref[...]
    ckv = _rms(y[:, 512:640], gkvn_ref[...])
    grp = y[:, 640:768]
    grp_r = grp * cos[:, 0:128] + _swap16(grp) * sin[:, 0:128]
    ckv_ref[...] = ckv
    kpe_ref[...] = grp_r[:, 0:ROPE_DIM]
    lane = lax.broadcasted_iota(I32, grp_r.shape, 1)
    kvb_ref[:, 0:128] = ckv.astype(BF16)
    kvb_ref[:, 128:256] = jnp.where(lane < ROPE_DIM, grp_r, jnp.where(lane == ROPE_DIM, 1.0, 0.0)).astype(BF16)
    gl = _dot(grp.astype(BF16), wgk_ref[...]) + bgk_ref[...]
    glog_ref[...] = (jnp.minimum(gl, 0.0) - jnp.log(1.0 + jnp.exp(-jnp.abs(gl)))) * (1.0 / GATE_TAU)
    cqn = _rms(y[:, 256:512], gqn_ref[...]).astype(BF16)
    q = _dot(cqn, wuq_ref[...])
    qlat = _dot(q[:, 0:512].astype(BF16), wuk_ref[...]) * Q_PRESCALE
    qr = q[:, 512:768]
    qr = (qr * cos + _swap16(qr) * sin) * Q_PRESCALE
    for hh in range(MLA_H):
        q_ref[hh, :, 0:128] = qlat[:, 128 * hh:128 * hh + 128].astype(BF16)
        col = qr[:, 128 * (hh // 4):128 * (hh // 4) + 128]
        sh = (128 - 32 * (hh % 4)) % 128
        if sh:
            col = pltpu.roll(col, sh, 1)
        q_ref[hh, :, 128:256] = jnp.where(lane < ROPE_DIM, col, 0.0).astype(BF16)


def _inproj(x, cos_t, sin_t, w):
    n = x.shape[0]
    tm = _pick(n, (512, 256, 128, 64, 32, 16))
    tab_tiles = cos_t.shape[0] // tm
    assert cos_t.shape[0] % tm == 0
    row = lambda i: (i, 0)
    tab = lambda i: (i % tab_tiles, 0)
    full = lambda i: (0, 0)
    wspec = lambda a: pl.BlockSpec(a.shape, full)
    outs = [
        jax.ShapeDtypeStruct((n, 256), F32),
        jax.ShapeDtypeStruct((MLA_H, n, QK_PAD), BF16),
        jax.ShapeDtypeStruct((n, QK_PAD), BF16),
        jax.ShapeDtypeStruct((n, KV_LORA), F32),
        jax.ShapeDtypeStruct((n, ROPE_DIM), F32),
        jax.ShapeDtypeStruct((n, 128), F32),
        jax.ShapeDtypeStruct((n, 128), F32),
        jax.ShapeDtypeStruct((n, 256), F32),
        jax.ShapeDtypeStruct((n, 256), F32),
        jax.ShapeDtypeStruct((n, 128), F32),
    ]
    out_specs = [
        pl.BlockSpec((tm, 256), row),
        pl.BlockSpec((MLA_H, tm, QK_PAD), lambda i: (0, i, 0)),
        pl.BlockSpec((tm, QK_PAD), row),
        pl.BlockSpec((tm, KV_LORA), row),
        pl.BlockSpec((tm, ROPE_DIM), row),
        pl.BlockSpec((tm, 128), row),
        pl.BlockSpec((tm, 128), row),
        pl.BlockSpec((tm, 256), row),
        pl.BlockSpec((tm, 256), row),
        pl.BlockSpec((tm, 128), row),
    ]
    ins = [x, w["g_mix"], w["w_in"], cos_t, sin_t, w["g_qn"], w["w_uq"], w["w_ukbd"], w["g_kvn"], w["w_gk2"], w["b_gk"]]
    in_specs = [pl.BlockSpec((tm, D_MODEL), row), wspec(w["g_mix"]), wspec(w["w_in"]),
                pl.BlockSpec((tm, 256), tab), pl.BlockSpec((tm, 256), tab),
                wspec(w["g_qn"]), wspec(w["w_uq"]), wspec(w["w_ukbd"]), wspec(w["g_kvn"]), wspec(w["w_gk2"]), wspec(w["b_gk"])]
    return pl.pallas_call(_inproj_kernel, out_shape=outs, grid=(n // tm,), in_specs=in_specs, out_specs=out_specs,
                          compiler_params=_cparams(1), name="inproj")(*ins)


def _pool_mix(sums, u, cnts, wbd, scale):
    cols = []
    for c in range(2):
        wa, wb = POOL_WINDOWS[2 * c], POOL_WINDOWS[2 * c + 1]
        sa, sb = sums[(c, wa)], sums[(c, wb)]
        lane = lax.broadcasted_iota(I32, sa.shape, 1)
        pooled = jnp.where(lane < POOL_GC, sa / cnts[wa], sb / cnts[wb]) - u[:, 128 * c:128 * c + 128]
        cols.append(pooled)
    pooled = jnp.concatenate(cols, axis=1).astype(BF16)
    return (_dot(pooled, wbd) * scale).astype(BF16)


def _pool_kernel(u_ref, wbd_ref, scale_ref, ya_ref, st_ref, ue_ref, *, tp, nt):
    j = pl.program_id(1)

    @pl.when(j == 0)
    def _():
        ue_ref[0:16, :] = jnp.zeros((16, POOL_W), F32)

    @pl.when(j > 0)
    def _():
        ue_ref[0:16, :] = ue_ref[tp:tp + 16, :]

    u = u_ref[...]
    ue_ref[16:16 + tp, :] = u
    t = j * tp + lax.broadcasted_iota(I32, (tp, 1), 0)
    cnts = {w: jnp.minimum(t + 1, w).astype(F32) for w in POOL_WINDOWS}
    sums = {}
    for c in range(2):
        wa, wb = POOL_WINDOWS[2 * c], POOL_WINDOWS[2 * c + 1]
        acc = None
        for k in range(wb):
            sl = ue_ref[16 - k:16 - k + tp, 128 * c:128 * c + 128]
            acc = sl if acc is None else acc + sl
            if k + 1 == wa:
                sums[(c, wa)] = acc
        sums[(c, wb)] = acc
    ya_ref[...] = _pool_mix(sums, u, cnts, wbd_ref[...], scale_ref[...])

    @pl.when(j == nt - 1)
    def _():
        st_ref[0] = ue_ref[tp + 1:tp + 16, :]


def _pool_prompt(u, bsz, w):
    n = u.shape[0]
    t = n // bsz
    tp = _pick(t, (512, 256, 128, 64, 32, 16))
    nt = t // tp
    kern = functools.partial(_pool_kernel, tp=tp, nt=nt)
    return pl.pallas_call(
        kern,
        out_shape=[jax.ShapeDtypeStruct((n, POOL_W), BF16), jax.ShapeDtypeStruct((bsz, POOL_BUF, POOL_W), F32)],
        grid=(bsz, nt),
        in_specs=[pl.BlockSpec((tp, POOL_W), lambda b, j: (b * nt + j, 0)),
                  pl.BlockSpec((POOL_W, POOL_W), lambda b, j: (0, 0)),
                  pl.BlockSpec((1, POOL_W), lambda b, j: (0, 0))],
        out_specs=[pl.BlockSpec((tp, POOL_W), lambda b, j: (b * nt + j, 0)),
                   pl.BlockSpec((1, POOL_BUF, POOL_W), lambda b, j: (b, 0, 0))],
        scratch_shapes=[pltpu.VMEM((tp + 16, POOL_W), F32)],
        compiler_params=_cparams(2), name="pool_prompt")(u, w["w_poolbd"], w["pool_scale"])


def _pool_step_kernel(ue_ref, wbd_ref, scale_ref, ya_ref, st_ref, *, tt):
    cnts = {w: jnp.float32(w) for w in POOL_WINDOWS}
    for t in range(tt):
        sums = {}
        for c in range(2):
            wa, wb = POOL_WINDOWS[2 * c], POOL_WINDOWS[2 * c + 1]
            acc = None
            for k in range(wb):
                sl = ue_ref[POOL_BUF + t - k, :, 128 * c:128 * c + 128]
                acc = sl if acc is None else acc + sl
                if k + 1 == wa:
                    sums[(c, wa)] = acc
            sums[(c, wb)] = acc
        ya_ref[t] = _pool_mix(sums, ue_ref[POOL_BUF + t], cnts, wbd_ref[...], scale_ref[...])
    for r in range(POOL_BUF):
        st_ref[r] = ue_ref[tt + r]


def _pool_sample(ue_tm, w):
    rows, bsz, _ = ue_tm.shape
    tt = rows - POOL_BUF
    kern = functools.partial(_pool_step_kernel, tt=tt)
    full3 = lambda i: (0, 0, 0)
    return pl.pallas_call(
        kern,
        out_shape=[jax.ShapeDtypeStruct((tt, bsz, POOL_W), BF16), jax.ShapeDtypeStruct((POOL_BUF, bsz, POOL_W), F32)],
        grid=(1,),
        in_specs=[pl.BlockSpec(ue_tm.shape, full3), pl.BlockSpec((POOL_W, POOL_W), lambda i: (0, 0)),
                  pl.BlockSpec((1, POOL_W), lambda i: (0, 0))],
        out_specs=[pl.BlockSpec((tt, bsz, POOL_W), full3), pl.BlockSpec((POOL_BUF, bsz, POOL_W), full3)],
        compiler_params=_cparams(1), name="pool_sample")(ue_tm, w["w_poolbd"], w["pool_scale"])


def _mla_prompt_kernel(qi_ref, kj_ref, q_ref, k_ref, o_ref, m_ref, acc_ref, *, tq, tk):
    p = pl.program_id(1)
    qi = qi_ref[p]
    kj = kj_ref[p]
    rows = MLA_H * tq

    @pl.when(kj == 0)
    def _():
        m_ref[...] = jnp.full((rows, 128), NEG, F32)
        acc_ref[...] = jnp.zeros((rows, QK_PAD), F32)

    def step(masked):
        q = q_ref[...].reshape(rows, QK_PAD)
        k = k_ref[...]
        s = _dot_nt(q, k)
        if masked:
            r = lax.broadcasted_iota(I32, (rows, tk), 0)
            c = lax.broadcasted_iota(I32, (rows, tk), 1)
            qpos = qi * tq + (r & (tq - 1))
            s = jnp.where(kj * tk + c <= qpos, s, NEG)
        m_old = m_ref[...]
        m_new = jnp.maximum(m_old, jnp.max(s, axis=-1, keepdims=True))
        alpha = jnp.exp2(m_old - m_new)
        pr = jnp.exp2(s - jnp.concatenate([m_new] * (tk // 128), axis=1))
        acc_ref[...] = jnp.concatenate([alpha, alpha], axis=1) * acc_ref[...] + _dot(pr.astype(BF16), k)
        m_ref[...] = m_new

    crosses = (kj + 1) * tk - 1 > qi * tq

    @pl.when(crosses)
    def _():
        step(True)

    @pl.when(jnp.logical_not(crosses))
    def _():
        step(False)

    @pl.when(kj == ((qi + 1) * tq - 1) // tk)
    def _():
        acc = acc_ref[...]
        o = acc[:, 0:KV_LORA] / acc[:, ONES_LANE:ONES_LANE + 1]
        for h in range(MLA_H):
            o_ref[:, KV_LORA * h:KV_LORA * (h + 1)] = o[h * tq:(h + 1) * tq].astype(BF16)


def _mla_prompt(q_hm, kvb, bsz):
    n = kvb.shape[0]
    t = n // bsz
    tq = _pick(t, (256, 128, 64, 32, 16))
    tk = _pick(t, (512, 256, 128, 64, 32, 16))
    assert tq & (tq - 1) == 0
    nq, nk = t // tq, t // tk
    pairs = [(i, j) for i in range(nq) for j in range(((i + 1) * tq - 1) // tk + 1)]
    qi = jnp.asarray(np.array([p[0] for p in pairs], np.int32))
    kj = jnp.asarray(np.array([p[1] for p in pairs], np.int32))
    rows = MLA_H * tq
    kern = functools.partial(_mla_prompt_kernel, tq=tq, tk=tk)
    grid_spec = pltpu.PrefetchScalarGridSpec(
        num_scalar_prefetch=2, grid=(bsz, len(pairs)),
        in_specs=[pl.BlockSpec((MLA_H, tq, QK_PAD), lambda b, p, qi, kj: (0, b * nq + qi[p], 0)),
                  pl.BlockSpec((tk, QK_PAD), lambda b, p, qi, kj: (b * nk + kj[p], 0))],
        out_specs=pl.BlockSpec((tq, MLA_H * KV_LORA), lambda b, p, qi, kj: (b * nq + qi[p], 0)),
        scratch_shapes=[pltpu.VMEM((rows, 128), F32), pltpu.VMEM((rows, QK_PAD), F32)])
    return pl.pallas_call(kern, out_shape=jax.ShapeDtypeStruct((n, MLA_H * KV_LORA), BF16), grid_spec=grid_spec,
                          compiler_params=_cparams(2), name="mla_prompt")(qi, kj, q_hm, kvb)


def _mla_decode_kernel(pt_ref, q_ref, knew_ref, cckv_ref, ckpe_ref, o_ref,
                       cbuf, pbuf, csem, psem, m_ref, l_ref, acc_ref, *, layer, ng, gp, page, tt, total):
    s_idx = pl.program_id(0)
    grp = s_idx % ng
    slot = s_idx % 2

    def copies(step, slot_):
        out = []
        for g in range(gp):
            pg = pt_ref[step * gp + g]
            out.append(pltpu.make_async_copy(cckv_ref.at[layer, pg], cbuf.at[slot_, pl.ds(g * page, page)], csem.at[slot_]))
            out.append(pltpu.make_async_copy(ckpe_ref.at[layer, pg], pbuf.at[slot_, :, pl.ds(g * page, page)], psem.at[slot_]))
        return out

    @pl.when(s_idx == 0)
    def _():
        for c in copies(0, 0):
            c.start()

    @pl.when(s_idx + 1 < total)
    def _():
        for c in copies(s_idx + 1, 1 - slot):
            c.start()

    @pl.when(grp == 0)
    def _():
        m_ref[...] = jnp.full(m_ref.shape, NEG, F32)
        l_ref[...] = jnp.zeros(l_ref.shape, F32)
        acc_ref[...] = jnp.zeros(acc_ref.shape, F32)

    for c in copies(s_idx, slot):
        c.wait()

    q = q_ref[0]

    def update(s, v):
        m_old = m_ref[...]
        m_new = jnp.maximum(m_old, jnp.max(s, axis=-1, keepdims=True))
        alpha = jnp.exp2(m_old - m_new)
        pr = jnp.exp2(s - m_new)
        l_ref[...] = alpha * l_ref[...] + jnp.sum(pr, axis=-1, keepdims=True)
        acc_ref[...] = alpha * acc_ref[...] + _dot(pr.astype(BF16), v)
        m_ref[...] = m_new

    kc = cbuf[slot].astype(BF16)
    kr_t = pbuf[slot].astype(BF16)
    s = _dot_nt(q[:, 0:KV_LORA], kc) + _dot(q[:, KV_LORA:KV_LORA + ROPE_DIM], kr_t)
    update(s, kc)

    @pl.when(grp == ng - 1)
    def _():
        kn = knew_ref[0]
        sn = _dot_nt(q, kn)
        r = lax.broadcasted_iota(I32, sn.shape, 0)
        c = lax.broadcasted_iota(I32, sn.shape, 1)
        sn = jnp.where(c <= (r % tt), sn, NEG)
        update(sn, kn[:, 0:KV_LORA])
        o_ref[0] = acc_ref[...] / l_ref[...]


def _mla_decode(q_s, knew, cache_ckv, cache_kpe_t, page_table, layer):
    dbsz, rows, _ = q_s.shape
    tt = rows // MLA_H
    n_pages = page_table.shape[1]
    page = cache_ckv.shape[2]
    gp = _pick(n_pages, (32, 16, 8, 4, 2, 1))
    ng = n_pages // gp
    total = dbsz * ng
    kern = functools.partial(_mla_decode_kernel, layer=layer, ng=ng, gp=gp, page=page, tt=tt, total=total)
    grid_spec = pltpu.PrefetchScalarGridSpec(
        num_scalar_prefetch=1, grid=(total,),
        in_specs=[pl.BlockSpec((1, rows, QK_PAD), lambda s, pt: (s // ng, 0, 0)),
                  pl.BlockSpec((1, 16, QK_PAD), lambda s, pt: (s // ng, 0, 0)),
                  pl.BlockSpec(memory_space=pl.ANY), pl.BlockSpec(memory_space=pl.ANY)],
        out_specs=pl.BlockSpec((1, rows, KV_LORA), lambda s, pt: (s // ng, 0, 0)),
        scratch_shapes=[pltpu.VMEM((2, gp * page, KV_LORA), F32), pltpu.VMEM((2, ROPE_DIM, gp * page), F32),
                        pltpu.SemaphoreType.DMA((2,)), pltpu.SemaphoreType.DMA((2,)),
                        pltpu.VMEM((rows, 1), F32), pltpu.VMEM((rows, 1), F32), pltpu.VMEM((rows, KV_LORA), F32)])
    return pl.pallas_call(kern, out_shape=jax.ShapeDtypeStruct((dbsz, rows, KV_LORA), F32), grid_spec=grid_spec,
                          compiler_params=_cparams(1), name="mla_decode")(
        page_table.reshape(-1), q_s, knew, cache_ckv, cache_kpe_t)


def _gla_constants(c):
    t = np.arange(c)
    blocks = [(t[None, :] <= t[:, None]), (t[None, :] > t[:, None])]
    masks = []
    m = c // 2
    while m >= 1:
        bd = (t // (2 * m)) * 2 * m + m
        upper = t >= bd
        a = upper[:, None] & (t[None, :] >= bd[:, None]) & (t[None, :] <= t[:, None])
        b = (~upper)[:, None] & (t[None, :] > t[:, None]) & (t[None, :] <= bd[:, None] - 1)
        blocks += [a, b]
        same = (t[:, None] // (2 * m)) == (t[None, :] // (2 * m))
        masks.append(same & upper[:, None] & (~upper)[None, :])
        m //= 2
    masks.append(t[:, None] == t[None, :])
    sel = np.concatenate(blocks, axis=0).astype(np.float32)
    masks = np.concatenate([np.tile(mm, (1, GLA_H)) for mm in masks], axis=0).astype(np.float32)
    r = np.arange(GLA_H * c)
    kmask = ((r[:, None] // c) == (np.arange(128)[None, :] // GLA_DK)).astype(np.float32)
    vmask = ((r[:, None] // c) == (np.arange(GLA_W)[None, :] // GLA_DV)).astype(np.float32)
    smask = ((np.arange(128)[:, None] // GLA_DK) == (np.arange(GLA_W)[None, :] // GLA_DV)).astype(np.float32)
    last = np.zeros((c, GLA_W), np.float32)
    last[c - 1, :] = 1.0
    gsum = ((np.arange(GLA_W)[:, None] // GLA_DV) == (np.arange(GLA_W)[None, :] // GLA_DV)).astype(np.float32)
    return sel, masks, kmask, vmask, smask, last, gsum


def _gla_kernel(*refs, c, nsub, nsteps, has_init):
    if has_init:
        (q_ref, k_ref, v_ref, go_ref, g_ref, s0_ref, sel_ref, masks_ref, kmask_ref, vmask_ref, smask_ref, last_ref,
         gsum_ref, ggla_ref, y_ref, sout_ref, s_ref) = refs
    else:
        (q_ref, k_ref, v_ref, go_ref, g_ref, sel_ref, masks_ref, kmask_ref, vmask_ref, smask_ref, last_ref,
         gsum_ref, ggla_ref, y_ref, sout_ref, s_ref) = refs
        s0_ref = None
    j = pl.program_id(1)
    nlev = int(np.log2(c))
    smask = smask_ref[...]

    @pl.when(j == 0)
    def _():
        if has_init:
            s0 = s0_ref[0]
            s_ref[...] = jnp.concatenate([s0] * GLA_H, axis=1) * smask
        else:
            s_ref[...] = jnp.zeros(s_ref.shape, F32)

    def chunk(i, carry):
        sl = pl.ds(pl.multiple_of(i * c, c), c)
        q = q_ref[sl, :] * (GLA_DK ** -0.5)
        k = k_ref[sl, :]
        v = v_ref[sl, :].astype(BF16)
        g_hi, g_lo = _split_bf16(g_ref[sl, :])
        sel = sel_ref[...]
        e_all = jnp.exp(_dot(sel, g_hi) + _dot(sel, g_lo))
        eb = e_all[0:c]
        s_old = s_ref[...]
        o = _dot((q * eb).astype(BF16), s_old.astype(BF16))
        kmask = kmask_ref[...]
        att = None
        for lev in range(nlev + 1):
            if lev < nlev:
                ql = (q * e_all[(2 + 2 * lev) * c:(3 + 2 * lev) * c]).astype(BF16)
                kl = k * e_all[(3 + 2 * lev) * c:(4 + 2 * lev) * c]
            else:
                ql, kl = q.astype(BF16), k
            kb = (jnp.concatenate([kl] * GLA_H, axis=0) * kmask).astype(BF16)
            part = _dot_nt(ql, kb) * masks_ref[lev * c:(lev + 1) * c, :]
            att = part if att is None else att + part
        vb = (jnp.concatenate([v] * GLA_H, axis=0) * vmask_ref[...].astype(BF16))
        o = o + _dot(att.astype(BF16), vb)
        kk = (k * e_all[c:2 * c]).astype(BF16)
        eb_hi, eb_lo = _split_bf16(eb)
        last = last_ref[...].astype(BF16)
        decay = _dot_tn(eb_hi, last) + _dot_tn(eb_lo, last)
        s_ref[...] = (decay * s_old + _dot_tn(kk, v)) * smask
        o2_hi, o2_lo = _split_bf16(o * o)
        gsum = gsum_ref[...]
        ms = (_dot(o2_hi, gsum) + _dot(o2_lo, gsum)) * (1.0 / GLA_DV)
        yc = o * lax.rsqrt(ms + EPS) * ggla_ref[...] * _silu(go_ref[sl, :])
        y_ref[sl, :] = yc.astype(BF16)
        return carry

    lax.fori_loop(0, nsub, chunk, 0)

    @pl.when(j == nsteps - 1)
    def _():
        s = s_ref[...]
        acc = s[:, 0:GLA_DV]
        for h in range(1, GLA_H):
            acc = acc + s[:, GLA_DV * h:GLA_DV * (h + 1)]
        sout_ref[0] = acc


def _gla(gq, gk, gv, go, glog, s0, bsz, w, chunk):
    n = gq.shape[0]
    t = n // bsz
    c = min(chunk, t)
    assert t % c == 0 and c & (c - 1) == 0 and c >= 8
    nsub = _pick(t // c, (8, 4, 2, 1))
    tc = c * nsub
    nsteps = t // tc
    consts = [jnp.asarray(a) for a in _gla_constants(c)]
    consts[0] = consts[0].astype(BF16)
    consts[6] = consts[6].astype(BF16)
    has_init = s0 is not None
    kern = functools.partial(_gla_kernel, c=c, nsub=nsub, nsteps=nsteps, has_init=has_init)
    row = lambda b, j: (b * nsteps + j, 0)
    full = lambda b, j: (0, 0)
    ins = [gq, gk, gv, go, glog]
    in_specs = [pl.BlockSpec((tc, 128), row), pl.BlockSpec((tc, 128), row), pl.BlockSpec((tc, 256), row),
                pl.BlockSpec((tc, 256), row), pl.BlockSpec((tc, 128), row)]
    if has_init:
        ins.append(s0)
        in_specs.append(pl.BlockSpec((1, 128, GLA_DV), lambda b, j: (b, 0, 0)))
    ins += consts + [w["g_gla"]]
    in_specs += [pl.BlockSpec(a.shape, full) for a in consts] + [pl.BlockSpec(w["g_gla"].shape, full)]
    return pl.pallas_call(
        kern,
        out_shape=[jax.ShapeDtypeStruct((n, GLA_W), BF16), jax.ShapeDtypeStruct((bsz, 128, GLA_DV), F32)],
        grid=(bsz, nsteps), in_specs=in_specs,
        out_specs=[pl.BlockSpec((tc, GLA_W), row), pl.BlockSpec((1, 128, GLA_DV), lambda b, j: (b, 0, 0))],
        scratch_shapes=[pltpu.VMEM((128, GLA_W), F32)],
        compiler_params=_cparams(2), name="gla")(*ins)


def _memkv_kernel(m_ref, g_ref, wk_ref, wv_ref, mk_ref, mv_ref):
    h = _rms(m_ref[...], g_ref[...]).astype(BF16)
    mk_ref[...] = _dot(h, wk_ref[...])
    mv_ref[...] = _dot(h, wv_ref[...])


def _memkv(mem, w):
    n = mem.shape[0]
    tm = _pick(n, (256, 128, 64, 32, 16, 8))
    row = lambda i: (i, 0)
    full = lambda i: (0, 0)
    return pl.pallas_call(
        _memkv_kernel,
        out_shape=[jax.ShapeDtypeStruct((n, X_W), F32)] * 2, grid=(n // tm,),
        in_specs=[pl.BlockSpec((tm, D_MODEL), row), pl.BlockSpec((1, D_MODEL), full),
                  pl.BlockSpec((D_MODEL, X_W), full), pl.BlockSpec((D_MODEL, X_W), full)],
        out_specs=[pl.BlockSpec((tm, X_W), row)] * 2,
        compiler_params=_cparams(1), name="memkv")(mem, w["g_mem"], w["w_xk"], w["w_xv"])


def _post_kernel(x_ref, ya_ref, ol_ref, yc_ref, mk_ref, mv_ref, wuv_ref, woa_ref, wob_ref, woc_ref, gx_ref, wxq_ref, wxo_ref,
                 gffn_ref, wrh_ref, wrl_ref, br_ref, ltri_ref,
                 x2_ref, h3_ref, info_ref, cnt_ref, carry_ref, *, nb, rpb):
    first = jnp.logical_and(pl.program_id(0) == 0, pl.program_id(1) == 0)

    @pl.when(first)
    def _():
        carry_ref[...] = jnp.zeros(carry_ref.shape, F32)

    tm = x_ref.shape[0]
    yb = _dot(ol_ref[...], wuv_ref[...]).astype(BF16)
    x1 = x_ref[...] + _dot(ya_ref[...], woa_ref[...]) + _dot(yb, wob_ref[...]) + _dot(yc_ref[...], woc_ref[...])
    q = _dot(_rms(x1, gx_ref[...]).astype(BF16), wxq_ref[...])
    def head_kv(ref, h):
        if len(ref.shape) == 4:
            a = ref[:, :, h, :]
        else:
            a = ref[:, :, X_HD * h:X_HD * (h + 1)]
        return a.reshape(nb * N_MEM, X_HD).astype(BF16)

    outs = []
    for h in range(X_H):
        sl = slice(X_HD * h, X_HD * (h + 1))
        s = _dot_nt(q[:, sl].astype(BF16), head_kv(mk_ref, h)) * (X_HD ** -0.5)
        if nb > 1:
            r = lax.broadcasted_iota(I32, s.shape, 0)
            c = lax.broadcasted_iota(I32, s.shape, 1)
            s = jnp.where(r // rpb == c // N_MEM, s, NEG)
        e = jnp.exp(s - jnp.max(s, axis=-1, keepdims=True))
        p = e / jnp.sum(e, axis=-1, keepdims=True)
        outs.append(_dot(p.astype(BF16), head_kv(mv_ref, h)))
    o = jnp.concatenate(outs, axis=1).astype(BF16)
    x2 = x1 + _dot(o, wxo_ref[...])
    x2_ref[...] = x2
    h3 = _rms(x2, gffn_ref[...])
    h3_ref[...] = h3
    h_hi, h_lo = _split_bf16(h3)
    logit = _dot(h_hi, wrh_ref[...]) + _dot(h_lo, wrh_ref[...]) + _dot(h_hi, wrl_ref[...]) + br_ref[...]
    lane = lax.broadcasted_iota(I32, logit.shape, 1)
    gl = jnp.where(lane < N_GROUPS, logit, NEG)
    gmax = jnp.max(gl, axis=-1, keepdims=True)
    gsel = jnp.min(jnp.where(gl == gmax, lane, 1 << 20), axis=-1, keepdims=True)
    gw = 1.0 / jnp.sum(jnp.exp(gl - gmax), axis=-1, keepdims=True)
    emask = jnp.logical_and(lane >= EXPERT_LANE0, (lane - EXPERT_LANE0) // EXP_PER_GROUP == gsel)
    el = jnp.where(emask, logit, NEG)
    pe = jnp.where(emask, jnp.exp(el - jnp.max(el, axis=-1, keepdims=True)), 0.0)
    prob = pe / jnp.sum(pe, axis=-1, keepdims=True)
    prob = jnp.where(emask, prob, -1.0)
    p1 = jnp.max(prob, axis=-1, keepdims=True)
    i1 = jnp.min(jnp.where(prob == p1, lane, 1 << 20), axis=-1, keepdims=True)
    prob2 = jnp.where(lane == i1, -1.0, prob)
    p2 = jnp.max(prob2, axis=-1, keepdims=True)
    i2 = jnp.min(jnp.where(prob2 == p2, lane, 1 << 20), axis=-1, keepdims=True)
    gate1 = gw * p1 / (p1 + p2)
    gate2 = gw * p2 / (p1 + p2)
    oh1 = (lane == i1).astype(F32)
    oh2 = (lane == i2).astype(F32)
    both = oh1 + oh2
    before = _dot(ltri_ref[...], both.astype(BF16)) + carry_ref[...]
    rank1 = jnp.sum(oh1 * before, axis=-1, keepdims=True)
    rank2 = jnp.sum(oh2 * before, axis=-1, keepdims=True)
    carry = carry_ref[...] + jnp.sum(both, axis=0, keepdims=True)
    carry_ref[...] = carry
    cnt_ref[...] = carry
    info = jnp.where(lane == 0, (i1 - EXPERT_LANE0).astype(F32), 0.0)
    info = jnp.where(lane == 1, (i2 - EXPERT_LANE0).astype(F32), info)
    info = jnp.where(lane == 2, gate1, info)
    info = jnp.where(lane == 3, gate2, info)
    info = jnp.where(lane == 4, rank1, info)
    info = jnp.where(lane == 5, rank2, info)
    info_ref[...] = info


def _post(x, ya, olat, yc, mk, mv, bsz, w, kv_seq0=0):
    n = x.shape[0]
    t = n // bsz
    if t >= 16:
        tm = _pick(t, (512, 256, 128, 64, 32, 16))
        nb, rpb = 1, tm
        grid = (bsz, t // tm)
        row = lambda b, j: (b * (t // tm) + j, 0)
        kvm = lambda b, j: (kv_seq0 + b, 0, 0)
    else:
        nb = _pick(bsz, (8, 4, 2, 1))
        tm, rpb = nb * t, t
        assert tm % 16 == 0 and kv_seq0 % nb == 0
        grid = (bsz // nb, 1)
        row = lambda b, j: (b, 0)
        kvm = lambda b, j: (kv_seq0 // nb + b, 0, 0)
    ltri = jnp.asarray(np.tril(np.ones((tm, tm), np.float32), -1)).astype(BF16)
    full = lambda b, j: (0, 0)
    wnames = ["w_uvbd", "wo_a", "wo_b", "wo_c", "g_x", "w_xq", "w_xo", "g_ffn", "w_r_hi", "w_r_lo", "b_r"]
    ins = [x, ya, olat, yc, mk, mv] + [w[k] for k in wnames] + [ltri]
    in_specs = [pl.BlockSpec((tm, D_MODEL), row), pl.BlockSpec((tm, POOL_W), row), pl.BlockSpec((tm, MLA_H * KV_LORA), row),
                pl.BlockSpec((tm, GLA_W), row)]
    if mk.ndim == 4:
        kvm4 = lambda b, j: kvm(b, j) + (0,)
        in_specs += [pl.BlockSpec((nb, N_MEM, X_H, X_HD), kvm4)] * 2
    else:
        in_specs += [pl.BlockSpec((nb, N_MEM, X_W), kvm)] * 2
    in_specs += [pl.BlockSpec(w[k].shape, full) for k in wnames] + [pl.BlockSpec((tm, tm), full)]
    kern = functools.partial(_post_kernel, nb=nb, rpb=rpb)
    return pl.pallas_call(
        kern,
        out_shape=[jax.ShapeDtypeStruct((n, D_MODEL), F32), jax.ShapeDtypeStruct((n, D_MODEL), F32),
                   jax.ShapeDtypeStruct((n, 128), F32), jax.ShapeDtypeStruct((1, 128), F32)],
        grid=grid, in_specs=in_specs,
        out_specs=[pl.BlockSpec((tm, D_MODEL), row), pl.BlockSpec((tm, D_MODEL), row), pl.BlockSpec((tm, 128), row),
                   pl.BlockSpec((1, 128), full)],
        scratch_shapes=[pltpu.VMEM((1, 128), F32)],
        compiler_params=_cparams(2), name="post")(*ins)


def _dispatch_kernel(pstart_ref, eid_ref, rank_ref, h_ref, init_ref, out_ref, sem, *, tm):
    del init_ref

    def copy(n, k):
        s = 2 * n + k
        d = pstart_ref[eid_ref[s]] + rank_ref[s]
        return pltpu.make_async_copy(h_ref.at[pl.ds(n, 1)], out_ref.at[pl.ds(d, 1)], sem)

    def start(n, c):
        copy(n, 0).start()
        copy(n, 1).start()
        return c

    def wait(n, c):
        copy(n, 0).wait()
        copy(n, 1).wait()
        return c

    lax.fori_loop(0, tm, start, 0)
    lax.fori_loop(0, tm, wait, 0)


def _dispatch(h3, eid, rank, pstart, n_rows):
    n = h3.shape[0]
    tm = _pick(n, (256, 128, 64, 32, 16, 8))
    init = jnp.zeros((n_rows, D_MODEL), F32)
    grid_spec = pltpu.PrefetchScalarGridSpec(
        num_scalar_prefetch=1, grid=(n // tm,),
        in_specs=[pl.BlockSpec((2 * tm,), lambda i, ps: (i,), memory_space=pltpu.SMEM),
                  pl.BlockSpec((2 * tm,), lambda i, ps: (i,), memory_space=pltpu.SMEM),
                  pl.BlockSpec((tm, D_MODEL), lambda i, ps: (i, 0)), pl.BlockSpec(memory_space=pl.ANY)],
        out_specs=pl.BlockSpec(memory_space=pl.ANY),
        scratch_shapes=[pltpu.SemaphoreType.DMA(())])
    return pl.pallas_call(functools.partial(_dispatch_kernel, tm=tm),
                          out_shape=jax.ShapeDtypeStruct((n_rows, D_MODEL), F32), grid_spec=grid_spec,
                          input_output_aliases={4: 0}, compiler_params=_cparams(1), name="moe_dispatch")(
        pstart, eid, rank, h3, init)


def _expert_kernel(be_ref, nu_ref, x_ref, wg_ref, wu_ref, wd_ref, o_ref):
    j = pl.program_id(0)

    @pl.when(j < nu_ref[0])
    def _():
        xb = x_ref[...].astype(BF16)
        hid = _silu(_dot(xb, wg_ref[...])) * _dot(xb, wu_ref[...])
        o_ref[...] = _dot(hid.astype(BF16), wd_ref[...])

    @pl.when(j >= nu_ref[0])
    def _():
        o_ref[...] = jnp.zeros(o_ref.shape, F32)


def _experts(rows, block_e, n_used, w):
    n_rows = rows.shape[0]
    nblk = n_rows // MOE_BLOCK
    grid_spec = pltpu.PrefetchScalarGridSpec(
        num_scalar_prefetch=2, grid=(nblk,),
        in_specs=[pl.BlockSpec((MOE_BLOCK, D_MODEL), lambda j, be, nu: (j, 0)),
                  pl.BlockSpec((None, D_MODEL, D_EXPERT), lambda j, be, nu: (be[j], 0, 0)),
                  pl.BlockSpec((None, D_MODEL, D_EXPERT), lambda j, be, nu: (be[j], 0, 0)),
                  pl.BlockSpec((None, D_EXPERT, D_MODEL), lambda j, be, nu: (be[j], 0, 0))],
        out_specs=pl.BlockSpec((MOE_BLOCK, D_MODEL), lambda j, be, nu: (j, 0)))
    return pl.pallas_call(_expert_kernel, out_shape=jax.ShapeDtypeStruct((n_rows, D_MODEL), F32), grid_spec=grid_spec,
                          compiler_params=_cparams(1), name="moe_experts")(block_e, n_used, rows, w["w_eg"], w["w_eu"], w["w_ed"])


def _combine_kernel(pstart_ref, eid_ref, rank_ref, x_ref, info_ref, g_ref, y_hbm, o_ref, buf, sem, *, tm, final):
    def copy(n, k):
        s = 2 * n + k
        d = pstart_ref[eid_ref[s]] + rank_ref[s]
        return pltpu.make_async_copy(y_hbm.at[pl.ds(d, 1)], buf.at[k, pl.ds(n, 1)], sem)

    def start(n, c):
        copy(n, 0).start()
        copy(n, 1).start()
        return c

    def wait(n, c):
        copy(n, 0).wait()
        copy(n, 1).wait()
        return c

    lax.fori_loop(0, tm, start, 0)
    lax.fori_loop(0, tm, wait, 0)
    info = info_ref[...]
    y = x_ref[...] + info[:, 2:3] * buf[0] + info[:, 3:4] * buf[1]
    if final:
        y = _rms(y, g_ref[...])
    o_ref[...] = y


def _combine(x2, info, eid, rank, pstart, y_rows, g_final, final):
    n = x2.shape[0]
    tm = _pick(n, (256, 128, 64, 32, 16, 8))
    grid_spec = pltpu.PrefetchScalarGridSpec(
        num_scalar_prefetch=1, grid=(n // tm,),
        in_specs=[pl.BlockSpec((2 * tm,), lambda i, ps: (i,), memory_space=pltpu.SMEM),
                  pl.BlockSpec((2 * tm,), lambda i, ps: (i,), memory_space=pltpu.SMEM),
                  pl.BlockSpec((tm, D_MODEL), lambda i, ps: (i, 0)),
                  pl.BlockSpec((tm, 128), lambda i, ps: (i, 0)),
                  pl.BlockSpec((1, D_MODEL), lambda i, ps: (0, 0)),
                  pl.BlockSpec(memory_space=pl.ANY)],
        out_specs=pl.BlockSpec((tm, D_MODEL), lambda i, ps: (i, 0)),
        scratch_shapes=[pltpu.VMEM((2, tm, D_MODEL), F32), pltpu.SemaphoreType.DMA(())])
    return pl.pallas_call(functools.partial(_combine_kernel, tm=tm, final=final),
                          out_shape=jax.ShapeDtypeStruct((n, D_MODEL), F32), grid_spec=grid_spec,
                          compiler_params=_cparams(1), name="moe_combine")(pstart, eid, rank, x2, info, g_final, y_rows)


def _moe(x2, h3, info, counts, w, g_final, final):
    n = x2.shape[0]
    nblk = -(-(n * TOP_K) // MOE_BLOCK) + N_EXPERTS
    eid = info[:, 0:2].astype(I32).reshape(-1)
    rank = info[:, 4:6].astype(I32).reshape(-1)
    cnt = counts[0, EXPERT_LANE0:EXPERT_LANE0 + N_EXPERTS].astype(I32)
    pcnt = (cnt + MOE_BLOCK - 1) // MOE_BLOCK * MOE_BLOCK
    pend = jnp.cumsum(pcnt)
    pstart = (pend - pcnt).astype(I32)
    blk0 = jnp.arange(nblk, dtype=I32) * MOE_BLOCK
    block_e = jnp.minimum(jnp.sum((pend[None, :] <= blk0[:, None]).astype(I32), axis=1), N_EXPERTS - 1).astype(I32)
    n_used = (pend[-1:] // MOE_BLOCK).astype(I32)
    rows = _dispatch(h3, eid, rank, pstart, nblk * MOE_BLOCK)
    y_rows = _experts(rows, block_e, n_used, w)
    return _combine(x2, info, eid, rank, pstart, y_rows, g_final, final)


def _rope_tables(pos):
    half = ROPE_DIM // 2
    inv = ROPE_BASE ** (-jnp.arange(half, dtype=F32) * 2.0 / ROPE_DIM)
    ang = pos.astype(F32)[:, None] * inv[None, :]
    c, s = jnp.cos(ang), jnp.sin(ang)
    cos32 = jnp.concatenate([c, c], axis=1)
    sin32 = jnp.concatenate([-s, s], axis=1)
    return jnp.tile(cos32, (1, MLA_H)), jnp.tile(sin32, (1, MLA_H))


def _prep_layer(l, p):
    w_in = p["w_in"][l]
    zeros80 = jnp.zeros((D_MODEL, 80), F32)
    w_in_p = jnp.concatenate([w_in[:, 0:640], w_in[:, 640:672], w_in[:, 1184:1200], zeros80,
                              w_in[:, 672:1184], w_in[:, 1200:1456]], axis=1).astype(BF16)
    assert w_in_p.shape[1] == IN_PAD
    w_uq = p["w_uq"][l].reshape(Q_LORA, MLA_H, NOPE + ROPE_DIM)
    w_uq_p = jnp.concatenate([w_uq[:, :, :NOPE].reshape(Q_LORA, MLA_H * NOPE),
                              w_uq[:, :, NOPE:].reshape(Q_LORA, MLA_H * ROPE_DIM)], axis=1).astype(BF16)
    eye_h = jnp.eye(MLA_H, dtype=F32)
    w_ukbd = jnp.einsum("hnc,hg->hngc", p["w_uk"][l].transpose(1, 2, 0), eye_h).reshape(MLA_H * NOPE, MLA_H * KV_LORA).astype(BF16)
    w_uvbd = jnp.einsum("hcv,hg->hcgv", p["w_uv"][l].transpose(1, 0, 2), eye_h).reshape(MLA_H * KV_LORA, MLA_H * V_DIM).astype(BF16)
    w_gk2 = jnp.zeros((128, 128), F32).at[ROPE_DIM:ROPE_DIM + GATE_RANK, :].set(p["w_gk2"][l]).astype(BF16)
    w_poolbd = jnp.einsum("gcd,gk->gckd", p["w_pool"][l], jnp.eye(POOL_GROUPS, dtype=F32)).reshape(POOL_W, POOL_W).astype(BF16)
    w_out = p["w_out"][l].astype(BF16)
    w_r = jnp.zeros((D_MODEL, 128), F32).at[:, 0:N_GROUPS].set(p["w_rg"][l]).at[:, EXPERT_LANE0:EXPERT_LANE0 + N_EXPERTS].set(p["w_re"][l])
    w_r_hi = w_r.astype(BF16)
    w_r_lo = (w_r - w_r_hi.astype(F32)).astype(BF16)
    b_r = jnp.zeros((1, 128), F32).at[0, 0:N_GROUPS].set(p["b_rg"][l]).at[0, EXPERT_LANE0:EXPERT_LANE0 + N_EXPERTS].set(p["b_re"][l])
    return {
        "g_mix": p["g_mix"][l][None, :], "w_in": w_in_p, "g_qn": p["g_qn"][l][None, :], "w_uq": w_uq_p, "w_ukbd": w_ukbd,
        "g_kvn": p["g_kvn"][l][None, :], "w_gk2": w_gk2, "b_gk": p["b_gk"][l][None, :],
        "w_poolbd": w_poolbd, "pool_scale": p["pool_scale"][l][None, :],
        "g_gla": jnp.tile(p["g_gla"][l], GLA_H)[None, :],
        "w_uvbd": w_uvbd, "wo_a": w_out[0:256], "wo_b": w_out[256:768], "wo_c": w_out[768:1024],
        "g_x": p["g_x"][l][None, :], "w_xq": p["w_xq"][l].astype(BF16), "w_xo": p["w_xo"][l].astype(BF16),
        "g_mem": p["g_mem"][l][None, :], "w_xk": p["w_xk"][l].astype(BF16), "w_xv": p["w_xv"][l].astype(BF16),
        "g_ffn": p["g_ffn"][l][None, :], "w_r_hi": w_r_hi, "w_r_lo": w_r_lo, "b_r": b_r,
        "w_eg": p["w_eg"][l].astype(BF16), "w_eu": p["w_eu"][l].astype(BF16), "w_ed": p["w_ed"][l].astype(BF16),
    }


GLA_CHUNK = 64


def kernel(x_prompt, x_sample, mem_prompt, cache_ckv, cache_kpe, page_table, state_pool, state_gla, cache_mem_k, cache_mem_v, g_mix, w_in, w_pool, pool_scale, g_qn, w_uq, g_kvn, w_uk, w_uv, w_gk2, b_gk, g_gla, w_out, g_x, g_mem, w_xq, w_xk, w_xv, w_xo, g_ffn, w_rg, b_rg, w_re, b_re, w_eg, w_eu, w_ed, g_final):
    params = dict(g_mix=g_mix, w_in=w_in, w_pool=w_pool, pool_scale=pool_scale, g_qn=g_qn, w_uq=w_uq, g_kvn=g_kvn, w_uk=w_uk,
                  w_uv=w_uv, w_gk2=w_gk2, b_gk=b_gk, g_gla=g_gla, w_out=w_out, g_x=g_x, g_mem=g_mem, w_xq=w_xq, w_xk=w_xk,
                  w_xv=w_xv, w_xo=w_xo, g_ffn=g_ffn, w_rg=w_rg, b_rg=b_rg, w_re=w_re, b_re=b_re, w_eg=w_eg, w_eu=w_eu, w_ed=w_ed)
    depth = w_in.shape[0]
    bsz, seq, _ = x_prompt.shape
    dbsz, dseq, _ = x_sample.shape
    past_len = page_table.shape[1] * cache_ckv.shape[2]
    gfin = g_final[None, :]
    cache_kpe_t = jnp.swapaxes(cache_kpe, 2, 3)

    cos_p, sin_p = _rope_tables(jnp.arange(seq, dtype=I32))
    cos_s, sin_s = _rope_tables(past_len + (jnp.arange(dbsz * dseq, dtype=I32) % dseq))

    xp = x_prompt.reshape(bsz * seq, D_MODEL)
    xs = x_sample.reshape(dbsz * dseq, D_MODEL)
    mem = mem_prompt.reshape(bsz * N_MEM, D_MODEL)
    gpad = 8 - dseq
    assert 0 <= gpad < 8

    outs = {k: [] for k in ("ckv_p", "kpe_p", "pool_p", "gla_p", "mk_p", "mv_p", "ckv_s", "kpe_s", "pool_s", "gla_s")}
    for l in range(depth):
        w = _prep_layer(l, params)
        final = l == depth - 1
        mk, mv = _memkv(mem, w)
        u, q_hm, kvb, ckv, kpe, gq, gk, gv, go, glog = _inproj(xp, cos_p, sin_p, w)
        ya, pool_new = _pool_prompt(u, bsz, w)
        olat = _mla_prompt(q_hm, kvb, bsz)
        yc, s_new = _gla(gq, gk, gv, go, glog, None, bsz, w, GLA_CHUNK)
        x2, h3, info, counts = _post(xp, ya, olat, yc, mk.reshape(bsz, N_MEM, X_W), mv.reshape(bsz, N_MEM, X_W), bsz, w)
        xp = _moe(x2, h3, info, counts, w, gfin, final)
        outs["ckv_p"].append(ckv.reshape(bsz, seq, KV_LORA))
        outs["kpe_p"].append(kpe.reshape(bsz, seq, ROPE_DIM))
        outs["pool_p"].append(pool_new)
        outs["gla_p"].append(s_new.reshape(bsz, GLA_H, GLA_DK, GLA_DV))
        outs["mk_p"].append(mk.reshape(bsz, N_MEM, X_H, X_HD))
        outs["mv_p"].append(mv.reshape(bsz, N_MEM, X_H, X_HD))
        u, q_hm, kvb, ckv, kpe, gq, gk, gv, go, glog = _inproj(xs, cos_s, sin_s, w)
        ue_tm = jnp.concatenate([state_pool[l], u.reshape(dbsz, dseq, POOL_W)], axis=1).transpose(1, 0, 2)
        ya_tm, st_tm = _pool_sample(ue_tm, w)
        ya = ya_tm.transpose(1, 0, 2).reshape(dbsz * dseq, POOL_W)
        q_s = q_hm.reshape(MLA_H, dbsz, dseq, QK_PAD).transpose(1, 0, 2, 3).reshape(dbsz, MLA_H * dseq, QK_PAD)
        knew = jnp.pad(kvb.reshape(dbsz, dseq, QK_PAD), ((0, 0), (0, 16 - dseq), (0, 0)))
        o_s = _mla_decode(q_s, knew, cache_ckv, cache_kpe_t, page_table, l)
        olat = o_s.reshape(dbsz, MLA_H, dseq, KV_LORA).transpose(0, 2, 1, 3).reshape(dbsz * dseq, MLA_H * KV_LORA).astype(BF16)

        def pad8(a):
            return jnp.pad(a.reshape(dbsz, dseq, -1), ((0, 0), (0, gpad), (0, 0))).reshape(dbsz * 8, -1)

        yc8, s_new = _gla(pad8(gq), pad8(gk), pad8(gv), pad8(go), pad8(glog),
                          state_gla[l].reshape(dbsz, GLA_H * GLA_DK, GLA_DV), dbsz, w, 8)
        yc = yc8.reshape(dbsz, 8, GLA_W)[:, :dseq].reshape(dbsz * dseq, GLA_W)
        x2, h3, info, counts = _post(xs, ya, olat, yc, cache_mem_k.reshape(depth * dbsz, N_MEM, X_H, X_HD),
                                     cache_mem_v.reshape(depth * dbsz, N_MEM, X_H, X_HD), dbsz, w, kv_seq0=l * dbsz)
        xs = _moe(x2, h3, info, counts, w, gfin, final)
        outs["ckv_s"].append(ckv.reshape(dbsz, dseq, KV_LORA))
        outs["kpe_s"].append(kpe.reshape(dbsz, dseq, ROPE_DIM))
        outs["pool_s"].append(st_tm.transpose(1, 0, 2))
        outs["gla_s"].append(s_new.reshape(dbsz, GLA_H, GLA_DK, GLA_DV))

    st = lambda k: jnp.stack(outs[k])
    return (xp.reshape(bsz, seq, D_MODEL), xs.reshape(dbsz, dseq, D_MODEL),
            st("ckv_p"), st("kpe_p"), st("pool_p"), st("gla_p"), st("mk_p"), st("mv_p"),
            st("ckv_s"), st("kpe_s"), st("pool_s"), st("gla_s"))
```

```python
import functools

import numpy as np
import jax
import jax.numpy as jnp
from jax import lax
from jax.experimental import pallas as pl
from jax.experimental.pallas import tpu as pltpu

F32 = jnp.float32
BF16 = jnp.bfloat16
I32 = jnp.int32

EPS = 1e-6
D_MODEL = 1024
POOL_GROUPS, POOL_GC = 4, 64
POOL_W = POOL_GROUPS * POOL_GC
POOL_WINDOWS = (2, 4, 8, 16)
POOL_BUF = 15
MLA_H, Q_LORA, KV_LORA, NOPE, ROPE_DIM, V_DIM = 8, 256, 128, 64, 32, 64
ROPE_BASE = 10000.0
MLA_SCALE = (NOPE + ROPE_DIM) ** -0.5
Q_PRESCALE = MLA_SCALE * 1.4426950408889634
GLA_H, GLA_DK, GLA_DV, GATE_RANK, GATE_TAU = 4, 32, 64, 16, 16.0
GLA_W = GLA_H * GLA_DV
N_MEM, X_H, X_HD = 256, 4, 128
X_W = X_H * X_HD
N_GROUPS, EXP_PER_GROUP, N_EXPERTS, TOP_K, D_EXPERT = 4, 8, 32, 2, 256
QK_PAD = 256
ONES_LANE = KV_LORA + ROPE_DIM
IN_PAD = 1536
EXPERT_LANE0 = 32
MOE_BLOCK = 512
MOE_CHUNK = 16
NEG = -1e30
VMEM_LIMIT = 56 * 1024 * 1024


def _pick(n, prefs):
    for p in prefs:
        if n % p == 0:
            return p
    raise ValueError(f"no tile in {prefs} divides {n}")


def _cparams(n_axes, **flags):
    return pltpu.CompilerParams(dimension_semantics=("arbitrary",) * n_axes, vmem_limit_bytes=VMEM_LIMIT,
                                flags=flags or None)


def _rms(x, g):
    ms = jnp.mean(x * x, axis=-1, keepdims=True)
    return x * lax.rsqrt(ms + EPS) * g


def _dot(a, b):
    return jnp.dot(a, b, preferred_element_type=F32)


def _dot_nt(a, b):
    return lax.dot_general(a, b, (((1,), (1,)), ((), ())), preferred_element_type=F32)


def _dot_tn(a, b):
    return lax.dot_general(a, b, (((0,), (0,)), ((), ())), preferred_element_type=F32)


def _split_bf16(x):
    hi = x.astype(BF16)
    lo = (x - hi.astype(F32)).astype(BF16)
    return hi, lo


def _swap16(x):
    w = x.shape[-1]
    lane = lax.broadcasted_iota(I32, x.shape, x.ndim - 1)
    first = (lane & 31) < 16
    return jnp.where(first, pltpu.roll(x, w - 16, x.ndim - 1), pltpu.roll(x, 16, x.ndim - 1))


def _silu(x):
    return x / (1.0 + jnp.exp(-x))


def _inproj_kernel(x_ref, gmix_ref, win_ref, cos_ref, sin_ref, gqn_ref, wuq_ref, wuk_ref, gkvn_ref, wgk_ref, bgk_ref,
                   u_ref, q_ref, kvb_ref, ckv_ref, kpe_ref, gq_ref, gk_ref, gv_ref, go_ref, glog_ref):
    h = _rms(x_ref[...], gmix_ref[...]).astype(BF16)
    y = _dot(h, win_ref[...])
    u_ref[...] = y[:, 0:256]
    gq_ref[...] = y[:, 768:896]
    gk_ref[...] = y[:, 896:1024]
    gv_ref[...] = y[:, 1024:1280]
    go_ref[...] = y[:, 1280:1536]
    cos = cos_ref[...]
    sin = sin_ref[...]
    ckv = _rms(y[:, 512:640], gkvn_ref[...])
    grp = y[:, 640:768]
    grp_r = grp * cos[:, 0:128] + _swap16(grp) * sin[:, 0:128]
    ckv_ref[...] = ckv
    kpe_ref[...] = grp_r[:, 0:ROPE_DIM]
    lane = lax.broadcasted_iota(I32, grp_r.shape, 1)
    kvb_ref[:, 0:128] = ckv.astype(BF16)
    kvb_ref[:, 128:256] = jnp.where(lane < ROPE_DIM, grp_r, jnp.where(lane == ROPE_DIM, 1.0, 0.0)).astype(BF16)
    gl = _dot(grp.astype(BF16), wgk_ref[...]) + bgk_ref[...]
    glog_ref[...] = (jnp.minimum(gl, 0.0) - jnp.log(1.0 + jnp.exp(-jnp.abs(gl)))) * (1.0 / GATE_TAU)
    cqn = _rms(y[:, 256:512], gqn_ref[...]).astype(BF16)
    q = _dot(cqn, wuq_ref[...])
    qlat = _dot(q[:, 0:512].astype(BF16), wuk_ref[...]) * Q_PRESCALE
    qr = q[:, 512:768]
    qr = (qr * cos + _swap16(qr) * sin) * Q_PRESCALE
    for hh in range(MLA_H):
        q_ref[hh, :, 0:128] = qlat[:, 128 * hh:128 * hh + 128].astype(BF16)
        col = qr[:, 128 * (hh // 4):128 * (hh // 4) + 128]
        sh = (128 - 32 * (hh % 4)) % 128
        if sh:
            col = pltpu.roll(col, sh, 1)
        q_ref[hh, :, 128:256] = jnp.where(lane < ROPE_DIM, col, 0.0).astype(BF16)


def _inproj(x, cos_t, sin_t, w):
    n = x.shape[0]
    tm = _pick(n, (512, 256, 128, 64, 32, 16))
    tab_tiles = cos_t.shape[0] // tm
    assert cos_t.shape[0] % tm == 0
    row = lambda i: (i, 0)
    tab = lambda i: (i % tab_tiles, 0)
    full = lambda i: (0, 0)
    wspec = lambda a: pl.BlockSpec(a.shape, full)
    outs = [
        jax.ShapeDtypeStruct((n, 256), F32),
        jax.ShapeDtypeStruct((MLA_H, n, QK_PAD), BF16),
        jax.ShapeDtypeStruct((n, QK_PAD), BF16),
        jax.ShapeDtypeStruct((n, KV_LORA), F32),
        jax.ShapeDtypeStruct((n, ROPE_DIM), F32),
        jax.ShapeDtypeStruct((n, 128), F32),
        jax.ShapeDtypeStruct((n, 128), F32),
        jax.ShapeDtypeStruct((n, 256), F32),
        jax.ShapeDtypeStruct((n, 256), F32),
        jax.ShapeDtypeStruct((n, 128), F32),
    ]
    out_specs = [
        pl.BlockSpec((tm, 256), row),
        pl.BlockSpec((MLA_H, tm, QK_PAD), lambda i: (0, i, 0)),
        pl.BlockSpec((tm, QK_PAD), row),
        pl.BlockSpec((tm, KV_LORA), row),
        pl.BlockSpec((tm, ROPE_DIM), row),
        pl.BlockSpec((tm, 128), row),
        pl.BlockSpec((tm, 128), row),
        pl.BlockSpec((tm, 256), row),
        pl.BlockSpec((tm, 256), row),
        pl.BlockSpec((tm, 128), row),
    ]
    ins = [x, w["g_mix"], w["w_in"], cos_t, sin_t, w["g_qn"], w["w_uq"], w["w_ukbd"], w["g_kvn"], w["w_gk2"], w["b_gk"]]
    in_specs = [pl.BlockSpec((tm, D_MODEL), row), wspec(w["g_mix"]), wspec(w["w_in"]),
                pl.BlockSpec((tm, 256), tab), pl.BlockSpec((tm, 256), tab),
                wspec(w["g_qn"]), wspec(w["w_uq"]), wspec(w["w_ukbd"]), wspec(w["g_kvn"]), wspec(w["w_gk2"]), wspec(w["b_gk"])]
    return pl.pallas_call(_inproj_kernel, out_shape=outs, grid=(n // tm,), in_specs=in_specs, out_specs=out_specs,
                          compiler_params=_cparams(1), name="inproj")(*ins)


def _pool_mix(sums, u, cnts, wbd, scale):
    cols = []
    for c in range(2):
        wa, wb = POOL_WINDOWS[2 * c], POOL_WINDOWS[2 * c + 1]
        sa, sb = sums[(c, wa)], sums[(c, wb)]
        lane = lax.broadcasted_iota(I32, sa.shape, 1)
        pooled = jnp.where(lane < POOL_GC, sa / cnts[wa], sb / cnts[wb]) - u[:, 128 * c:128 * c + 128]
        cols.append(pooled)
    pooled = jnp.concatenate(cols, axis=1).astype(BF16)
    return (_dot(pooled, wbd) * scale).astype(BF16)


def _pool_kernel(u_ref, wbd_ref, scale_ref, ya_ref, st_ref, ue_ref, *, tp, nt):
    j = pl.program_id(1)

    @pl.when(j == 0)
    def _():
        ue_ref[0:16, :] = jnp.zeros((16, POOL_W), F32)

    @pl.when(j > 0)
    def _():
        ue_ref[0:16, :] = ue_ref[tp:tp + 16, :]

    u = u_ref[...]
    ue_ref[16:16 + tp, :] = u
    t = j * tp + lax.broadcasted_iota(I32, (tp, 1), 0)
    cnts = {w: jnp.minimum(t + 1, w).astype(F32) for w in POOL_WINDOWS}
    sums = {}
    for c in range(2):
        wa, wb = POOL_WINDOWS[2 * c], POOL_WINDOWS[2 * c + 1]
        acc = None
        for k in range(wb):
            sl = ue_ref[16 - k:16 - k + tp, 128 * c:128 * c + 128]
            acc = sl if acc is None else acc + sl
            if k + 1 == wa:
                sums[(c, wa)] = acc
        sums[(c, wb)] = acc
    ya_ref[...] = _pool_mix(sums, u, cnts, wbd_ref[...], scale_ref[...])

    @pl.when(j == nt - 1)
    def _():
        st_ref[0] = ue_ref[tp + 1:tp + 16, :]


def _pool_prompt(u, bsz, w):
    n = u.shape[0]
    t = n // bsz
    tp = _pick(t, (512, 256, 128, 64, 32, 16))
    nt = t // tp
    kern = functools.partial(_pool_kernel, tp=tp, nt=nt)
    return pl.pallas_call(
        kern,
        out_shape=[jax.ShapeDtypeStruct((n, POOL_W), BF16), jax.ShapeDtypeStruct((bsz, POOL_BUF, POOL_W), F32)],
        grid=(bsz, nt),
        in_specs=[pl.BlockSpec((tp, POOL_W), lambda b, j: (b * nt + j, 0)),
                  pl.BlockSpec((POOL_W, POOL_W), lambda b, j: (0, 0)),
                  pl.BlockSpec((1, POOL_W), lambda b, j: (0, 0))],
        out_specs=[pl.BlockSpec((tp, POOL_W), lambda b, j: (b * nt + j, 0)),
                   pl.BlockSpec((1, POOL_BUF, POOL_W), lambda b, j: (b, 0, 0))],
        scratch_shapes=[pltpu.VMEM((tp + 16, POOL_W), F32)],
        compiler_params=_cparams(2), name="pool_prompt")(u, w["w_poolbd"], w["pool_scale"])


def _pool_step_kernel(ue_ref, wbd_ref, scale_ref, ya_ref, st_ref, *, tt):
    cnts = {w: jnp.float32(w) for w in POOL_WINDOWS}
    for t in range(tt):
        sums = {}
        for c in range(2):
            wa, wb = POOL_WINDOWS[2 * c], POOL_WINDOWS[2 * c + 1]
            acc = None
            for k in range(wb):
                sl = ue_ref[POOL_BUF + t - k, :, 128 * c:128 * c + 128]
                acc = sl if acc is None else acc + sl
                if k + 1 == wa:
                    sums[(c, wa)] = acc
            sums[(c, wb)] = acc
        ya_ref[t] = _pool_mix(sums, ue_ref[POOL_BUF + t], cnts, wbd_ref[...], scale_ref[...])
    for r in range(POOL_BUF):
        st_ref[r] = ue_ref[tt + r]


def _pool_sample(ue_tm, w):
    rows, bsz, _ = ue_tm.shape
    tt = rows - POOL_BUF
    kern = functools.partial(_pool_step_kernel, tt=tt)
    full3 = lambda i: (0, 0, 0)
    return pl.pallas_call(
        kern,
        out_shape=[jax.ShapeDtypeStruct((tt, bsz, POOL_W), BF16), jax.ShapeDtypeStruct((POOL_BUF, bsz, POOL_W), F32)],
        grid=(1,),
        in_specs=[pl.BlockSpec(ue_tm.shape, full3), pl.BlockSpec((POOL_W, POOL_W), lambda i: (0, 0)),
                  pl.BlockSpec((1, POOL_W), lambda i: (0, 0))],
        out_specs=[pl.BlockSpec((tt, bsz, POOL_W), full3), pl.BlockSpec((POOL_BUF, bsz, POOL_W), full3)],
        compiler_params=_cparams(1), name="pool_sample")(ue_tm, w["w_poolbd"], w["pool_scale"])


def _mla_prompt_kernel(qi_ref, kj_ref, q_ref, k_ref, o_ref, m_ref, acc_ref, *, tq, tk):
    p = pl.program_id(1)
    qi = qi_ref[p]
    kj = kj_ref[p]
    rows = MLA_H * tq

    @pl.when(kj == 0)
    def _():
        m_ref[...] = jnp.full((rows, 128), NEG, F32)
        acc_ref[...] = jnp.zeros((rows, QK_PAD), F32)

    def step(masked):
        q = q_ref[...].reshape(rows, QK_PAD)
        k = k_ref[...]
        s = _dot_nt(q, k)
        if masked:
            r = lax.broadcasted_iota(I32, (rows, tk), 0)
            c = lax.broadcasted_iota(I32, (rows, tk), 1)
            qpos = qi * tq + (r & (tq - 1))
            s = jnp.where(kj * tk + c <= qpos, s, NEG)
        m_old = m_ref[...]
        m_new = jnp.maximum(m_old, jnp.max(s, axis=-1, keepdims=True))
        alpha = jnp.exp2(m_old - m_new)
        pr = jnp.exp2(s - jnp.concatenate([m_new] * (tk // 128), axis=1))
        acc_ref[...] = jnp.concatenate([alpha, alpha], axis=1) * acc_ref[...] + _dot(pr.astype(BF16), k)
        m_ref[...] = m_new

    crosses = (kj + 1) * tk - 1 > qi * tq

    @pl.when(crosses)
    def _():
        step(True)

    @pl.when(jnp.logical_not(crosses))
    def _():
        step(False)

    @pl.when(kj == ((qi + 1) * tq - 1) // tk)
    def _():
        acc = acc_ref[...]
        o = acc[:, 0:KV_LORA] / acc[:, ONES_LANE:ONES_LANE + 1]
        for h in range(MLA_H):
            o_ref[:, KV_LORA * h:KV_LORA * (h + 1)] = o[h * tq:(h + 1) * tq].astype(BF16)


def _mla_prompt(q_hm, kvb, bsz):
    n = kvb.shape[0]
    t = n // bsz
    tq = _pick(t, (256, 128, 64, 32, 16))
    tk = _pick(t, (512, 256, 128, 64, 32, 16))
    assert tq & (tq - 1) == 0
    nq, nk = t // tq, t // tk
    pairs = [(i, j) for i in range(nq) for j in range(((i + 1) * tq - 1) // tk + 1)]
    qi = jnp.asarray(np.array([p[0] for p in pairs], np.int32))
    kj = jnp.asarray(np.array([p[1] for p in pairs], np.int32))
    rows = MLA_H * tq
    kern = functools.partial(_mla_prompt_kernel, tq=tq, tk=tk)
    grid_spec = pltpu.PrefetchScalarGridSpec(
        num_scalar_prefetch=2, grid=(bsz, len(pairs)),
        in_specs=[pl.BlockSpec((MLA_H, tq, QK_PAD), lambda b, p, qi, kj: (0, b * nq + qi[p], 0)),
                  pl.BlockSpec((tk, QK_PAD), lambda b, p, qi, kj: (b * nk + kj[p], 0))],
        out_specs=pl.BlockSpec((tq, MLA_H * KV_LORA), lambda b, p, qi, kj: (b * nq + qi[p], 0)),
        scratch_shapes=[pltpu.VMEM((rows, 128), F32), pltpu.VMEM((rows, QK_PAD), F32)])
    return pl.pallas_call(kern, out_shape=jax.ShapeDtypeStruct((n, MLA_H * KV_LORA), BF16), grid_spec=grid_spec,
                          compiler_params=_cparams(2), name="mla_prompt")(qi, kj, q_hm, kvb)


def _mla_decode_kernel(pt_ref, q_ref, knew_ref, cckv_ref, ckpe_ref, o_ref,
                       cbuf, pbuf, csem, psem, m_ref, l_ref, acc_ref, *, layer, ng, gp, page, tt, total):
    s_idx = pl.program_id(0)
    grp = s_idx % ng
    slot = s_idx % 2

    def copies(step, slot_):
        out = []
        for g in range(gp):
            pg = pt_ref[step * gp + g]
            out.append(pltpu.make_async_copy(cckv_ref.at[layer, pg], cbuf.at[slot_, pl.ds(g * page, page)], csem.at[slot_]))
            out.append(pltpu.make_async_copy(ckpe_ref.at[layer, pg], pbuf.at[slot_, :, pl.ds(g * page, page)], psem.at[slot_]))
        return out

    @pl.when(s_idx == 0)
    def _():
        for c in copies(0, 0):
            c.start()

    @pl.when(s_idx + 1 < total)
    def _():
        for c in copies(s_idx + 1, 1 - slot):
            c.start()

    @pl.when(grp == 0)
    def _():
        m_ref[...] = jnp.full(m_ref.shape, NEG, F32)
        l_ref[...] = jnp.zeros(l_ref.shape, F32)
        acc_ref[...] = jnp.zeros(acc_ref.shape, F32)

    for c in copies(s_idx, slot):
        c.wait()

    q = q_ref[0]

    def update(s, v):
        m_old = m_ref[...]
        m_new = jnp.maximum(m_old, jnp.max(s, axis=-1, keepdims=True))
        alpha = jnp.exp2(m_old - m_new)
        pr = jnp.exp2(s - m_new[:, 0:1])
        l_ref[...] = alpha * l_ref[...] + jnp.sum(pr, axis=-1, keepdims=True)
        acc_ref[...] = alpha * acc_ref[...] + _dot(pr.astype(BF16), v)
        m_ref[...] = m_new

    kc = cbuf[slot].astype(BF16)
    kr_t = pbuf[slot].astype(BF16)
    s = _dot_nt(q[:, 0:KV_LORA], kc) + _dot(q[:, KV_LORA:KV_LORA + ROPE_DIM], kr_t)
    update(s, kc)

    @pl.when(grp == ng - 1)
    def _():
        kn = knew_ref[0]
        sn = _dot_nt(q, kn)
        r = lax.broadcasted_iota(I32, sn.shape, 0)
        c = lax.broadcasted_iota(I32, sn.shape, 1)
        sn = jnp.where(c <= (r % tt), sn, NEG)
        update(sn, kn[:, 0:KV_LORA])
        o_ref[0] = acc_ref[...] / l_ref[...]


def _mla_decode(q_s, knew, cache_ckv, cache_kpe_t, page_table, layer):
    dbsz, rows, _ = q_s.shape
    tt = rows // MLA_H
    n_pages = page_table.shape[1]
    page = cache_ckv.shape[2]
    gp = _pick(n_pages, (64, 32, 16, 8, 4, 2, 1))
    ng = n_pages // gp
    total = dbsz * ng
    kern = functools.partial(_mla_decode_kernel, layer=layer, ng=ng, gp=gp, page=page, tt=tt, total=total)
    grid_spec = pltpu.PrefetchScalarGridSpec(
        num_scalar_prefetch=1, grid=(total,),
        in_specs=[pl.BlockSpec((1, rows, QK_PAD), lambda s, pt: (s // ng, 0, 0)),
                  pl.BlockSpec((1, 16, QK_PAD), lambda s, pt: (s // ng, 0, 0)),
                  pl.BlockSpec(memory_space=pl.ANY), pl.BlockSpec(memory_space=pl.ANY)],
        out_specs=pl.BlockSpec((1, rows, KV_LORA), lambda s, pt: (s // ng, 0, 0)),
        scratch_shapes=[pltpu.VMEM((2, gp * page, KV_LORA), F32), pltpu.VMEM((2, ROPE_DIM, gp * page), F32),
                        pltpu.SemaphoreType.DMA((2,)), pltpu.SemaphoreType.DMA((2,)),
                        pltpu.VMEM((rows, 128), F32), pltpu.VMEM((rows, 128), F32), pltpu.VMEM((rows, KV_LORA), F32)])
    return pl.pallas_call(kern, out_shape=jax.ShapeDtypeStruct((dbsz, rows, KV_LORA), F32), grid_spec=grid_spec,
                          compiler_params=_cparams(1), name="mla_decode")(
        page_table.reshape(-1), q_s, knew, cache_ckv, cache_kpe_t)


def _gla_constants(c):
    t = np.arange(c)
    blocks = [(t[None, :] <= t[:, None]), (t[None, :] > t[:, None])]
    masks = []
    m = c // 2
    while m >= 1:
        bd = (t // (2 * m)) * 2 * m + m
        upper = t >= bd
        a = upper[:, None] & (t[None, :] >= bd[:, None]) & (t[None, :] <= t[:, None])
        b = (~upper)[:, None] & (t[None, :] > t[:, None]) & (t[None, :] <= bd[:, None] - 1)
        blocks += [a, b]
        same = (t[:, None] // (2 * m)) == (t[None, :] // (2 * m))
        masks.append(same & upper[:, None] & (~upper)[None, :])
        m //= 2
    masks.append(t[:, None] == t[None, :])
    sel = np.concatenate(blocks, axis=0).astype(np.float32)
    masks = np.concatenate([np.tile(mm, (1, GLA_H)) for mm in masks], axis=0).astype(np.float32)
    r = np.arange(GLA_H * c)
    kmask = ((r[:, None] // c) == (np.arange(128)[None, :] // GLA_DK)).astype(np.float32)
    vmask = ((r[:, None] // c) == (np.arange(GLA_W)[None, :] // GLA_DV)).astype(np.float32)
    smask = ((np.arange(128)[:, None] // GLA_DK) == (np.arange(GLA_W)[None, :] // GLA_DV)).astype(np.float32)
    last = np.zeros((c, GLA_W), np.float32)
    last[c - 1, :] = 1.0
    gsum = ((np.arange(GLA_W)[:, None] // GLA_DV) == (np.arange(GLA_W)[None, :] // GLA_DV)).astype(np.float32)
    return sel, masks, kmask, vmask, smask, last, gsum


def _gla_kernel(*refs, c, nsub, nsteps, has_init):
    if has_init:
        (q_ref, k_ref, v_ref, go_ref, g_ref, s0_ref, sel_ref, masks_ref, kmask_ref, vmask_ref, smask_ref, last_ref,
         gsum_ref, ggla_ref, y_ref, sout_ref, s_ref) = refs
    else:
        (q_ref, k_ref, v_ref, go_ref, g_ref, sel_ref, masks_ref, kmask_ref, vmask_ref, smask_ref, last_ref,
         gsum_ref, ggla_ref, y_ref, sout_ref, s_ref) = refs
        s0_ref = None
    j = pl.program_id(1)
    nlev = int(np.log2(c))
    smask = smask_ref[...]

    @pl.when(j == 0)
    def _():
        if has_init:
            s0 = s0_ref[0]
            s_ref[...] = jnp.concatenate([s0] * GLA_H, axis=1) * smask
        else:
            s_ref[...] = jnp.zeros(s_ref.shape, F32)

    def chunk(i, carry):
        sl = pl.ds(pl.multiple_of(i * c, c), c)
        q = q_ref[sl, :] * (GLA_DK ** -0.5)
        k = k_ref[sl, :]
        v = v_ref[sl, :].astype(BF16)
        g_hi, g_lo = _split_bf16(g_ref[sl, :])
        sel = sel_ref[...]
        e_all = jnp.exp(_dot(sel, g_hi) + _dot(sel, g_lo))
        eb = e_all[0:c]
        s_old = s_ref[...]
        o = _dot((q * eb).astype(BF16), s_old.astype(BF16))
        kmask = kmask_ref[...]
        att = None
        for lev in range(nlev + 1):
            if lev < nlev:
                ql = (q * e_all[(2 + 2 * lev) * c:(3 + 2 * lev) * c]).astype(BF16)
                kl = k * e_all[(3 + 2 * lev) * c:(4 + 2 * lev) * c]
            else:
                ql, kl = q.astype(BF16), k
            kb = (jnp.concatenate([kl] * GLA_H, axis=0) * kmask).astype(BF16)
            part = _dot_nt(ql, kb) * masks_ref[lev * c:(lev + 1) * c, :]
            att = part if att is None else att + part
        vb = (jnp.concatenate([v] * GLA_H, axis=0) * vmask_ref[...].astype(BF16))
        o = o + _dot(att.astype(BF16), vb)
        kk = (k * e_all[c:2 * c]).astype(BF16)
        eb_hi, eb_lo = _split_bf16(eb)
        last = last_ref[...].astype(BF16)
        decay = _dot_tn(eb_hi, last) + _dot_tn(eb_lo, last)
        s_ref[...] = (decay * s_old + _dot_tn(kk, v)) * smask
        o2_hi, o2_lo = _split_bf16(o * o)
        gsum = gsum_ref[...]
        ms = (_dot(o2_hi, gsum) + _dot(o2_lo, gsum)) * (1.0 / GLA_DV)
        yc = o * lax.rsqrt(ms + EPS) * ggla_ref[...] * _silu(go_ref[sl, :])
        y_ref[sl, :] = yc.astype(BF16)
        return carry

    lax.fori_loop(0, nsub, chunk, 0, unroll=True)

    @pl.when(j == nsteps - 1)
    def _():
        s = s_ref[...]
        acc = s[:, 0:GLA_DV]
        for h in range(1, GLA_H):
            acc = acc + s[:, GLA_DV * h:GLA_DV * (h + 1)]
        sout_ref[0] = acc


def _gla(gq, gk, gv, go, glog, s0, bsz, w, chunk):
    n = gq.shape[0]
    t = n // bsz
    c = min(chunk, t)
    assert t % c == 0 and c & (c - 1) == 0 and c >= 8
    nsub = _pick(t // c, (8, 4, 2, 1))
    tc = c * nsub
    nsteps = t // tc
    consts = [jnp.asarray(a) for a in _gla_constants(c)]
    consts[0] = consts[0].astype(BF16)
    consts[6] = consts[6].astype(BF16)
    has_init = s0 is not None
    kern = functools.partial(_gla_kernel, c=c, nsub=nsub, nsteps=nsteps, has_init=has_init)
    row = lambda b, j: (b * nsteps + j, 0)
    full = lambda b, j: (0, 0)
    ins = [gq, gk, gv, go, glog]
    in_specs = [pl.BlockSpec((tc, 128), row), pl.BlockSpec((tc, 128), row), pl.BlockSpec((tc, 256), row),
                pl.BlockSpec((tc, 256), row), pl.BlockSpec((tc, 128), row)]
    if has_init:
        ins.append(s0)
        in_specs.append(pl.BlockSpec((1, 128, GLA_DV), lambda b, j: (b, 0, 0)))
    ins += consts + [w["g_gla"]]
    in_specs += [pl.BlockSpec(a.shape, full) for a in consts] + [pl.BlockSpec(w["g_gla"].shape, full)]
    return pl.pallas_call(
        kern,
        out_shape=[jax.ShapeDtypeStruct((n, GLA_W), BF16), jax.ShapeDtypeStruct((bsz, 128, GLA_DV), F32)],
        grid=(bsz, nsteps), in_specs=in_specs,
        out_specs=[pl.BlockSpec((tc, GLA_W), row), pl.BlockSpec((1, 128, GLA_DV), lambda b, j: (b, 0, 0))],
        scratch_shapes=[pltpu.VMEM((128, GLA_W), F32)],
        compiler_params=_cparams(2), name="gla")(*ins)


def _memkv_kernel(m_ref, g_ref, wk_ref, wv_ref, mk_ref, mv_ref):
    h = _rms(m_ref[...], g_ref[...]).astype(BF16)
    mk_ref[...] = _dot(h, wk_ref[...])
    mv_ref[...] = _dot(h, wv_ref[...])


def _memkv(mem, w):
    n = mem.shape[0]
    tm = _pick(n, (256, 128, 64, 32, 16, 8))
    row = lambda i: (i, 0)
    full = lambda i: (0, 0)
    return pl.pallas_call(
        _memkv_kernel,
        out_shape=[jax.ShapeDtypeStruct((n, X_W), F32)] * 2, grid=(n // tm,),
        in_specs=[pl.BlockSpec((tm, D_MODEL), row), pl.BlockSpec((1, D_MODEL), full),
                  pl.BlockSpec((D_MODEL, X_W), full), pl.BlockSpec((D_MODEL, X_W), full)],
        out_specs=[pl.BlockSpec((tm, X_W), row)] * 2,
        compiler_params=_cparams(1), name="memkv")(mem, w["g_mem"], w["w_xk"], w["w_xv"])


def _post_kernel(x_ref, ya_ref, ol_ref, yc_ref, mk_ref, mv_ref, wuv_ref, woa_ref, wob_ref, woc_ref, gx_ref, wxq_ref, wxo_ref,
                 gffn_ref, wrh_ref, wrl_ref, br_ref, ltri_ref, ustr_ref,
                 x2_ref, rows_ref, info_ref, nch_ref, *, nb, rpb):
    tm = x_ref.shape[0]
    lr = rows_ref.shape[0]
    yb = _dot(ol_ref[...], wuv_ref[...]).astype(BF16)
    x1 = x_ref[...] + _dot(ya_ref[...], woa_ref[...]) + _dot(yb, wob_ref[...]) + _dot(yc_ref[...], woc_ref[...])
    q = _dot(_rms(x1, gx_ref[...]).astype(BF16), wxq_ref[...])
    def head_kv(ref, h):
        if len(ref.shape) == 4:
            a = ref[:, :, h, :]
        else:
            a = ref[:, :, X_HD * h:X_HD * (h + 1)]
        return a.reshape(nb * N_MEM, X_HD).astype(BF16)

    outs = []
    for h in range(X_H):
        sl = slice(X_HD * h, X_HD * (h + 1))
        s = _dot_nt(q[:, sl].astype(BF16), head_kv(mk_ref, h)) * (X_HD ** -0.5)
        if nb > 1:
            r = lax.broadcasted_iota(I32, s.shape, 0)
            c = lax.broadcasted_iota(I32, s.shape, 1)
            s = jnp.where(r // rpb == c // N_MEM, s, NEG)
        e = jnp.exp(s - jnp.max(s, axis=-1, keepdims=True))
        p = e / jnp.sum(e, axis=-1, keepdims=True)
        outs.append(_dot(p.astype(BF16), head_kv(mv_ref, h)))
    o = jnp.concatenate(outs, axis=1).astype(BF16)
    x2 = x1 + _dot(o, wxo_ref[...])
    x2_ref[...] = x2
    h3 = _rms(x2, gffn_ref[...])
    h_hi, h_lo = _split_bf16(h3)
    logit = _dot(h_hi, wrh_ref[...]) + _dot(h_lo, wrh_ref[...]) + _dot(h_hi, wrl_ref[...]) + br_ref[...]
    lane = lax.broadcasted_iota(I32, logit.shape, 1)
    gl = jnp.where(lane < N_GROUPS, logit, NEG)
    gmax = jnp.max(gl, axis=-1, keepdims=True)
    gsel = jnp.min(jnp.where(gl == gmax, lane, 1 << 20), axis=-1, keepdims=True)
    gw = 1.0 / jnp.sum(jnp.exp(gl - gmax), axis=-1, keepdims=True)
    emask = jnp.logical_and(lane >= EXPERT_LANE0, (lane - EXPERT_LANE0) // EXP_PER_GROUP == gsel)
    el = jnp.where(emask, logit, NEG)
    pe = jnp.where(emask, jnp.exp(el - jnp.max(el, axis=-1, keepdims=True)), 0.0)
    prob = pe / jnp.sum(pe, axis=-1, keepdims=True)
    prob = jnp.where(emask, prob, -1.0)
    p1 = jnp.max(prob, axis=-1, keepdims=True)
    i1 = jnp.min(jnp.where(prob == p1, lane, 1 << 20), axis=-1, keepdims=True)
    prob2 = jnp.where(lane == i1, -1.0, prob)
    p2 = jnp.max(prob2, axis=-1, keepdims=True)
    i2 = jnp.min(jnp.where(prob2 == p2, lane, 1 << 20), axis=-1, keepdims=True)
    gate1 = gw * p1 / (p1 + p2)
    gate2 = gw * p2 / (p1 + p2)
    oh1 = (lane == i1).astype(F32)
    oh2 = (lane == i2).astype(F32)
    both = oh1 + oh2
    before = _dot(ltri_ref[...], both.astype(BF16))
    cnt = jnp.sum(both, axis=0, keepdims=True)
    nch = jnp.floor((cnt + (MOE_CHUNK - 1.0)) * (1.0 / MOE_CHUNK))
    nch8 = jnp.broadcast_to(nch, (8, 128))
    off = _dot(nch8.astype(BF16), ustr_ref[...])[0:1] * float(MOE_CHUNK)
    pos = off + before
    v1 = oh1 * pos
    v2 = oh2 * pos
    loc1 = jnp.sum(v1, axis=-1, keepdims=True)
    loc2 = jnp.sum(v2, axis=-1, keepdims=True)
    ones8 = jnp.ones((8, 128), BF16)

    def as_row(v):
        hi = jnp.floor(v * (1.0 / 32.0))
        lo = v - 32.0 * hi
        return (32.0 * _dot_nt(ones8, hi.astype(BF16)) + _dot_nt(ones8, lo.astype(BF16)))[0:1]

    slot_row = lax.broadcasted_iota(I32, (lr, tm), 0).astype(F32)
    perm = jnp.logical_or(slot_row == as_row(v1), slot_row == as_row(v2))
    perm = jnp.where(perm, 1.0, 0.0).astype(BF16)
    rows_ref[...] = _dot(perm, h3.astype(BF16)).astype(BF16)
    nch_ref[...] = nch8
    info = jnp.where(lane == 2, gate1, 0.0)
    info = jnp.where(lane == 3, gate2, info)
    info = jnp.where(lane == 4, loc1, info)
    info = jnp.where(lane == 5, loc2, info)
    info_ref[...] = info


def _local_rows(tm):
    need = TOP_K * tm + N_EXPERTS * (MOE_CHUNK - 1) + MOE_CHUNK
    return -(-need // 128) * 128


def _post(x, ya, olat, yc, mk, mv, bsz, w, kv_seq0=0):
    n = x.shape[0]
    t = n // bsz
    if t >= 16:
        tm = _pick(t, (512, 256, 128, 64, 32, 16))
        nb, rpb = 1, tm
        grid = (bsz, t // tm)
        row = lambda b, j: (b * (t // tm) + j, 0)
        kvm = lambda b, j: (kv_seq0 + b, 0, 0)
    else:
        nb = _pick(bsz, (8, 4, 2, 1))
        tm, rpb = nb * t, t
        assert tm % 16 == 0 and kv_seq0 % nb == 0
        grid = (bsz // nb, 1)
        row = lambda b, j: (b, 0)
        kvm = lambda b, j: (kv_seq0 // nb + b, 0, 0)
    ltri = jnp.asarray(np.tril(np.ones((tm, tm), np.float32), -1)).astype(BF16)
    ustr = jnp.asarray(np.triu(np.ones((128, 128), np.float32), 1)).astype(BF16)
    lr = _local_rows(tm)
    ntile = n // tm
    tile = lambda b, j: (row(b, j)[0], 0)
    full = lambda b, j: (0, 0)
    wnames = ["w_uvbd", "wo_a", "wo_b", "wo_c", "g_x", "w_xq", "w_xo", "g_ffn", "w_r_hi", "w_r_lo", "b_r"]
    ins = [x, ya, olat, yc, mk, mv] + [w[k] for k in wnames] + [ltri, ustr]
    in_specs = [pl.BlockSpec((tm, D_MODEL), row), pl.BlockSpec((tm, POOL_W), row), pl.BlockSpec((tm, MLA_H * KV_LORA), row),
                pl.BlockSpec((tm, GLA_W), row)]
    if mk.ndim == 4:
        kvm4 = lambda b, j: kvm(b, j) + (0,)
        in_specs += [pl.BlockSpec((nb, N_MEM, X_H, X_HD), kvm4)] * 2
    else:
        in_specs += [pl.BlockSpec((nb, N_MEM, X_W), kvm)] * 2
    in_specs += [pl.BlockSpec(w[k].shape, full) for k in wnames] + [pl.BlockSpec((tm, tm), full), pl.BlockSpec((128, 128), full)]
    kern = functools.partial(_post_kernel, nb=nb, rpb=rpb)
    x2, rows, info, nch = pl.pallas_call(
        kern,
        out_shape=[jax.ShapeDtypeStruct((n, D_MODEL), F32), jax.ShapeDtypeStruct((ntile * lr, D_MODEL), BF16),
                   jax.ShapeDtypeStruct((n, 128), F32), jax.ShapeDtypeStruct((ntile * 8, 128), F32)],
        grid=grid, in_specs=in_specs,
        out_specs=[pl.BlockSpec((tm, D_MODEL), row), pl.BlockSpec((lr, D_MODEL), tile), pl.BlockSpec((tm, 128), row),
                   pl.BlockSpec((8, 128), tile)],
        compiler_params=_cparams(2), name="post")(*ins)
    return x2, rows, info, nch, tm


def _expert_kernel(sc_ref, be_ref, nu_ref, rows_hbm, wg_ref, wu_ref, wd_ref, yinit_hbm, y_hbm, xbuf, ybuf, gsem, ssem, *, dummy):
    del yinit_hbm
    j = pl.program_id(0)
    nu = nu_ref[0]
    slot = j % 2
    cpb = MOE_BLOCK // MOE_CHUNK

    def gather(blk, s):
        out = []
        for c in range(cpb):
            src = pl.multiple_of(sc_ref[blk * cpb + c] * MOE_CHUNK, MOE_CHUNK)
            out.append(pltpu.make_async_copy(rows_hbm.at[pl.ds(src, MOE_CHUNK)],
                                             xbuf.at[s, pl.ds(c * MOE_CHUNK, MOE_CHUNK)], gsem.at[s]))
        return out

    def writeback(blk, s, act):
        for c in range(cpb):
            chunk = sc_ref[blk * cpb + c]

            @pl.when(chunk != dummy)
            def _():
                dst = pl.multiple_of(chunk * MOE_CHUNK, MOE_CHUNK)
                act(pltpu.make_async_copy(ybuf.at[s, pl.ds(c * MOE_CHUNK, MOE_CHUNK)],
                                          y_hbm.at[pl.ds(dst, MOE_CHUNK)], ssem.at[s]))

    @pl.when(j < nu)
    def _():
        @pl.when(j == 0)
        def _():
            for cp in gather(0, 0):
                cp.start()

        @pl.when(j + 1 < nu)
        def _():
            for cp in gather(j + 1, 1 - slot):
                cp.start()

        for cp in gather(j, slot):
            cp.wait()

        @pl.when(j >= 2)
        def _():
            writeback(j - 2, slot, lambda cp: cp.wait())

        xb = xbuf[slot]
        hid = _silu(_dot(xb, wg_ref[...])) * _dot(xb, wu_ref[...])
        ybuf[slot] = _dot(hid.astype(BF16), wd_ref[...]).astype(BF16)
        writeback(j, slot, lambda cp: cp.start())

        @pl.when(j == nu - 1)
        def _():
            writeback(j, slot, lambda cp: cp.wait())

            @pl.when(j >= 1)
            def _():
                writeback(j - 1, 1 - slot, lambda cp: cp.wait())


def _experts(rows, src_chunk, block_e, n_used, dummy, w):
    n_rows = rows.shape[0]
    nblk = block_e.shape[0]
    y_init = jnp.zeros((n_rows, D_MODEL), BF16)
    wmap = lambda j, sc, be, nu: (be[j], 0, 0)
    grid_spec = pltpu.PrefetchScalarGridSpec(
        num_scalar_prefetch=3, grid=(nblk,),
        in_specs=[pl.BlockSpec(memory_space=pl.ANY),
                  pl.BlockSpec((None, D_MODEL, D_EXPERT), wmap), pl.BlockSpec((None, D_MODEL, D_EXPERT), wmap),
                  pl.BlockSpec((None, D_EXPERT, D_MODEL), wmap),
                  pl.BlockSpec(memory_space=pl.ANY)],
        out_specs=pl.BlockSpec(memory_space=pl.ANY),
        scratch_shapes=[pltpu.VMEM((2, MOE_BLOCK, D_MODEL), BF16), pltpu.VMEM((2, MOE_BLOCK, D_MODEL), BF16),
                        pltpu.SemaphoreType.DMA((2,)), pltpu.SemaphoreType.DMA((2,))])
    return pl.pallas_call(functools.partial(_expert_kernel, dummy=dummy),
                          out_shape=jax.ShapeDtypeStruct((n_rows, D_MODEL), BF16), grid_spec=grid_spec,
                          input_output_aliases={7: 0}, compiler_params=_cparams(1), name="moe_experts")(
        src_chunk, block_e, n_used, rows, w["w_eg"], w["w_eu"], w["w_ed"], y_init)


def _combine_kernel(x_ref, info_ref, y_ref, g_ref, o_ref, *, final):
    tm, lr = x_ref.shape[0], y_ref.shape[0]
    info = info_ref[...]
    col = lax.broadcasted_iota(I32, (tm, lr), 1).astype(F32)
    gmat = jnp.where(col == info[:, 4:5], info[:, 2:3], 0.0) + jnp.where(col == info[:, 5:6], info[:, 3:4], 0.0)
    y = x_ref[...] + _dot(gmat.astype(BF16), y_ref[...])
    if final:
        y = _rms(y, g_ref[...])
    o_ref[...] = y


def _combine(x2, info, y_loc, tm, g_final, final):
    n = x2.shape[0]
    lr = _local_rows(tm)
    return pl.pallas_call(
        functools.partial(_combine_kernel, final=final),
        out_shape=jax.ShapeDtypeStruct((n, D_MODEL), F32), grid=(n // tm,),
        in_specs=[pl.BlockSpec((tm, D_MODEL), lambda i: (i, 0)), pl.BlockSpec((tm, 128), lambda i: (i, 0)),
                  pl.BlockSpec((lr, D_MODEL), lambda i: (i, 0)), pl.BlockSpec((1, D_MODEL), lambda i: (0, 0))],
        out_specs=pl.BlockSpec((tm, D_MODEL), lambda i: (i, 0)),
        compiler_params=_cparams(1), name="moe_combine")(x2, info, y_loc, g_final)


def _moe(x2, rows, info, nch, tm, w, g_final, final):
    n = x2.shape[0]
    ntile = n // tm
    lrc = _local_rows(tm) // MOE_CHUNK
    cpb = MOE_BLOCK // MOE_CHUNK
    dummy = lrc - 1
    nc = nch.reshape(ntile, 8, 128)[:, 0, EXPERT_LANE0:EXPERT_LANE0 + N_EXPERTS].astype(I32)
    src_base = jnp.arange(ntile, dtype=I32)[:, None] * lrc + (jnp.cumsum(nc, axis=1) - nc)
    pe = (jnp.sum(nc, axis=0) + cpb - 1) // cpb * cpb
    pend = jnp.cumsum(pe)
    dst = (pend - pe)[None, :] + (jnp.cumsum(nc, axis=0) - nc)
    dst_f, nc_f, src_f = dst.T.reshape(-1), nc.T.reshape(-1), src_base.T.reshape(-1)
    max_chunks = ntile * ((TOP_K * tm + N_EXPERTS * (MOE_CHUNK - 1)) // MOE_CHUNK) + N_EXPERTS * (cpb - 1)
    nblk = -(-max_chunks // cpb)
    pos = jnp.arange(nblk * cpb, dtype=I32)
    run = jnp.sum((dst_f[None, :] <= pos[:, None]).astype(I32), axis=1) - 1
    rel = pos - dst_f[run]
    src_chunk = jnp.where(rel < nc_f[run], src_f[run] + rel, dummy).astype(I32)
    blk0 = jnp.arange(nblk, dtype=I32) * cpb
    block_e = jnp.minimum(jnp.sum((pend[None, :] <= blk0[:, None]).astype(I32), axis=1), N_EXPERTS - 1).astype(I32)
    n_used = (pend[-1:] // cpb).astype(I32)
    y_loc = _experts(rows, src_chunk, block_e, n_used, dummy, w)
    return _combine(x2, info, y_loc, tm, g_final, final)


def _rope_tables(pos):
    half = ROPE_DIM // 2
    inv = ROPE_BASE ** (-jnp.arange(half, dtype=F32) * 2.0 / ROPE_DIM)
    ang = pos.astype(F32)[:, None] * inv[None, :]
    c, s = jnp.cos(ang), jnp.sin(ang)
    cos32 = jnp.concatenate([c, c], axis=1)
    sin32 = jnp.concatenate([-s, s], axis=1)
    return jnp.tile(cos32, (1, MLA_H)), jnp.tile(sin32, (1, MLA_H))


def _prep_layer(l, p):
    w_in = p["w_in"][l]
    zeros80 = jnp.zeros((D_MODEL, 80), F32)
    w_in_p = jnp.concatenate([w_in[:, 0:640], w_in[:, 640:672], w_in[:, 1184:1200], zeros80,
                              w_in[:, 672:1184], w_in[:, 1200:1456]], axis=1).astype(BF16)
    assert w_in_p.shape[1] == IN_PAD
    w_uq = p["w_uq"][l].reshape(Q_LORA, MLA_H, NOPE + ROPE_DIM)
    w_uq_p = jnp.concatenate([w_uq[:, :, :NOPE].reshape(Q_LORA, MLA_H * NOPE),
                              w_uq[:, :, NOPE:].reshape(Q_LORA, MLA_H * ROPE_DIM)], axis=1).astype(BF16)
    eye_h = jnp.eye(MLA_H, dtype=F32)
    w_ukbd = jnp.einsum("hnc,hg->hngc", p["w_uk"][l].transpose(1, 2, 0), eye_h).reshape(MLA_H * NOPE, MLA_H * KV_LORA).astype(BF16)
    w_uvbd = jnp.einsum("hcv,hg->hcgv", p["w_uv"][l].transpose(1, 0, 2), eye_h).reshape(MLA_H * KV_LORA, MLA_H * V_DIM).astype(BF16)
    w_gk2 = jnp.zeros((128, 128), F32).at[ROPE_DIM:ROPE_DIM + GATE_RANK, :].set(p["w_gk2"][l]).astype(BF16)
    w_poolbd = jnp.einsum("gcd,gk->gckd", p["w_pool"][l], jnp.eye(POOL_GROUPS, dtype=F32)).reshape(POOL_W, POOL_W).astype(BF16)
    w_out = p["w_out"][l].astype(BF16)
    w_r = jnp.zeros((D_MODEL, 128), F32).at[:, 0:N_GROUPS].set(p["w_rg"][l]).at[:, EXPERT_LANE0:EXPERT_LANE0 + N_EXPERTS].set(p["w_re"][l])
    w_r_hi = w_r.astype(BF16)
    w_r_lo = (w_r - w_r_hi.astype(F32)).astype(BF16)
    b_r = jnp.zeros((1, 128), F32).at[0, 0:N_GROUPS].set(p["b_rg"][l]).at[0, EXPERT_LANE0:EXPERT_LANE0 + N_EXPERTS].set(p["b_re"][l])
    return {
        "g_mix": p["g_mix"][l][None, :], "w_in": w_in_p, "g_qn": p["g_qn"][l][None, :], "w_uq": w_uq_p, "w_ukbd": w_ukbd,
        "g_kvn": p["g_kvn"][l][None, :], "w_gk2": w_gk2, "b_gk": p["b_gk"][l][None, :],
        "w_poolbd": w_poolbd, "pool_scale": p["pool_scale"][l][None, :],
        "g_gla": jnp.tile(p["g_gla"][l], GLA_H)[None, :],
        "w_uvbd": w_uvbd, "wo_a": w_out[0:256], "wo_b": w_out[256:768], "wo_c": w_out[768:1024],
        "g_x": p["g_x"][l][None, :], "w_xq": p["w_xq"][l].astype(BF16), "w_xo": p["w_xo"][l].astype(BF16),
        "g_mem": p["g_mem"][l][None, :], "w_xk": p["w_xk"][l].astype(BF16), "w_xv": p["w_xv"][l].astype(BF16),
        "g_ffn": p["g_ffn"][l][None, :], "w_r_hi": w_r_hi, "w_r_lo": w_r_lo, "b_r": b_r,
        "w_eg": p["w_eg"][l].astype(BF16), "w_eu": p["w_eu"][l].astype(BF16), "w_ed": p["w_ed"][l].astype(BF16),
    }


GLA_CHUNK = 64


def kernel(x_prompt, x_sample, mem_prompt, cache_ckv, cache_kpe, page_table, state_pool, state_gla, cache_mem_k, cache_mem_v, g_mix, w_in, w_pool, pool_scale, g_qn, w_uq, g_kvn, w_uk, w_uv, w_gk2, b_gk, g_gla, w_out, g_x, g_mem, w_xq, w_xk, w_xv, w_xo, g_ffn, w_rg, b_rg, w_re, b_re, w_eg, w_eu, w_ed, g_final):
    params = dict(g_mix=g_mix, w_in=w_in, w_pool=w_pool, pool_scale=pool_scale, g_qn=g_qn, w_uq=w_uq, g_kvn=g_kvn, w_uk=w_uk,
                  w_uv=w_uv, w_gk2=w_gk2, b_gk=b_gk, g_gla=g_gla, w_out=w_out, g_x=g_x, g_mem=g_mem, w_xq=w_xq, w_xk=w_xk,
                  w_xv=w_xv, w_xo=w_xo, g_ffn=g_ffn, w_rg=w_rg, b_rg=b_rg, w_re=w_re, b_re=b_re, w_eg=w_eg, w_eu=w_eu, w_ed=w_ed)
    depth = w_in.shape[0]
    bsz, seq, _ = x_prompt.shape
    dbsz, dseq, _ = x_sample.shape
    past_len = page_table.shape[1] * cache_ckv.shape[2]
    gfin = g_final[None, :]
    cache_kpe_t = jnp.swapaxes(cache_kpe, 2, 3)

    cos_p, sin_p = _rope_tables(jnp.arange(seq, dtype=I32))
    cos_s, sin_s = _rope_tables(past_len + (jnp.arange(dbsz * dseq, dtype=I32) % dseq))

    xp = x_prompt.reshape(bsz * seq, D_MODEL)
    xs = x_sample.reshape(dbsz * dseq, D_MODEL)
    mem = mem_prompt.reshape(bsz * N_MEM, D_MODEL)
    gpad = 8 - dseq
    assert 0 <= gpad < 8

    outs = {k: [] for k in ("ckv_p", "kpe_p", "pool_p", "gla_p", "mk_p", "mv_p", "ckv_s", "kpe_s", "pool_s", "gla_s")}
    for l in range(depth):
        w = _prep_layer(l, params)
        final = l == depth - 1
        mk, mv = _memkv(mem, w)
        u, q_hm, kvb, ckv, kpe, gq, gk, gv, go, glog = _inproj(xp, cos_p, sin_p, w)
        ya, pool_new = _pool_prompt(u, bsz, w)
        olat = _mla_prompt(q_hm, kvb, bsz)
        yc, s_new = _gla(gq, gk, gv, go, glog, None, bsz, w, GLA_CHUNK)
        x2, rows, info, nch, tm = _post(xp, ya, olat, yc, mk.reshape(bsz, N_MEM, X_W), mv.reshape(bsz, N_MEM, X_W), bsz, w)
        xp = _moe(x2, rows, info, nch, tm, w, gfin, final)
        outs["ckv_p"].append(ckv.reshape(bsz, seq, KV_LORA))
        outs["kpe_p"].append(kpe.reshape(bsz, seq, ROPE_DIM))
        outs["pool_p"].append(pool_new)
        outs["gla_p"].append(s_new.reshape(bsz, GLA_H, GLA_DK, GLA_DV))
        outs["mk_p"].append(mk.reshape(bsz, N_MEM, X_H, X_HD))
        outs["mv_p"].append(mv.reshape(bsz, N_MEM, X_H, X_HD))
        u, q_hm, kvb, ckv, kpe, gq, gk, gv, go, glog = _inproj(xs, cos_s, sin_s, w)
        ue_tm = jnp.concatenate([state_pool[l], u.reshape(dbsz, dseq, POOL_W)], axis=1).transpose(1, 0, 2)
        ya_tm, st_tm = _pool_sample(ue_tm, w)
        ya = ya_tm.transpose(1, 0, 2).reshape(dbsz * dseq, POOL_W)
        q_s = q_hm.reshape(MLA_H, dbsz, dseq, QK_PAD).transpose(1, 0, 2, 3).reshape(dbsz, MLA_H * dseq, QK_PAD)
        knew = jnp.pad(kvb.reshape(dbsz, dseq, QK_PAD), ((0, 0), (0, 16 - dseq), (0, 0)))
        o_s = _mla_decode(q_s, knew, cache_ckv, cache_kpe_t, page_table, l)
        olat = o_s.reshape(dbsz, MLA_H, dseq, KV_LORA).transpose(0, 2, 1, 3).reshape(dbsz * dseq, MLA_H * KV_LORA).astype(BF16)

        def pad8(a):
            return jnp.pad(a.reshape(dbsz, dseq, -1), ((0, 0), (0, gpad), (0, 0))).reshape(dbsz * 8, -1)

        yc8, s_new = _gla(pad8(gq), pad8(gk), pad8(gv), pad8(go), pad8(glog),
                          state_gla[l].reshape(dbsz, GLA_H * GLA_DK, GLA_DV), dbsz, w, 8)
        yc = yc8.reshape(dbsz, 8, GLA_W)[:, :dseq].reshape(dbsz * dseq, GLA_W)
        x2, rows, info, nch, tm = _post(xs, ya, olat, yc, cache_mem_k.reshape(depth * dbsz, N_MEM, X_H, X_HD),
                                     cache_mem_v.reshape(depth * dbsz, N_MEM, X_H, X_HD), dbsz, w, kv_seq0=l * dbsz)
        xs = _moe(x2, rows, info, nch, tm, w, gfin, final)
        outs["ckv_s"].append(ckv.reshape(dbsz, dseq, KV_LORA))
        outs["kpe_s"].append(kpe.reshape(dbsz, dseq, ROPE_DIM))
        outs["pool_s"].append(st_tm.transpose(1, 0, 2))
        outs["gla_s"].append(s_new.reshape(dbsz, GLA_H, GLA_DK, GLA_DV))

    st = lambda k: jnp.stack(outs[k])
    return (xp.reshape(bsz, seq, D_MODEL), xs.reshape(dbsz, dseq, D_MODEL),
            st("ckv_p"), st("kpe_p"), st("pool_p"), st("gla_p"), st("mk_p"), st("mv_p"),
            st("ckv_s"), st("kpe_s"), st("pool_s"), st("gla_s"))
```

```python
import functools

import numpy as np
import jax
import jax.numpy as jnp
from jax import lax
from jax.experimental import pallas as pl
from jax.experimental.pallas import tpu as pltpu

F32 = jnp.float32
BF16 = jnp.bfloat16
I32 = jnp.int32

EPS = 1e-6
D_MODEL = 1024
POOL_GROUPS, POOL_GC = 4, 64
POOL_W = POOL_GROUPS * POOL_GC
POOL_WINDOWS = (2, 4, 8, 16)
POOL_BUF = 15
MLA_H, Q_LORA, KV_LORA, NOPE, ROPE_DIM, V_DIM = 8, 256, 128, 64, 32, 64
ROPE_BASE = 10000.0
MLA_SCALE = (NOPE + ROPE_DIM) ** -0.5
Q_PRESCALE = MLA_SCALE * 1.4426950408889634
GLA_H, GLA_DK, GLA_DV, GATE_RANK, GATE_TAU = 4, 32, 64, 16, 16.0
GLA_W = GLA_H * GLA_DV
N_MEM, X_H, X_HD = 256, 4, 128
X_W = X_H * X_HD
N_GROUPS, EXP_PER_GROUP, N_EXPERTS, TOP_K, D_EXPERT = 4, 8, 32, 2, 256
QK_PAD = 256
ONES_LANE = KV_LORA + ROPE_DIM
MLA_HEAD_GROUPS = 4
IN_PAD = 1536
EXPERT_LANE0 = 32
MOE_BLOCK = 512
MOE_CHUNK = 16
NEG = -1e30
VMEM_LIMIT = 56 * 1024 * 1024


def _pick(n, prefs):
    for p in prefs:
        if n % p == 0:
            return p
    raise ValueError(f"no tile in {prefs} divides {n}")


def _cparams(n_axes, **flags):
    return pltpu.CompilerParams(dimension_semantics=("arbitrary",) * n_axes, vmem_limit_bytes=VMEM_LIMIT,
                                flags=flags or None)


def _rms(x, g):
    ms = jnp.mean(x * x, axis=-1, keepdims=True)
    return x * lax.rsqrt(ms + EPS) * g


def _dot(a, b):
    return jnp.dot(a, b, preferred_element_type=F32)


def _dot_nt(a, b):
    return lax.dot_general(a, b, (((1,), (1,)), ((), ())), preferred_element_type=F32)


def _dot_tn(a, b):
    return lax.dot_general(a, b, (((0,), (0,)), ((), ())), preferred_element_type=F32)


def _split_bf16(x):
    hi = x.astype(BF16)
    lo = (x - hi.astype(F32)).astype(BF16)
    return hi, lo


def _swap16(x):
    w = x.shape[-1]
    lane = lax.broadcasted_iota(I32, x.shape, x.ndim - 1)
    first = (lane & 31) < 16
    return jnp.where(first, pltpu.roll(x, w - 16, x.ndim - 1), pltpu.roll(x, 16, x.ndim - 1))


def _silu(x):
    return x / (1.0 + jnp.exp(-x))


def _inproj_kernel(x_ref, gmix_ref, win_ref, cos_ref, sin_ref, gqn_ref, wuq_ref, wuk_ref, gkvn_ref, wgk_ref, bgk_ref,
                   u_ref, q_ref, kvb_ref, ckv_ref, kpe_ref, gq_ref, gk_ref, gv_ref, go_ref, glog_ref):
    h = _rms(x_ref[...], gmix_ref[...]).astype(BF16)
    y = _dot(h, win_ref[...])
    u_ref[...] = y[:, 0:256]
    gq_ref[...] = y[:, 768:896]
    gk_ref[...] = y[:, 896:1024]
    gv_ref[...] = y[:, 1024:1280]
    go_ref[...] = y[:, 1280:1536]
    cos = cos_ref[...]
    sin = sin_ref[...]
    ckv = _rms(y[:, 512:640], gkvn_ref[...])
    grp = y[:, 640:768]
    grp_r = grp * cos[:, 0:128] + _swap16(grp) * sin[:, 0:128]
    ckv_ref[...] = ckv
    kpe_ref[...] = grp_r[:, 0:ROPE_DIM]
    lane = lax.broadcasted_iota(I32, grp_r.shape, 1)
    kvb_ref[:, 0:128] = ckv.astype(BF16)
    kvb_ref[:, 128:256] = jnp.where(lane < ROPE_DIM, grp_r, jnp.where(lane == ROPE_DIM, 1.0, 0.0)).astype(BF16)
    gl = _dot(grp.astype(BF16), wgk_ref[...]) + bgk_ref[...]
    glog_ref[...] = (jnp.minimum(gl, 0.0) - jnp.log(1.0 + jnp.exp(-jnp.abs(gl)))) * (1.0 / GATE_TAU)
    cqn = _rms(y[:, 256:512], gqn_ref[...]).astype(BF16)
    q = _dot(cqn, wuq_ref[...])
    qlat = _dot(q[:, 0:512].astype(BF16), wuk_ref[...]) * Q_PRESCALE
    qr = q[:, 512:768]
    qr = (qr * cos + _swap16(qr) * sin) * Q_PRESCALE
    for hh in range(MLA_H):
        q_ref[hh, :, 0:128] = qlat[:, 128 * hh:128 * hh + 128].astype(BF16)
        col = qr[:, 128 * (hh // 4):128 * (hh // 4) + 128]
        sh = (128 - 32 * (hh % 4)) % 128
        if sh:
            col = pltpu.roll(col, sh, 1)
        q_ref[hh, :, 128:256] = jnp.where(lane < ROPE_DIM, col, 0.0).astype(BF16)


def _inproj(x, cos_t, sin_t, w):
    n = x.shape[0]
    tm = _pick(n, (512, 256, 128, 64, 32, 16))
    tab_tiles = cos_t.shape[0] // tm
    assert cos_t.shape[0] % tm == 0
    row = lambda i: (i, 0)
    tab = lambda i: (i % tab_tiles, 0)
    full = lambda i: (0, 0)
    wspec = lambda a: pl.BlockSpec(a.shape, full)
    outs = [
        jax.ShapeDtypeStruct((n, 256), F32),
        jax.ShapeDtypeStruct((MLA_H, n, QK_PAD), BF16),
        jax.ShapeDtypeStruct((n, QK_PAD), BF16),
        jax.ShapeDtypeStruct((n, KV_LORA), F32),
        jax.ShapeDtypeStruct((n, ROPE_DIM), F32),
        jax.ShapeDtypeStruct((n, 128), F32),
        jax.ShapeDtypeStruct((n, 128), F32),
        jax.ShapeDtypeStruct((n, 256), F32),
        jax.ShapeDtypeStruct((n, 256), F32),
        jax.ShapeDtypeStruct((n, 128), F32),
    ]
    out_specs = [
        pl.BlockSpec((tm, 256), row),
        pl.BlockSpec((MLA_H, tm, QK_PAD), lambda i: (0, i, 0)),
        pl.BlockSpec((tm, QK_PAD), row),
        pl.BlockSpec((tm, KV_LORA), row),
        pl.BlockSpec((tm, ROPE_DIM), row),
        pl.BlockSpec((tm, 128), row),
        pl.BlockSpec((tm, 128), row),
        pl.BlockSpec((tm, 256), row),
        pl.BlockSpec((tm, 256), row),
        pl.BlockSpec((tm, 128), row),
    ]
    ins = [x, w["g_mix"], w["w_in"], cos_t, sin_t, w["g_qn"], w["w_uq"], w["w_ukbd"], w["g_kvn"], w["w_gk2"], w["b_gk"]]
    in_specs = [pl.BlockSpec((tm, D_MODEL), row), wspec(w["g_mix"]), wspec(w["w_in"]),
                pl.BlockSpec((tm, 256), tab), pl.BlockSpec((tm, 256), tab),
                wspec(w["g_qn"]), wspec(w["w_uq"]), wspec(w["w_ukbd"]), wspec(w["g_kvn"]), wspec(w["w_gk2"]), wspec(w["b_gk"])]
    return pl.pallas_call(_inproj_kernel, out_shape=outs, grid=(n // tm,), in_specs=in_specs, out_specs=out_specs,
                          compiler_params=_cparams(1), name="inproj")(*ins)


def _pool_mix(sums, u, cnts, wbd, scale):
    cols = []
    for c in range(2):
        wa, wb = POOL_WINDOWS[2 * c], POOL_WINDOWS[2 * c + 1]
        sa, sb = sums[(c, wa)], sums[(c, wb)]
        lane = lax.broadcasted_iota(I32, sa.shape, 1)
        pooled = jnp.where(lane < POOL_GC, sa / cnts[wa], sb / cnts[wb]) - u[:, 128 * c:128 * c + 128]
        cols.append(pooled)
    pooled = jnp.concatenate(cols, axis=1).astype(BF16)
    return (_dot(pooled, wbd) * scale).astype(BF16)


def _pool_kernel(u_ref, wbd_ref, scale_ref, ya_ref, st_ref, ue_ref, *, tp, nt):
    j = pl.program_id(1)

    @pl.when(j == 0)
    def _():
        ue_ref[0:16, :] = jnp.zeros((16, POOL_W), F32)

    @pl.when(j > 0)
    def _():
        ue_ref[0:16, :] = ue_ref[tp:tp + 16, :]

    u = u_ref[...]
    ue_ref[16:16 + tp, :] = u
    t = j * tp + lax.broadcasted_iota(I32, (tp, 1), 0)
    cnts = {w: jnp.minimum(t + 1, w).astype(F32) for w in POOL_WINDOWS}
    sums = {}
    for c in range(2):
        wa, wb = POOL_WINDOWS[2 * c], POOL_WINDOWS[2 * c + 1]
        acc = None
        for k in range(wb):
            sl = ue_ref[16 - k:16 - k + tp, 128 * c:128 * c + 128]
            acc = sl if acc is None else acc + sl
            if k + 1 == wa:
                sums[(c, wa)] = acc
        sums[(c, wb)] = acc
    ya_ref[...] = _pool_mix(sums, u, cnts, wbd_ref[...], scale_ref[...])

    @pl.when(j == nt - 1)
    def _():
        st_ref[0] = ue_ref[tp + 1:tp + 16, :]


def _pool_prompt(u, bsz, w):
    n = u.shape[0]
    t = n // bsz
    tp = _pick(t, (512, 256, 128, 64, 32, 16))
    nt = t // tp
    kern = functools.partial(_pool_kernel, tp=tp, nt=nt)
    return pl.pallas_call(
        kern,
        out_shape=[jax.ShapeDtypeStruct((n, POOL_W), BF16), jax.ShapeDtypeStruct((bsz, POOL_BUF, POOL_W), F32)],
        grid=(bsz, nt),
        in_specs=[pl.BlockSpec((tp, POOL_W), lambda b, j: (b * nt + j, 0)),
                  pl.BlockSpec((POOL_W, POOL_W), lambda b, j: (0, 0)),
                  pl.BlockSpec((1, POOL_W), lambda b, j: (0, 0))],
        out_specs=[pl.BlockSpec((tp, POOL_W), lambda b, j: (b * nt + j, 0)),
                   pl.BlockSpec((1, POOL_BUF, POOL_W), lambda b, j: (b, 0, 0))],
        scratch_shapes=[pltpu.VMEM((tp + 16, POOL_W), F32)],
        compiler_params=_cparams(2), name="pool_prompt")(u, w["w_poolbd"], w["pool_scale"])


def _pool_step_kernel(ue_ref, wbd_ref, scale_ref, ya_ref, st_ref, *, tt):
    cnts = {w: jnp.float32(w) for w in POOL_WINDOWS}
    for t in range(tt):
        sums = {}
        for c in range(2):
            wa, wb = POOL_WINDOWS[2 * c], POOL_WINDOWS[2 * c + 1]
            acc = None
            for k in range(wb):
                sl = ue_ref[POOL_BUF + t - k, :, 128 * c:128 * c + 128]
                acc = sl if acc is None else acc + sl
                if k + 1 == wa:
                    sums[(c, wa)] = acc
            sums[(c, wb)] = acc
        ya_ref[t] = _pool_mix(sums, ue_ref[POOL_BUF + t], cnts, wbd_ref[...], scale_ref[...])
    for r in range(POOL_BUF):
        st_ref[r] = ue_ref[tt + r]


def _pool_sample(ue_tm, w):
    rows, bsz, _ = ue_tm.shape
    tt = rows - POOL_BUF
    kern = functools.partial(_pool_step_kernel, tt=tt)
    full3 = lambda i: (0, 0, 0)
    return pl.pallas_call(
        kern,
        out_shape=[jax.ShapeDtypeStruct((tt, bsz, POOL_W), BF16), jax.ShapeDtypeStruct((POOL_BUF, bsz, POOL_W), F32)],
        grid=(1,),
        in_specs=[pl.BlockSpec(ue_tm.shape, full3), pl.BlockSpec((POOL_W, POOL_W), lambda i: (0, 0)),
                  pl.BlockSpec((1, POOL_W), lambda i: (0, 0))],
        out_specs=[pl.BlockSpec((tt, bsz, POOL_W), full3), pl.BlockSpec((POOL_BUF, bsz, POOL_W), full3)],
        compiler_params=_cparams(1), name="pool_sample")(ue_tm, w["w_poolbd"], w["pool_scale"])


def _mla_prompt_kernel(qi_ref, kj_ref, q_ref, k_ref, o_ref, m_ref, acc_ref, *, tq, tk):
    p = pl.program_id(1)
    qi = qi_ref[p]
    kj = kj_ref[p]
    rows = MLA_H * tq

    @pl.when(kj == 0)
    def _():
        m_ref[...] = jnp.full((rows, 128), NEG, F32)
        acc_ref[...] = jnp.zeros((rows, QK_PAD), F32)

    def step(masked, nk):
        k = k_ref[0:nk, :]
        hg = MLA_H // MLA_HEAD_GROUPS
        gr = hg * tq
        for g in range(MLA_HEAD_GROUPS):
            rs = slice(g * gr, (g + 1) * gr)
            q = q_ref[g * hg:(g + 1) * hg].reshape(gr, QK_PAD)
            s = _dot_nt(q, k)
            if masked:
                r = lax.broadcasted_iota(I32, (gr, nk), 0)
                c = lax.broadcasted_iota(I32, (gr, nk), 1)
                qpos = qi * tq + (r & (tq - 1))
                s = jnp.where(kj * tk + c <= qpos, s, NEG)
            m_old = m_ref[rs, :]
            m_new = jnp.maximum(m_old, jnp.max(s, axis=-1, keepdims=True))
            alpha = jnp.exp2(m_old - m_new)
            pr = jnp.exp2(s - jnp.concatenate([m_new] * (nk // 128), axis=1))
            acc_ref[rs, :] = jnp.concatenate([alpha, alpha], axis=1) * acc_ref[rs, :] + _dot(pr.astype(BF16), k)
            m_ref[rs, :] = m_new

    visible = (qi + 1) * tq - kj * tk
    half = tk // 2 if (tk % 256 == 0 and tq <= tk // 2) else tk
    crosses = visible < tk + tq

    if half < tk:
        @pl.when(jnp.logical_and(crosses, visible <= half))
        def _():
            step(True, half)

    @pl.when(jnp.logical_and(crosses, visible > half) if half < tk else crosses)
    def _():
        step(True, tk)

    @pl.when(jnp.logical_not(crosses))
    def _():
        step(False, tk)

    @pl.when(kj == ((qi + 1) * tq - 1) // tk)
    def _():
        acc = acc_ref[...]
        o = acc[:, 0:KV_LORA] / acc[:, ONES_LANE:ONES_LANE + 1]
        for h in range(MLA_H):
            o_ref[:, KV_LORA * h:KV_LORA * (h + 1)] = o[h * tq:(h + 1) * tq].astype(BF16)


def _mla_prompt(q_hm, kvb, bsz):
    n = kvb.shape[0]
    t = n // bsz
    tq = _pick(t, (512, 256, 128, 64, 32, 16))
    tk = _pick(t, (512, 256, 128, 64, 32, 16))
    assert tq & (tq - 1) == 0
    nq, nk = t // tq, t // tk
    pairs = [(i, j) for i in range(nq) for j in range(((i + 1) * tq - 1) // tk + 1)]
    qi = jnp.asarray(np.array([p[0] for p in pairs], np.int32))
    kj = jnp.asarray(np.array([p[1] for p in pairs], np.int32))
    rows = MLA_H * tq
    kern = functools.partial(_mla_prompt_kernel, tq=tq, tk=tk)
    grid_spec = pltpu.PrefetchScalarGridSpec(
        num_scalar_prefetch=2, grid=(bsz, len(pairs)),
        in_specs=[pl.BlockSpec((MLA_H, tq, QK_PAD), lambda b, p, qi, kj: (0, b * nq + qi[p], 0)),
                  pl.BlockSpec((tk, QK_PAD), lambda b, p, qi, kj: (b * nk + kj[p], 0))],
        out_specs=pl.BlockSpec((tq, MLA_H * KV_LORA), lambda b, p, qi, kj: (b * nq + qi[p], 0)),
        scratch_shapes=[pltpu.VMEM((rows, 128), F32), pltpu.VMEM((rows, QK_PAD), F32)])
    return pl.pallas_call(kern, out_shape=jax.ShapeDtypeStruct((n, MLA_H * KV_LORA), BF16), grid_spec=grid_spec,
                          compiler_params=_cparams(2), name="mla_prompt")(qi, kj, q_hm, kvb)


def _mla_decode_kernel(pt_ref, q_ref, knew_ref, cckv_ref, ckpe_ref, o_ref,
                       cbuf, pbuf, csem, psem, m_ref, l_ref, acc_ref, *, layer, ng, gp, page, tt, total):
    s_idx = pl.program_id(0)
    grp = s_idx % ng
    slot = s_idx % 2

    def copies(step, slot_):
        out = []
        for g in range(gp):
            pg = pt_ref[step * gp + g]
            out.append(pltpu.make_async_copy(cckv_ref.at[layer, pg], cbuf.at[slot_, pl.ds(g * page, page)], csem.at[slot_]))
            out.append(pltpu.make_async_copy(ckpe_ref.at[layer, pg], pbuf.at[slot_, :, pl.ds(g * page, page)], psem.at[slot_]))
        return out

    @pl.when(s_idx == 0)
    def _():
        for c in copies(0, 0):
            c.start()

    @pl.when(grp == 0)
    def _():
        m_ref[...] = jnp.full(m_ref.shape, NEG, F32)
        l_ref[...] = jnp.zeros(l_ref.shape, F32)
        acc_ref[...] = jnp.zeros(acc_ref.shape, F32)

    for c in copies(s_idx, slot):
        c.wait()
    nxt = jnp.minimum(s_idx + 1, total - 1)
    for c in copies(nxt, 1 - slot):
        c.start()

    q = q_ref[0]

    def update(s, v):
        m_old = m_ref[...]
        m_new = jnp.maximum(m_old, jnp.max(s, axis=-1, keepdims=True))
        alpha = jnp.exp2(m_old - m_new)
        pr = jnp.exp2(s - m_new[:, 0:1])
        l_ref[...] = alpha * l_ref[...] + jnp.sum(pr, axis=-1, keepdims=True)
        acc_ref[...] = alpha * acc_ref[...] + _dot(pr.astype(BF16), v)
        m_ref[...] = m_new

    kc = cbuf[slot].astype(BF16)
    kr_t = pbuf[slot].astype(BF16)
    s = _dot_nt(q[:, 0:KV_LORA], kc) + _dot(q[:, KV_LORA:KV_LORA + ROPE_DIM], kr_t)
    update(s, kc)

    @pl.when(grp == ng - 1)
    def _():
        kn = knew_ref[0]
        sn = _dot_nt(q, kn)
        r = lax.broadcasted_iota(I32, sn.shape, 0)
        c = lax.broadcasted_iota(I32, sn.shape, 1)
        sn = jnp.where(c <= (r % tt), sn, NEG)
        update(sn, kn[:, 0:KV_LORA])
        o_ref[0] = acc_ref[...] / l_ref[...]

    @pl.when(s_idx == total - 1)
    def _():
        for c in copies(total - 1, 1 - slot):
            c.wait()


def _mla_decode(q_s, knew, cache_ckv, cache_kpe_t, page_table, layer):
    dbsz, rows, _ = q_s.shape
    tt = rows // MLA_H
    n_pages = page_table.shape[1]
    page = cache_ckv.shape[2]
    gp = _pick(n_pages, (64, 32, 16, 8, 4, 2, 1))
    ng = n_pages // gp
    total = dbsz * ng
    kern = functools.partial(_mla_decode_kernel, layer=layer, ng=ng, gp=gp, page=page, tt=tt, total=total)
    grid_spec = pltpu.PrefetchScalarGridSpec(
        num_scalar_prefetch=1, grid=(total,),
        in_specs=[pl.BlockSpec((1, rows, QK_PAD), lambda s, pt: (s // ng, 0, 0)),
                  pl.BlockSpec((1, 16, QK_PAD), lambda s, pt: (s // ng, 0, 0)),
                  pl.BlockSpec(memory_space=pl.ANY), pl.BlockSpec(memory_space=pl.ANY)],
        out_specs=pl.BlockSpec((1, rows, KV_LORA), lambda s, pt: (s // ng, 0, 0)),
        scratch_shapes=[pltpu.VMEM((2, gp * page, KV_LORA), F32), pltpu.VMEM((2, ROPE_DIM, gp * page), F32),
                        pltpu.SemaphoreType.DMA((2,)), pltpu.SemaphoreType.DMA((2,)),
                        pltpu.VMEM((rows, 128), F32), pltpu.VMEM((rows, 128), F32), pltpu.VMEM((rows, KV_LORA), F32)])
    return pl.pallas_call(kern, out_shape=jax.ShapeDtypeStruct((dbsz, rows, KV_LORA), F32), grid_spec=grid_spec,
                          compiler_params=_cparams(1), name="mla_decode")(
        page_table.reshape(-1), q_s, knew, cache_ckv, cache_kpe_t)


def _gla_constants(c):
    t = np.arange(c)
    blocks = [(t[None, :] <= t[:, None]), (t[None, :] > t[:, None])]
    masks = []
    m = c // 2
    while m >= 1:
        bd = (t // (2 * m)) * 2 * m + m
        upper = t >= bd
        a = upper[:, None] & (t[None, :] >= bd[:, None]) & (t[None, :] <= t[:, None])
        b = (~upper)[:, None] & (t[None, :] > t[:, None]) & (t[None, :] <= bd[:, None] - 1)
        blocks += [a, b]
        same = (t[:, None] // (2 * m)) == (t[None, :] // (2 * m))
        masks.append(same & upper[:, None] & (~upper)[None, :])
        m //= 2
    masks.append(t[:, None] == t[None, :])
    sel = np.concatenate(blocks, axis=0).astype(np.float32)
    masks = np.concatenate([np.tile(mm, (1, GLA_H)) for mm in masks], axis=0).astype(np.float32)
    r = np.arange(GLA_H * c)
    kmask = ((r[:, None] // c) == (np.arange(128)[None, :] // GLA_DK)).astype(np.float32)
    vmask = ((r[:, None] // c) == (np.arange(GLA_W)[None, :] // GLA_DV)).astype(np.float32)
    smask = ((np.arange(128)[:, None] // GLA_DK) == (np.arange(GLA_W)[None, :] // GLA_DV)).astype(np.float32)
    last = np.zeros((c, GLA_W), np.float32)
    last[c - 1, :] = 1.0
    gsum = ((np.arange(GLA_W)[:, None] // GLA_DV) == (np.arange(GLA_W)[None, :] // GLA_DV)).astype(np.float32)
    return sel, masks, kmask, vmask, smask, last, gsum


def _gla_kernel(*refs, c, nsub, nsteps, has_init):
    if has_init:
        (q_ref, k_ref, v_ref, go_ref, g_ref, s0_ref, sel_ref, masks_ref, kmask_ref, vmask_ref, smask_ref, last_ref,
         gsum_ref, ggla_ref, y_ref, sout_ref, s_ref) = refs
    else:
        (q_ref, k_ref, v_ref, go_ref, g_ref, sel_ref, masks_ref, kmask_ref, vmask_ref, smask_ref, last_ref,
         gsum_ref, ggla_ref, y_ref, sout_ref, s_ref) = refs
        s0_ref = None
    j = pl.program_id(1)
    nlev = int(np.log2(c))
    smask = smask_ref[...]

    @pl.when(j == 0)
    def _():
        if has_init:
            s0 = s0_ref[0]
            s_ref[...] = jnp.concatenate([s0] * GLA_H, axis=1) * smask
        else:
            s_ref[...] = jnp.zeros(s_ref.shape, F32)

    def chunk(i, carry):
        sl = pl.ds(pl.multiple_of(i * c, c), c)
        q = q_ref[sl, :] * (GLA_DK ** -0.5)
        k = k_ref[sl, :]
        v = v_ref[sl, :].astype(BF16)
        g_hi, g_lo = _split_bf16(g_ref[sl, :])
        sel = sel_ref[...]
        e_all = jnp.exp(_dot(sel, g_hi) + _dot(sel, g_lo))
        eb = e_all[0:c]
        s_old = s_ref[...]
        o = _dot((q * eb).astype(BF16), s_old.astype(BF16))
        kmask = kmask_ref[...]
        att = None
        for lev in range(nlev + 1):
            if lev < nlev:
                ql = (q * e_all[(2 + 2 * lev) * c:(3 + 2 * lev) * c]).astype(BF16)
                kl = k * e_all[(3 + 2 * lev) * c:(4 + 2 * lev) * c]
            else:
                ql, kl = q.astype(BF16), k
            kb = (jnp.concatenate([kl] * GLA_H, axis=0) * kmask).astype(BF16)
            part = _dot_nt(ql, kb) * masks_ref[lev * c:(lev + 1) * c, :]
            att = part if att is None else att + part
        vb = (jnp.concatenate([v] * GLA_H, axis=0) * vmask_ref[...].astype(BF16))
        o = o + _dot(att.astype(BF16), vb)
        kk = (k * e_all[c:2 * c]).astype(BF16)
        eb_hi, eb_lo = _split_bf16(eb)
        last = last_ref[...].astype(BF16)
        decay = _dot_tn(eb_hi, last) + _dot_tn(eb_lo, last)
        s_ref[...] = (decay * s_old + _dot_tn(kk, v)) * smask
        o2_hi, o2_lo = _split_bf16(o * o)
        gsum = gsum_ref[...]
        ms = (_dot(o2_hi, gsum) + _dot(o2_lo, gsum)) * (1.0 / GLA_DV)
        yc = o * lax.rsqrt(ms + EPS) * ggla_ref[...] * _silu(go_ref[sl, :])
        y_ref[sl, :] = yc.astype(BF16)
        return carry

    lax.fori_loop(0, nsub, chunk, 0, unroll=True)

    @pl.when(j == nsteps - 1)
    def _():
        s = s_ref[...]
        acc = s[:, 0:GLA_DV]
        for h in range(1, GLA_H):
            acc = acc + s[:, GLA_DV * h:GLA_DV * (h + 1)]
        sout_ref[0] = acc


def _gla(gq, gk, gv, go, glog, s0, bsz, w, chunk):
    n = gq.shape[0]
    t = n // bsz
    c = min(chunk, t)
    assert t % c == 0 and c & (c - 1) == 0 and c >= 8
    nsub = _pick(t // c, (8, 4, 2, 1))
    tc = c * nsub
    nsteps = t // tc
    consts = [jnp.asarray(a) for a in _gla_constants(c)]
    consts[0] = consts[0].astype(BF16)
    consts[6] = consts[6].astype(BF16)
    has_init = s0 is not None
    kern = functools.partial(_gla_kernel, c=c, nsub=nsub, nsteps=nsteps, has_init=has_init)
    row = lambda b, j: (b * nsteps + j, 0)
    full = lambda b, j: (0, 0)
    ins = [gq, gk, gv, go, glog]
    in_specs = [pl.BlockSpec((tc, 128), row), pl.BlockSpec((tc, 128), row), pl.BlockSpec((tc, 256), row),
                pl.BlockSpec((tc, 256), row), pl.BlockSpec((tc, 128), row)]
    if has_init:
        ins.append(s0)
        in_specs.append(pl.BlockSpec((1, 128, GLA_DV), lambda b, j: (b, 0, 0)))
    ins += consts + [w["g_gla"]]
    in_specs += [pl.BlockSpec(a.shape, full) for a in consts] + [pl.BlockSpec(w["g_gla"].shape, full)]
    return pl.pallas_call(
        kern,
        out_shape=[jax.ShapeDtypeStruct((n, GLA_W), BF16), jax.ShapeDtypeStruct((bsz, 128, GLA_DV), F32)],
        grid=(bsz, nsteps), in_specs=in_specs,
        out_specs=[pl.BlockSpec((tc, GLA_W), row), pl.BlockSpec((1, 128, GLA_DV), lambda b, j: (b, 0, 0))],
        scratch_shapes=[pltpu.VMEM((128, GLA_W), F32)],
        compiler_params=_cparams(2), name="gla")(*ins)


def _memkv_kernel(m_ref, g_ref, wk_ref, wv_ref, mk_ref, mv_ref):
    h = _rms(m_ref[...], g_ref[...]).astype(BF16)
    mk_ref[...] = _dot(h, wk_ref[...])
    mv_ref[...] = _dot(h, wv_ref[...])


def _memkv(mem, w):
    n = mem.shape[0]
    tm = _pick(n, (256, 128, 64, 32, 16, 8))
    row = lambda i: (i, 0)
    full = lambda i: (0, 0)
    return pl.pallas_call(
        _memkv_kernel,
        out_shape=[jax.ShapeDtypeStruct((n, X_W), F32)] * 2, grid=(n // tm,),
        in_specs=[pl.BlockSpec((tm, D_MODEL), row), pl.BlockSpec((1, D_MODEL), full),
                  pl.BlockSpec((D_MODEL, X_W), full), pl.BlockSpec((D_MODEL, X_W), full)],
        out_specs=[pl.BlockSpec((tm, X_W), row)] * 2,
        compiler_params=_cparams(1), name="memkv")(mem, w["g_mem"], w["w_xk"], w["w_xv"])


def _post_kernel(x_ref, ya_ref, ol_ref, yc_ref, mk_ref, mv_ref, wuv_ref, woa_ref, wob_ref, woc_ref, gx_ref, wxq_ref, wxo_ref,
                 gffn_ref, wrh_ref, wrl_ref, br_ref, ltri_ref, ustr_ref,
                 x2_ref, rows_ref, info_ref, nch_ref, *, nb, rpb):
    tm = x_ref.shape[0]
    lr = rows_ref.shape[0]
    yb = _dot(ol_ref[...], wuv_ref[...]).astype(BF16)
    x1 = x_ref[...] + _dot(ya_ref[...], woa_ref[...]) + _dot(yb, wob_ref[...]) + _dot(yc_ref[...], woc_ref[...])
    q = _dot(_rms(x1, gx_ref[...]).astype(BF16), wxq_ref[...])
    def head_kv(ref, h):
        if len(ref.shape) == 4:
            a = ref[:, :, h, :]
        else:
            a = ref[:, :, X_HD * h:X_HD * (h + 1)]
        return a.reshape(nb * N_MEM, X_HD).astype(BF16)

    outs = []
    for h in range(X_H):
        sl = slice(X_HD * h, X_HD * (h + 1))
        s = _dot_nt(q[:, sl].astype(BF16), head_kv(mk_ref, h)) * (X_HD ** -0.5)
        if nb > 1:
            r = lax.broadcasted_iota(I32, s.shape, 0)
            c = lax.broadcasted_iota(I32, s.shape, 1)
            s = jnp.where(r // rpb == c // N_MEM, s, NEG)
        e = jnp.exp(s - jnp.max(s, axis=-1, keepdims=True))
        p = e / jnp.sum(e, axis=-1, keepdims=True)
        outs.append(_dot(p.astype(BF16), head_kv(mv_ref, h)))
    o = jnp.concatenate(outs, axis=1).astype(BF16)
    x2 = x1 + _dot(o, wxo_ref[...])
    x2_ref[...] = x2
    h3 = _rms(x2, gffn_ref[...])
    h_hi, h_lo = _split_bf16(h3)
    logit = _dot(h_hi, wrh_ref[...]) + _dot(h_lo, wrh_ref[...]) + _dot(h_hi, wrl_ref[...]) + br_ref[...]
    lane = lax.broadcasted_iota(I32, logit.shape, 1)
    gl = jnp.where(lane < N_GROUPS, logit, NEG)
    gmax = jnp.max(gl, axis=-1, keepdims=True)
    gsel = jnp.min(jnp.where(gl == gmax, lane, 1 << 20), axis=-1, keepdims=True)
    gw = 1.0 / jnp.sum(jnp.exp(gl - gmax), axis=-1, keepdims=True)
    emask = jnp.logical_and(lane >= EXPERT_LANE0, (lane - EXPERT_LANE0) // EXP_PER_GROUP == gsel)
    el = jnp.where(emask, logit, NEG)
    pe = jnp.where(emask, jnp.exp(el - jnp.max(el, axis=-1, keepdims=True)), 0.0)
    prob = pe / jnp.sum(pe, axis=-1, keepdims=True)
    prob = jnp.where(emask, prob, -1.0)
    p1 = jnp.max(prob, axis=-1, keepdims=True)
    i1 = jnp.min(jnp.where(prob == p1, lane, 1 << 20), axis=-1, keepdims=True)
    prob2 = jnp.where(lane == i1, -1.0, prob)
    p2 = jnp.max(prob2, axis=-1, keepdims=True)
    i2 = jnp.min(jnp.where(prob2 == p2, lane, 1 << 20), axis=-1, keepdims=True)
    gate1 = gw * p1 / (p1 + p2)
    gate2 = gw * p2 / (p1 + p2)
    oh1 = (lane == i1).astype(F32)
    oh2 = (lane == i2).astype(F32)
    both = oh1 + oh2
    before = _dot(ltri_ref[...], both.astype(BF16))
    cnt = jnp.sum(both, axis=0, keepdims=True)
    nch = jnp.floor((cnt + (MOE_CHUNK - 1.0)) * (1.0 / MOE_CHUNK))
    nch8 = jnp.broadcast_to(nch, (8, 128))
    off = _dot(nch8.astype(BF16), ustr_ref[...])[0:1] * float(MOE_CHUNK)
    pos = off + before
    v1 = oh1 * pos
    v2 = oh2 * pos
    loc1 = jnp.sum(v1, axis=-1, keepdims=True)
    loc2 = jnp.sum(v2, axis=-1, keepdims=True)
    ones8 = jnp.ones((8, 128), BF16)

    def as_row(v):
        hi = jnp.floor(v * (1.0 / 32.0))
        lo = v - 32.0 * hi
        return (32.0 * _dot_nt(ones8, hi.astype(BF16)) + _dot_nt(ones8, lo.astype(BF16)))[0:1]

    slot_row = lax.broadcasted_iota(I32, (lr, tm), 0).astype(F32)
    perm = jnp.logical_or(slot_row == as_row(v1), slot_row == as_row(v2))
    perm = jnp.where(perm, 1.0, 0.0).astype(BF16)
    rows_ref[...] = _dot(perm, h3.astype(BF16)).astype(BF16)
    nch_ref[...] = nch8
    info = jnp.where(lane == 2, gate1, 0.0)
    info = jnp.where(lane == 3, gate2, info)
    info = jnp.where(lane == 4, loc1, info)
    info = jnp.where(lane == 5, loc2, info)
    info_ref[...] = info


def _local_rows(tm):
    need = TOP_K * tm + N_EXPERTS * (MOE_CHUNK - 1) + MOE_CHUNK
    return -(-need // 128) * 128


def _post(x, ya, olat, yc, mk, mv, bsz, w, kv_seq0=0):
    n = x.shape[0]
    t = n // bsz
    if t >= 16:
        tm = _pick(t, (512, 256, 128, 64, 32, 16))
        nb, rpb = 1, tm
        grid = (bsz, t // tm)
        row = lambda b, j: (b * (t // tm) + j, 0)
        kvm = lambda b, j: (kv_seq0 + b, 0, 0)
    else:
        nb = _pick(bsz, (8, 4, 2, 1))
        tm, rpb = nb * t, t
        assert tm % 16 == 0 and kv_seq0 % nb == 0
        grid = (bsz // nb, 1)
        row = lambda b, j: (b, 0)
        kvm = lambda b, j: (kv_seq0 // nb + b, 0, 0)
    ltri = jnp.asarray(np.tril(np.ones((tm, tm), np.float32), -1)).astype(BF16)
    ustr = jnp.asarray(np.triu(np.ones((128, 128), np.float32), 1)).astype(BF16)
    lr = _local_rows(tm)
    ntile = n // tm
    tile = lambda b, j: (row(b, j)[0], 0)
    full = lambda b, j: (0, 0)
    wnames = ["w_uvbd", "wo_a", "wo_b", "wo_c", "g_x", "w_xq", "w_xo", "g_ffn", "w_r_hi", "w_r_lo", "b_r"]
    ins = [x, ya, olat, yc, mk, mv] + [w[k] for k in wnames] + [ltri, ustr]
    in_specs = [pl.BlockSpec((tm, D_MODEL), row), pl.BlockSpec((tm, POOL_W), row), pl.BlockSpec((tm, MLA_H * KV_LORA), row),
                pl.BlockSpec((tm, GLA_W), row)]
    if mk.ndim == 4:
        kvm4 = lambda b, j: kvm(b, j) + (0,)
        in_specs += [pl.BlockSpec((nb, N_MEM, X_H, X_HD), kvm4)] * 2
    else:
        in_specs += [pl.BlockSpec((nb, N_MEM, X_W), kvm)] * 2
    in_specs += [pl.BlockSpec(w[k].shape, full) for k in wnames] + [pl.BlockSpec((tm, tm), full), pl.BlockSpec((128, 128), full)]
    kern = functools.partial(_post_kernel, nb=nb, rpb=rpb)
    x2, rows, info, nch = pl.pallas_call(
        kern,
        out_shape=[jax.ShapeDtypeStruct((n, D_MODEL), F32), jax.ShapeDtypeStruct((ntile * lr, D_MODEL), BF16),
                   jax.ShapeDtypeStruct((n, 128), F32), jax.ShapeDtypeStruct((ntile * 8, 128), F32)],
        grid=grid, in_specs=in_specs,
        out_specs=[pl.BlockSpec((tm, D_MODEL), row), pl.BlockSpec((lr, D_MODEL), tile), pl.BlockSpec((tm, 128), row),
                   pl.BlockSpec((8, 128), tile)],
        compiler_params=_cparams(2), name="post")(*ins)
    return x2, rows, info, nch, tm


def _expert_kernel(sc_ref, be_ref, nu_ref, rows_hbm, wg_ref, wu_ref, wd_ref, yinit_hbm, y_hbm, xbuf, ybuf, gsem, ssem, *, dummy):
    del yinit_hbm
    j = pl.program_id(0)
    nu = nu_ref[0]
    slot = j % 2
    cpb = MOE_BLOCK // MOE_CHUNK

    def gather(blk, s):
        out = []
        for c in range(cpb):
            src = pl.multiple_of(sc_ref[blk * cpb + c] * MOE_CHUNK, MOE_CHUNK)
            out.append(pltpu.make_async_copy(rows_hbm.at[pl.ds(src, MOE_CHUNK)],
                                             xbuf.at[s, pl.ds(c * MOE_CHUNK, MOE_CHUNK)], gsem.at[s]))
        return out

    def writeback(blk, s, act):
        for c in range(cpb):
            chunk = sc_ref[blk * cpb + c]

            @pl.when(chunk != dummy)
            def _():
                dst = pl.multiple_of(chunk * MOE_CHUNK, MOE_CHUNK)
                act(pltpu.make_async_copy(ybuf.at[s, pl.ds(c * MOE_CHUNK, MOE_CHUNK)],
                                          y_hbm.at[pl.ds(dst, MOE_CHUNK)], ssem.at[s]))

    @pl.when(j < nu)
    def _():
        @pl.when(j == 0)
        def _():
            for cp in gather(0, 0):
                cp.start()

        @pl.when(j + 1 < nu)
        def _():
            for cp in gather(j + 1, 1 - slot):
                cp.start()

        for cp in gather(j, slot):
            cp.wait()

        @pl.when(j >= 2)
        def _():
            writeback(j - 2, slot, lambda cp: cp.wait())

        xb = xbuf[slot]
        hid = _silu(_dot(xb, wg_ref[...])) * _dot(xb, wu_ref[...])
        ybuf[slot] = _dot(hid.astype(BF16), wd_ref[...]).astype(BF16)
        writeback(j, slot, lambda cp: cp.start())

        @pl.when(j == nu - 1)
        def _():
            writeback(j, slot, lambda cp: cp.wait())

            @pl.when(j >= 1)
            def _():
                writeback(j - 1, 1 - slot, lambda cp: cp.wait())


def _experts(rows, src_chunk, block_e, n_used, dummy, w):
    n_rows = rows.shape[0]
    nblk = block_e.shape[0]
    y_init = jnp.zeros((n_rows, D_MODEL), BF16)
    wmap = lambda j, sc, be, nu: (be[j], 0, 0)
    grid_spec = pltpu.PrefetchScalarGridSpec(
        num_scalar_prefetch=3, grid=(nblk,),
        in_specs=[pl.BlockSpec(memory_space=pl.ANY),
                  pl.BlockSpec((None, D_MODEL, D_EXPERT), wmap), pl.BlockSpec((None, D_MODEL, D_EXPERT), wmap),
                  pl.BlockSpec((None, D_EXPERT, D_MODEL), wmap),
                  pl.BlockSpec(memory_space=pl.ANY)],
        out_specs=pl.BlockSpec(memory_space=pl.ANY),
        scratch_shapes=[pltpu.VMEM((2, MOE_BLOCK, D_MODEL), BF16), pltpu.VMEM((2, MOE_BLOCK, D_MODEL), BF16),
                        pltpu.SemaphoreType.DMA((2,)), pltpu.SemaphoreType.DMA((2,))])
    return pl.pallas_call(functools.partial(_expert_kernel, dummy=dummy),
                          out_shape=jax.ShapeDtypeStruct((n_rows, D_MODEL), BF16), grid_spec=grid_spec,
                          input_output_aliases={7: 0}, compiler_params=_cparams(1), name="moe_experts")(
        src_chunk, block_e, n_used, rows, w["w_eg"], w["w_eu"], w["w_ed"], y_init)


def _combine_kernel(x_ref, info_ref, y_ref, g_ref, o_ref, *, final):
    tm, lr = x_ref.shape[0], y_ref.shape[0]
    info = info_ref[...]
    col = lax.broadcasted_iota(I32, (tm, lr), 1).astype(F32)
    gmat = jnp.where(col == info[:, 4:5], info[:, 2:3], 0.0) + jnp.where(col == info[:, 5:6], info[:, 3:4], 0.0)
    y = x_ref[...] + _dot(gmat.astype(BF16), y_ref[...])
    if final:
        y = _rms(y, g_ref[...])
    o_ref[...] = y


def _combine(x2, info, y_loc, tm, g_final, final):
    n = x2.shape[0]
    lr = _local_rows(tm)
    return pl.pallas_call(
        functools.partial(_combine_kernel, final=final),
        out_shape=jax.ShapeDtypeStruct((n, D_MODEL), F32), grid=(n // tm,),
        in_specs=[pl.BlockSpec((tm, D_MODEL), lambda i: (i, 0)), pl.BlockSpec((tm, 128), lambda i: (i, 0)),
                  pl.BlockSpec((lr, D_MODEL), lambda i: (i, 0)), pl.BlockSpec((1, D_MODEL), lambda i: (0, 0))],
        out_specs=pl.BlockSpec((tm, D_MODEL), lambda i: (i, 0)),
        compiler_params=_cparams(1), name="moe_combine")(x2, info, y_loc, g_final)


def _moe(x2, rows, info, nch, tm, w, g_final, final):
    n = x2.shape[0]
    ntile = n // tm
    lrc = _local_rows(tm) // MOE_CHUNK
    cpb = MOE_BLOCK // MOE_CHUNK
    dummy = lrc - 1
    nc = nch.reshape(ntile, 8, 128)[:, 0, EXPERT_LANE0:EXPERT_LANE0 + N_EXPERTS].astype(I32)
    src_base = jnp.arange(ntile, dtype=I32)[:, None] * lrc + (jnp.cumsum(nc, axis=1) - nc)
    pe = (jnp.sum(nc, axis=0) + cpb - 1) // cpb * cpb
    pend = jnp.cumsum(pe)
    dst = (pend - pe)[None, :] + (jnp.cumsum(nc, axis=0) - nc)
    dst_f, nc_f, src_f = dst.T.reshape(-1), nc.T.reshape(-1), src_base.T.reshape(-1)
    max_chunks = ntile * ((TOP_K * tm + N_EXPERTS * (MOE_CHUNK - 1)) // MOE_CHUNK) + N_EXPERTS * (cpb - 1)
    nblk = -(-max_chunks // cpb)
    pos = jnp.arange(nblk * cpb, dtype=I32)
    run = jnp.sum((dst_f[None, :] <= pos[:, None]).astype(I32), axis=1) - 1
    at_run = jnp.stack([dst_f, nc_f, src_f], axis=1)[run]
    rel = pos - at_run[:, 0]
    src_chunk = jnp.where(rel < at_run[:, 1], at_run[:, 2] + rel, dummy).astype(I32)
    blk0 = jnp.arange(nblk, dtype=I32) * cpb
    block_e = jnp.minimum(jnp.sum((pend[None, :] <= blk0[:, None]).astype(I32), axis=1), N_EXPERTS - 1).astype(I32)
    n_used = (pend[-1:] // cpb).astype(I32)
    y_loc = _experts(rows, src_chunk, block_e, n_used, dummy, w)
    return _combine(x2, info, y_loc, tm, g_final, final)


def _rope_tables(pos):
    half = ROPE_DIM // 2
    inv = ROPE_BASE ** (-jnp.arange(half, dtype=F32) * 2.0 / ROPE_DIM)
    ang = pos.astype(F32)[:, None] * inv[None, :]
    c, s = jnp.cos(ang), jnp.sin(ang)
    cos32 = jnp.concatenate([c, c], axis=1)
    sin32 = jnp.concatenate([-s, s], axis=1)
    return jnp.tile(cos32, (1, MLA_H)), jnp.tile(sin32, (1, MLA_H))


def _prep_layer(l, p):
    w_in = p["w_in"][l]
    zeros80 = jnp.zeros((D_MODEL, 80), F32)
    w_in_p = jnp.concatenate([w_in[:, 0:640], w_in[:, 640:672], w_in[:, 1184:1200], zeros80,
                              w_in[:, 672:1184], w_in[:, 1200:1456]], axis=1).astype(BF16)
    assert w_in_p.shape[1] == IN_PAD
    w_uq = p["w_uq"][l].reshape(Q_LORA, MLA_H, NOPE + ROPE_DIM)
    w_uq_p = jnp.concatenate([w_uq[:, :, :NOPE].reshape(Q_LORA, MLA_H * NOPE),
                              w_uq[:, :, NOPE:].reshape(Q_LORA, MLA_H * ROPE_DIM)], axis=1).astype(BF16)
    eye_h = jnp.eye(MLA_H, dtype=F32)
    w_ukbd = jnp.einsum("hnc,hg->hngc", p["w_uk"][l].transpose(1, 2, 0), eye_h).reshape(MLA_H * NOPE, MLA_H * KV_LORA).astype(BF16)
    w_uvbd = jnp.einsum("hcv,hg->hcgv", p["w_uv"][l].transpose(1, 0, 2), eye_h).reshape(MLA_H * KV_LORA, MLA_H * V_DIM).astype(BF16)
    w_gk2 = jnp.zeros((128, 128), F32).at[ROPE_DIM:ROPE_DIM + GATE_RANK, :].set(p["w_gk2"][l]).astype(BF16)
    w_poolbd = jnp.einsum("gcd,gk->gckd", p["w_pool"][l], jnp.eye(POOL_GROUPS, dtype=F32)).reshape(POOL_W, POOL_W).astype(BF16)
    w_out = p["w_out"][l].astype(BF16)
    w_r = jnp.zeros((D_MODEL, 128), F32).at[:, 0:N_GROUPS].set(p["w_rg"][l]).at[:, EXPERT_LANE0:EXPERT_LANE0 + N_EXPERTS].set(p["w_re"][l])
    w_r_hi = w_r.astype(BF16)
    b_r = jnp.zeros((1, 128), F32).at[0, 0:N_GROUPS].set(p["b_rg"][l]).at[0, EXPERT_LANE0:EXPERT_LANE0 + N_EXPERTS].set(p["b_re"][l])
    return {
        "g_mix": p["g_mix"][l][None, :], "w_in": w_in_p, "g_qn": p["g_qn"][l][None, :], "w_uq": w_uq_p, "w_ukbd": w_ukbd,
        "g_kvn": p["g_kvn"][l][None, :], "w_gk2": w_gk2, "b_gk": p["b_gk"][l][None, :],
        "w_poolbd": w_poolbd, "pool_scale": p["pool_scale"][l][None, :],
        "g_gla": jnp.tile(p["g_gla"][l], GLA_H)[None, :],
        "w_uvbd": w_uvbd, "wo_a": w_out[0:256], "wo_b": w_out[256:768], "wo_c": w_out[768:1024],
        "g_x": p["g_x"][l][None, :], "w_xq": p["w_xq"][l].astype(BF16), "w_xo": p["w_xo"][l].astype(BF16),
        "g_mem": p["g_mem"][l][None, :], "w_xk": p["w_xk"][l].astype(BF16), "w_xv": p["w_xv"][l].astype(BF16),
        "g_ffn": p["g_ffn"][l][None, :], "w_r_hi": w_r_hi, "w_r_lo": (w_r - w_r_hi.astype(F32)).astype(BF16), "b_r": b_r,
        "w_eg": p["w_eg"][l].astype(BF16), "w_eu": p["w_eu"][l].astype(BF16), "w_ed": p["w_ed"][l].astype(BF16),
    }


GLA_CHUNK = 64


def kernel(x_prompt, x_sample, mem_prompt, cache_ckv, cache_kpe, page_table, state_pool, state_gla, cache_mem_k, cache_mem_v, g_mix, w_in, w_pool, pool_scale, g_qn, w_uq, g_kvn, w_uk, w_uv, w_gk2, b_gk, g_gla, w_out, g_x, g_mem, w_xq, w_xk, w_xv, w_xo, g_ffn, w_rg, b_rg, w_re, b_re, w_eg, w_eu, w_ed, g_final):
    params = dict(g_mix=g_mix, w_in=w_in, w_pool=w_pool, pool_scale=pool_scale, g_qn=g_qn, w_uq=w_uq, g_kvn=g_kvn, w_uk=w_uk,
                  w_uv=w_uv, w_gk2=w_gk2, b_gk=b_gk, g_gla=g_gla, w_out=w_out, g_x=g_x, g_mem=g_mem, w_xq=w_xq, w_xk=w_xk,
                  w_xv=w_xv, w_xo=w_xo, g_ffn=g_ffn, w_rg=w_rg, b_rg=b_rg, w_re=w_re, b_re=b_re, w_eg=w_eg, w_eu=w_eu, w_ed=w_ed)
    depth = w_in.shape[0]
    bsz, seq, _ = x_prompt.shape
    dbsz, dseq, _ = x_sample.shape
    past_len = page_table.shape[1] * cache_ckv.shape[2]
    gfin = g_final[None, :]
    cache_kpe_t = jnp.swapaxes(cache_kpe, 2, 3)

    cos_p, sin_p = _rope_tables(jnp.arange(seq, dtype=I32))
    cos_s, sin_s = _rope_tables(past_len + (jnp.arange(dbsz * dseq, dtype=I32) % dseq))

    xp = x_prompt.reshape(bsz * seq, D_MODEL)
    xs = x_sample.reshape(dbsz * dseq, D_MODEL)
    mem = mem_prompt.reshape(bsz * N_MEM, D_MODEL)
    gpad = 8 - dseq
    assert 0 <= gpad < 8
    assert past_len >= max(POOL_WINDOWS) - 1

    outs = {k: [] for k in ("ckv_p", "kpe_p", "pool_p", "gla_p", "mk_p", "mv_p", "ckv_s", "kpe_s", "pool_s", "gla_s")}
    for l in range(depth):
        w = _prep_layer(l, params)
        final = l == depth - 1
        mk, mv = _memkv(mem, w)
        u, q_hm, kvb, ckv, kpe, gq, gk, gv, go, glog = _inproj(xp, cos_p, sin_p, w)
        ya, pool_new = _pool_prompt(u, bsz, w)
        olat = _mla_prompt(q_hm, kvb, bsz)
        yc, s_new = _gla(gq, gk, gv, go, glog, None, bsz, w, GLA_CHUNK)
        x2, rows, info, nch, tm = _post(xp, ya, olat, yc, mk.reshape(bsz, N_MEM, X_W), mv.reshape(bsz, N_MEM, X_W), bsz, w)
        xp = _moe(x2, rows, info, nch, tm, w, gfin, final)
        outs["ckv_p"].append(ckv.reshape(bsz, seq, KV_LORA))
        outs["kpe_p"].append(kpe.reshape(bsz, seq, ROPE_DIM))
        outs["pool_p"].append(pool_new)
        outs["gla_p"].append(s_new.reshape(bsz, GLA_H, GLA_DK, GLA_DV))
        outs["mk_p"].append(mk.reshape(bsz, N_MEM, X_H, X_HD))
        outs["mv_p"].append(mv.reshape(bsz, N_MEM, X_H, X_HD))
        u, q_hm, kvb, ckv, kpe, gq, gk, gv, go, glog = _inproj(xs, cos_s, sin_s, w)
        ue_tm = jnp.concatenate([state_pool[l], u.reshape(dbsz, dseq, POOL_W)], axis=1).transpose(1, 0, 2)
        ya_tm, st_tm = _pool_sample(ue_tm, w)
        ya = ya_tm.transpose(1, 0, 2).reshape(dbsz * dseq, POOL_W)
        q_s = q_hm.reshape(MLA_H, dbsz, dseq, QK_PAD).transpose(1, 0, 2, 3).reshape(dbsz, MLA_H * dseq, QK_PAD)
        knew = jnp.pad(kvb.reshape(dbsz, dseq, QK_PAD), ((0, 0), (0, 16 - dseq), (0, 0)))
        o_s = _mla_decode(q_s, knew, cache_ckv, cache_kpe_t, page_table, l)
        olat = o_s.reshape(dbsz, MLA_H, dseq, KV_LORA).transpose(0, 2, 1, 3).reshape(dbsz * dseq, MLA_H * KV_LORA).astype(BF16)

        def pad8(a):
            return jnp.pad(a.reshape(dbsz, dseq, -1), ((0, 0), (0, gpad), (0, 0))).reshape(dbsz * 8, -1)

        yc8, s_new = _gla(pad8(gq), pad8(gk), pad8(gv), pad8(go), pad8(glog),
                          state_gla[l].reshape(dbsz, GLA_H * GLA_DK, GLA_DV), dbsz, w, 8)
        yc = yc8.reshape(dbsz, 8, GLA_W)[:, :dseq].reshape(dbsz * dseq, GLA_W)
        x2, rows, info, nch, tm = _post(xs, ya, olat, yc, cache_mem_k.reshape(depth * dbsz, N_MEM, X_H, X_HD),
                                     cache_mem_v.reshape(depth * dbsz, N_MEM, X_H, X_HD), dbsz, w, kv_seq0=l * dbsz)
        xs = _moe(x2, rows, info, nch, tm, w, gfin, final)
        outs["ckv_s"].append(ckv.reshape(dbsz, dseq, KV_LORA))
        outs["kpe_s"].append(kpe.reshape(dbsz, dseq, ROPE_DIM))
        outs["pool_s"].append(st_tm.transpose(1, 0, 2))
        outs["gla_s"].append(s_new.reshape(dbsz, GLA_H, GLA_DK, GLA_DV))

    st = lambda k: jnp.stack(outs[k])
    return (xp.reshape(bsz, seq, D_MODEL), xs.reshape(dbsz, dseq, D_MODEL),
            st("ckv_p"), st("kpe_p"), st("pool_p"), st("gla_p"), st("mk_p"), st("mv_p"),
            st("ckv_s"), st("kpe_s"), st("pool_s"), st("gla_s"))
```

```python
import functools

import numpy as np
import jax
import jax.numpy as jnp
from jax import lax
from jax.experimental import pallas as pl
from jax.experimental.pallas import tpu as pltpu

F32 = jnp.float32
BF16 = jnp.bfloat16
I32 = jnp.int32

EPS = 1e-6
D_MODEL = 1024
POOL_GROUPS, POOL_GC = 4, 64
POOL_W = POOL_GROUPS * POOL_GC
POOL_WINDOWS = (2, 4, 8, 16)
POOL_BUF = 15
MLA_H, Q_LORA, KV_LORA, NOPE, ROPE_DIM, V_DIM = 8, 256, 128, 64, 32, 64
ROPE_BASE = 10000.0
MLA_SCALE = (NOPE + ROPE_DIM) ** -0.5
Q_PRESCALE = MLA_SCALE * 1.4426950408889634
GLA_H, GLA_DK, GLA_DV, GATE_RANK, GATE_TAU = 4, 32, 64, 16, 16.0
GLA_W = GLA_H * GLA_DV
N_MEM, X_H, X_HD = 256, 4, 128
X_W = X_H * X_HD
N_GROUPS, EXP_PER_GROUP, N_EXPERTS, TOP_K, D_EXPERT = 4, 8, 32, 2, 256
QK_PAD = 256
ONES_LANE = KV_LORA + ROPE_DIM
DECODE_SLOTS = 3
MLA_HEAD_GROUPS = 4
IN_PAD = 1536
EXPERT_LANE0 = 32
MOE_BLOCK = 512
MOE_CHUNK = 16
NEG = -1e30
VMEM_LIMIT = 56 * 1024 * 1024


def _pick(n, prefs):
    for p in prefs:
        if n % p == 0:
            return p
    raise ValueError(f"no tile in {prefs} divides {n}")


def _cparams(n_axes, **flags):
    return pltpu.CompilerParams(dimension_semantics=("arbitrary",) * n_axes, vmem_limit_bytes=VMEM_LIMIT,
                                flags=flags or None)


def _rms(x, g):
    ms = jnp.mean(x * x, axis=-1, keepdims=True)
    return x * lax.rsqrt(ms + EPS) * g


def _dot(a, b):
    return jnp.dot(a, b, preferred_element_type=F32)


def _dot_nt(a, b):
    return lax.dot_general(a, b, (((1,), (1,)), ((), ())), preferred_element_type=F32)


def _dot_tn(a, b):
    return lax.dot_general(a, b, (((0,), (0,)), ((), ())), preferred_element_type=F32)


def _split_bf16(x):
    hi = x.astype(BF16)
    lo = (x - hi.astype(F32)).astype(BF16)
    return hi, lo


def _swap16(x):
    w = x.shape[-1]
    lane = lax.broadcasted_iota(I32, x.shape, x.ndim - 1)
    first = (lane & 31) < 16
    return jnp.where(first, pltpu.roll(x, w - 16, x.ndim - 1), pltpu.roll(x, 16, x.ndim - 1))


def _silu(x):
    return x / (1.0 + jnp.exp(-x))


def _inproj_kernel(x_ref, gmix_ref, win_ref, cos_ref, sin_ref, gqn_ref, wuq_ref, wuk_ref, gkvn_ref, wgk_ref, bgk_ref,
                   u_ref, q_ref, kvb_ref, ckv_ref, kpe_ref, gq_ref, gk_ref, gv_ref, go_ref, glog_ref):
    h = _rms(x_ref[...], gmix_ref[...]).astype(BF16)
    y = _dot(h, win_ref[...])
    u_ref[...] = y[:, 0:256]
    gq_ref[...] = y[:, 768:896]
    gk_ref[...] = y[:, 896:1024]
    gv_ref[...] = y[:, 1024:1280]
    go_ref[...] = y[:, 1280:1536]
    cos = cos_ref[...]
    sin = sin_ref[...]
    ckv = _rms(y[:, 512:640], gkvn_ref[...])
    grp = y[:, 640:768]
    grp_r = grp * cos[:, 0:128] + _swap16(grp) * sin[:, 0:128]
    ckv_ref[...] = ckv
    kpe_ref[...] = grp_r[:, 0:ROPE_DIM]
    lane = lax.broadcasted_iota(I32, grp_r.shape, 1)
    kvb_ref[:, 0:128] = ckv.astype(BF16)
    kvb_ref[:, 128:256] = jnp.where(lane < ROPE_DIM, grp_r, jnp.where(lane == ROPE_DIM, 1.0, 0.0)).astype(BF16)
    gl = _dot(grp.astype(BF16), wgk_ref[...]) + bgk_ref[...]
    glog_ref[...] = (jnp.minimum(gl, 0.0) - jnp.log(1.0 + jnp.exp(-jnp.abs(gl)))) * (1.0 / GATE_TAU)
    cqn = _rms(y[:, 256:512], gqn_ref[...]).astype(BF16)
    q = _dot(cqn, wuq_ref[...])
    qlat = _dot(q[:, 0:512].astype(BF16), wuk_ref[...]) * Q_PRESCALE
    qr = q[:, 512:768]
    qr = (qr * cos + _swap16(qr) * sin) * Q_PRESCALE
    for hh in range(MLA_H):
        q_ref[hh, :, 0:128] = qlat[:, 128 * hh:128 * hh + 128].astype(BF16)
        col = qr[:, 128 * (hh // 4):128 * (hh // 4) + 128]
        sh = (128 - 32 * (hh % 4)) % 128
        if sh:
            col = pltpu.roll(col, sh, 1)
        q_ref[hh, :, 128:256] = jnp.where(lane < ROPE_DIM, col, 0.0).astype(BF16)


def _inproj(x, cos_t, sin_t, w):
    n = x.shape[0]
    tm = _pick(n, (512, 256, 128, 64, 32, 16))
    tab_tiles = cos_t.shape[0] // tm
    assert cos_t.shape[0] % tm == 0
    row = lambda i: (i, 0)
    tab = lambda i: (i % tab_tiles, 0)
    full = lambda i: (0, 0)
    wspec = lambda a: pl.BlockSpec(a.shape, full)
    outs = [
        jax.ShapeDtypeStruct((n, 256), F32),
        jax.ShapeDtypeStruct((MLA_H, n, QK_PAD), BF16),
        jax.ShapeDtypeStruct((n, QK_PAD), BF16),
        jax.ShapeDtypeStruct((n, KV_LORA), F32),
        jax.ShapeDtypeStruct((n, ROPE_DIM), F32),
        jax.ShapeDtypeStruct((n, 128), F32),
        jax.ShapeDtypeStruct((n, 128), F32),
        jax.ShapeDtypeStruct((n, 256), F32),
        jax.ShapeDtypeStruct((n, 256), F32),
        jax.ShapeDtypeStruct((n, 128), F32),
    ]
    out_specs = [
        pl.BlockSpec((tm, 256), row),
        pl.BlockSpec((MLA_H, tm, QK_PAD), lambda i: (0, i, 0)),
        pl.BlockSpec((tm, QK_PAD), row),
        pl.BlockSpec((tm, KV_LORA), row),
        pl.BlockSpec((tm, ROPE_DIM), row),
        pl.BlockSpec((tm, 128), row),
        pl.BlockSpec((tm, 128), row),
        pl.BlockSpec((tm, 256), row),
        pl.BlockSpec((tm, 256), row),
        pl.BlockSpec((tm, 128), row),
    ]
    ins = [x, w["g_mix"], w["w_in"], cos_t, sin_t, w["g_qn"], w["w_uq"], w["w_ukbd"], w["g_kvn"], w["w_gk2"], w["b_gk"]]
    in_specs = [pl.BlockSpec((tm, D_MODEL), row), wspec(w["g_mix"]), wspec(w["w_in"]),
                pl.BlockSpec((tm, 256), tab), pl.BlockSpec((tm, 256), tab),
                wspec(w["g_qn"]), wspec(w["w_uq"]), wspec(w["w_ukbd"]), wspec(w["g_kvn"]), wspec(w["w_gk2"]), wspec(w["b_gk"])]
    return pl.pallas_call(_inproj_kernel, out_shape=outs, grid=(n // tm,), in_specs=in_specs, out_specs=out_specs,
                          compiler_params=_cparams(1), name="inproj")(*ins)


def _pool_mix(sums, u, cnts, wbd, scale):
    cols = []
    for c in range(2):
        wa, wb = POOL_WINDOWS[2 * c], POOL_WINDOWS[2 * c + 1]
        sa, sb = sums[(c, wa)], sums[(c, wb)]
        lane = lax.broadcasted_iota(I32, sa.shape, 1)
        pooled = jnp.where(lane < POOL_GC, sa / cnts[wa], sb / cnts[wb]) - u[:, 128 * c:128 * c + 128]
        cols.append(pooled)
    pooled = jnp.concatenate(cols, axis=1).astype(BF16)
    return (_dot(pooled, wbd) * scale).astype(BF16)


def _pool_kernel(u_ref, wbd_ref, scale_ref, ya_ref, st_ref, ue_ref, *, tp, nt):
    j = pl.program_id(1)

    @pl.when(j == 0)
    def _():
        ue_ref[0:16, :] = jnp.zeros((16, POOL_W), F32)

    @pl.when(j > 0)
    def _():
        ue_ref[0:16, :] = ue_ref[tp:tp + 16, :]

    u = u_ref[...]
    ue_ref[16:16 + tp, :] = u
    t = j * tp + lax.broadcasted_iota(I32, (tp, 1), 0)
    cnts = {w: jnp.minimum(t + 1, w).astype(F32) for w in POOL_WINDOWS}
    sums = {}
    for c in range(2):
        wa, wb = POOL_WINDOWS[2 * c], POOL_WINDOWS[2 * c + 1]
        acc = None
        for k in range(wb):
            sl = ue_ref[16 - k:16 - k + tp, 128 * c:128 * c + 128]
            acc = sl if acc is None else acc + sl
            if k + 1 == wa:
                sums[(c, wa)] = acc
        sums[(c, wb)] = acc
    ya_ref[...] = _pool_mix(sums, u, cnts, wbd_ref[...], scale_ref[...])

    @pl.when(j == nt - 1)
    def _():
        st_ref[0] = ue_ref[tp + 1:tp + 16, :]


def _pool_prompt(u, bsz, w):
    n = u.shape[0]
    t = n // bsz
    tp = _pick(t, (512, 256, 128, 64, 32, 16))
    nt = t // tp
    kern = functools.partial(_pool_kernel, tp=tp, nt=nt)
    return pl.pallas_call(
        kern,
        out_shape=[jax.ShapeDtypeStruct((n, POOL_W), BF16), jax.ShapeDtypeStruct((bsz, POOL_BUF, POOL_W), F32)],
        grid=(bsz, nt),
        in_specs=[pl.BlockSpec((tp, POOL_W), lambda b, j: (b * nt + j, 0)),
                  pl.BlockSpec((POOL_W, POOL_W), lambda b, j: (0, 0)),
                  pl.BlockSpec((1, POOL_W), lambda b, j: (0, 0))],
        out_specs=[pl.BlockSpec((tp, POOL_W), lambda b, j: (b * nt + j, 0)),
                   pl.BlockSpec((1, POOL_BUF, POOL_W), lambda b, j: (b, 0, 0))],
        scratch_shapes=[pltpu.VMEM((tp + 16, POOL_W), F32)],
        compiler_params=_cparams(2), name="pool_prompt")(u, w["w_poolbd"], w["pool_scale"])


def _pool_step_kernel(ue_ref, wbd_ref, scale_ref, ya_ref, st_ref, *, tt):
    cnts = {w: jnp.float32(w) for w in POOL_WINDOWS}
    for t in range(tt):
        sums = {}
        for c in range(2):
            wa, wb = POOL_WINDOWS[2 * c], POOL_WINDOWS[2 * c + 1]
            acc = None
            for k in range(wb):
                sl = ue_ref[POOL_BUF + t - k, :, 128 * c:128 * c + 128]
                acc = sl if acc is None else acc + sl
                if k + 1 == wa:
                    sums[(c, wa)] = acc
            sums[(c, wb)] = acc
        ya_ref[t] = _pool_mix(sums, ue_ref[POOL_BUF + t], cnts, wbd_ref[...], scale_ref[...])
    for r in range(POOL_BUF):
        st_ref[r] = ue_ref[tt + r]


def _pool_sample(ue_tm, w):
    rows, bsz, _ = ue_tm.shape
    tt = rows - POOL_BUF
    kern = functools.partial(_pool_step_kernel, tt=tt)
    full3 = lambda i: (0, 0, 0)
    return pl.pallas_call(
        kern,
        out_shape=[jax.ShapeDtypeStruct((tt, bsz, POOL_W), BF16), jax.ShapeDtypeStruct((POOL_BUF, bsz, POOL_W), F32)],
        grid=(1,),
        in_specs=[pl.BlockSpec(ue_tm.shape, full3), pl.BlockSpec((POOL_W, POOL_W), lambda i: (0, 0)),
                  pl.BlockSpec((1, POOL_W), lambda i: (0, 0))],
        out_specs=[pl.BlockSpec((tt, bsz, POOL_W), full3), pl.BlockSpec((POOL_BUF, bsz, POOL_W), full3)],
        compiler_params=_cparams(1), name="pool_sample")(ue_tm, w["w_poolbd"], w["pool_scale"])


def _mla_prompt_kernel(qi_ref, kj_ref, q_ref, k_ref, o_ref, m_ref, acc_ref, *, tq, tk):
    p = pl.program_id(1)
    qi = qi_ref[p]
    kj = kj_ref[p]
    rows = MLA_H * tq

    @pl.when(kj == 0)
    def _():
        m_ref[...] = jnp.full((rows, 128), NEG, F32)
        acc_ref[...] = jnp.zeros((rows, QK_PAD), F32)

    def step(masked, nk):
        k = k_ref[0:nk, :]
        hg = MLA_H // MLA_HEAD_GROUPS
        gr = hg * tq
        for g in range(MLA_HEAD_GROUPS):
            rs = slice(g * gr, (g + 1) * gr)
            q = q_ref[g * hg:(g + 1) * hg].reshape(gr, QK_PAD)
            s = _dot_nt(q, k)
            if masked:
                r = lax.broadcasted_iota(I32, (gr, nk), 0)
                c = lax.broadcasted_iota(I32, (gr, nk), 1)
                qpos = qi * tq + (r & (tq - 1))
                s = jnp.where(kj * tk + c <= qpos, s, NEG)
            m_old = m_ref[rs, :]
            m_new = jnp.maximum(m_old, jnp.max(s, axis=-1, keepdims=True))
            alpha = jnp.exp2(m_old - m_new)
            pr = jnp.exp2(s - jnp.concatenate([m_new] * (nk // 128), axis=1))
            acc_ref[rs, :] = jnp.concatenate([alpha, alpha], axis=1) * acc_ref[rs, :] + _dot(pr.astype(BF16), k)
            m_ref[rs, :] = m_new

    visible = (qi + 1) * tq - kj * tk
    half = tk // 2 if (tk % 256 == 0 and tq <= tk // 2) else tk
    crosses = visible < tk + tq

    if half < tk:
        @pl.when(jnp.logical_and(crosses, visible <= half))
        def _():
            step(True, half)

    @pl.when(jnp.logical_and(crosses, visible > half) if half < tk else crosses)
    def _():
        step(True, tk)

    @pl.when(jnp.logical_not(crosses))
    def _():
        step(False, tk)

    @pl.when(kj == ((qi + 1) * tq - 1) // tk)
    def _():
        acc = acc_ref[...]
        o = acc[:, 0:KV_LORA] / acc[:, ONES_LANE:ONES_LANE + 1]
        for h in range(MLA_H):
            o_ref[:, KV_LORA * h:KV_LORA * (h + 1)] = o[h * tq:(h + 1) * tq].astype(BF16)


def _mla_prompt(q_hm, kvb, bsz):
    n = kvb.shape[0]
    t = n // bsz
    tq = _pick(t, (512, 256, 128, 64, 32, 16))
    tk = _pick(t, (512, 256, 128, 64, 32, 16))
    assert tq & (tq - 1) == 0
    nq, nk = t // tq, t // tk
    pairs = [(i, j) for i in range(nq) for j in range(((i + 1) * tq - 1) // tk + 1)]
    qi = jnp.asarray(np.array([p[0] for p in pairs], np.int32))
    kj = jnp.asarray(np.array([p[1] for p in pairs], np.int32))
    rows = MLA_H * tq
    kern = functools.partial(_mla_prompt_kernel, tq=tq, tk=tk)
    grid_spec = pltpu.PrefetchScalarGridSpec(
        num_scalar_prefetch=2, grid=(bsz, len(pairs)),
        in_specs=[pl.BlockSpec((MLA_H, tq, QK_PAD), lambda b, p, qi, kj: (0, b * nq + qi[p], 0)),
                  pl.BlockSpec((tk, QK_PAD), lambda b, p, qi, kj: (b * nk + kj[p], 0))],
        out_specs=pl.BlockSpec((tq, MLA_H * KV_LORA), lambda b, p, qi, kj: (b * nq + qi[p], 0)),
        scratch_shapes=[pltpu.VMEM((rows, 128), F32), pltpu.VMEM((rows, QK_PAD), F32)])
    return pl.pallas_call(kern, out_shape=jax.ShapeDtypeStruct((n, MLA_H * KV_LORA), BF16), grid_spec=grid_spec,
                          compiler_params=_cparams(2), name="mla_prompt")(qi, kj, q_hm, kvb)


def _mla_decode_kernel(pt_ref, q_ref, knew_ref, cckv_ref, ckpe_ref, o_ref,
                       cbuf, pbuf, csem, psem, m_ref, l_ref, acc_ref, *, layer, ng, gp, page, tt, total):
    s_idx = pl.program_id(0)
    grp = s_idx % ng
    slot = s_idx % DECODE_SLOTS

    def copies(step, slot_):
        out = []
        for g in range(gp):
            pg = pt_ref[step * gp + g]
            out.append(pltpu.make_async_copy(cckv_ref.at[layer, pg], cbuf.at[slot_, pl.ds(g * page, page)], csem.at[slot_]))
            out.append(pltpu.make_async_copy(ckpe_ref.at[layer, pg], pbuf.at[slot_, :, pl.ds(g * page, page)], psem.at[slot_]))
        return out

    @pl.when(s_idx == 0)
    def _():
        for ahead in range(min(DECODE_SLOTS - 1, total)):
            for c in copies(ahead, ahead):
                c.start()

    @pl.when(s_idx + (DECODE_SLOTS - 1) < total)
    def _():
        for c in copies(s_idx + (DECODE_SLOTS - 1), (s_idx + (DECODE_SLOTS - 1)) % DECODE_SLOTS):
            c.start()

    @pl.when(grp == 0)
    def _():
        m_ref[...] = jnp.full(m_ref.shape, NEG, F32)
        l_ref[...] = jnp.zeros(l_ref.shape, F32)
        acc_ref[...] = jnp.zeros(acc_ref.shape, F32)

    for c in copies(s_idx, slot):
        c.wait()

    q = q_ref[0]

    def update(s, v):
        m_old = m_ref[...]
        m_new = jnp.maximum(m_old, jnp.max(s, axis=-1, keepdims=True))
        alpha = jnp.exp2(m_old - m_new)
        pr = jnp.exp2(s - m_new[:, 0:1])
        l_ref[...] = alpha * l_ref[...] + jnp.sum(pr, axis=-1, keepdims=True)
        acc_ref[...] = alpha * acc_ref[...] + _dot(pr.astype(BF16), v)
        m_ref[...] = m_new

    kc = cbuf[slot].astype(BF16)
    kr_t = pbuf[slot].astype(BF16)
    s = _dot_nt(q[:, 0:KV_LORA], kc) + _dot(q[:, KV_LORA:KV_LORA + ROPE_DIM], kr_t)
    update(s, kc)

    @pl.when(grp == ng - 1)
    def _():
        kn = knew_ref[0]
        sn = _dot_nt(q, kn)
        r = lax.broadcasted_iota(I32, sn.shape, 0)
        c = lax.broadcasted_iota(I32, sn.shape, 1)
        sn = jnp.where(c <= (r % tt), sn, NEG)
        update(sn, kn[:, 0:KV_LORA])
        o_ref[0] = acc_ref[...] / l_ref[...]


def _mla_decode(q_s, knew, cache_ckv, cache_kpe_t, page_table, layer):
    dbsz, rows, _ = q_s.shape
    tt = rows // MLA_H
    n_pages = page_table.shape[1]
    page = cache_ckv.shape[2]
    gp = _pick(n_pages, (64, 32, 16, 8, 4, 2, 1))
    ng = n_pages // gp
    total = dbsz * ng
    kern = functools.partial(_mla_decode_kernel, layer=layer, ng=ng, gp=gp, page=page, tt=tt, total=total)
    grid_spec = pltpu.PrefetchScalarGridSpec(
        num_scalar_prefetch=1, grid=(total,),
        in_specs=[pl.BlockSpec((1, rows, QK_PAD), lambda s, pt: (s // ng, 0, 0)),
                  pl.BlockSpec((1, 16, QK_PAD), lambda s, pt: (s // ng, 0, 0)),
                  pl.BlockSpec(memory_space=pl.ANY), pl.BlockSpec(memory_space=pl.ANY)],
        out_specs=pl.BlockSpec((1, rows, KV_LORA), lambda s, pt: (s // ng, 0, 0)),
        scratch_shapes=[pltpu.VMEM((DECODE_SLOTS, gp * page, KV_LORA), F32), pltpu.VMEM((DECODE_SLOTS, ROPE_DIM, gp * page), F32),
                        pltpu.SemaphoreType.DMA((DECODE_SLOTS,)), pltpu.SemaphoreType.DMA((DECODE_SLOTS,)),
                        pltpu.VMEM((rows, 128), F32), pltpu.VMEM((rows, 128), F32), pltpu.VMEM((rows, KV_LORA), F32)])
    return pl.pallas_call(kern, out_shape=jax.ShapeDtypeStruct((dbsz, rows, KV_LORA), F32), grid_spec=grid_spec,
                          compiler_params=_cparams(1), name="mla_decode")(
        page_table.reshape(-1), q_s, knew, cache_ckv, cache_kpe_t)


def _gla_constants(c):
    t = np.arange(c)
    blocks = [(t[None, :] <= t[:, None]), (t[None, :] > t[:, None])]
    masks = []
    m = c // 2
    while m >= 1:
        bd = (t // (2 * m)) * 2 * m + m
        upper = t >= bd
        a = upper[:, None] & (t[None, :] >= bd[:, None]) & (t[None, :] <= t[:, None])
        b = (~upper)[:, None] & (t[None, :] > t[:, None]) & (t[None, :] <= bd[:, None] - 1)
        blocks += [a, b]
        same = (t[:, None] // (2 * m)) == (t[None, :] // (2 * m))
        masks.append(same & upper[:, None] & (~upper)[None, :])
        m //= 2
    masks.append(t[:, None] == t[None, :])
    sel = np.concatenate(blocks, axis=0).astype(np.float32)
    masks = np.concatenate([np.tile(mm, (1, GLA_H)) for mm in masks], axis=0).astype(np.float32)
    r = np.arange(GLA_H * c)
    kmask = ((r[:, None] // c) == (np.arange(128)[None, :] // GLA_DK)).astype(np.float32)
    vmask = ((r[:, None] // c) == (np.arange(GLA_W)[None, :] // GLA_DV)).astype(np.float32)
    smask = ((np.arange(128)[:, None] // GLA_DK) == (np.arange(GLA_W)[None, :] // GLA_DV)).astype(np.float32)
    last = np.zeros((c, GLA_W), np.float32)
    last[c - 1, :] = 1.0
    gsum = ((np.arange(GLA_W)[:, None] // GLA_DV) == (np.arange(GLA_W)[None, :] // GLA_DV)).astype(np.float32)
    return sel, masks, kmask, vmask, smask, last, gsum


def _gla_kernel(*refs, c, nsub, nsteps, has_init):
    if has_init:
        (q_ref, k_ref, v_ref, go_ref, g_ref, s0_ref, sel_ref, masks_ref, kmask_ref, vmask_ref, smask_ref, last_ref,
         gsum_ref, ggla_ref, y_ref, sout_ref, s_ref) = refs
    else:
        (q_ref, k_ref, v_ref, go_ref, g_ref, sel_ref, masks_ref, kmask_ref, vmask_ref, smask_ref, last_ref,
         gsum_ref, ggla_ref, y_ref, sout_ref, s_ref) = refs
        s0_ref = None
    j = pl.program_id(1)
    nlev = int(np.log2(c))
    smask = smask_ref[...]

    @pl.when(j == 0)
    def _():
        if has_init:
            s0 = s0_ref[0]
            s_ref[...] = jnp.concatenate([s0] * GLA_H, axis=1) * smask
        else:
            s_ref[...] = jnp.zeros(s_ref.shape, F32)

    def chunk(i, carry):
        sl = pl.ds(pl.multiple_of(i * c, c), c)
        q = q_ref[sl, :] * (GLA_DK ** -0.5)
        k = k_ref[sl, :]
        v = v_ref[sl, :].astype(BF16)
        g_hi, g_lo = _split_bf16(g_ref[sl, :])
        sel = sel_ref[...]
        ee = _dot(sel, jnp.concatenate([g_hi, g_lo], axis=1))
        e_all = jnp.exp(ee[:, 0:128] + ee[:, 128:256])
        eb = e_all[0:c]
        s_old = s_ref[...]
        o = _dot((q * eb).astype(BF16), s_old.astype(BF16))
        kmask = kmask_ref[...]
        att = None
        for lev in range(nlev + 1):
            if lev < nlev:
                ql = (q * e_all[(2 + 2 * lev) * c:(3 + 2 * lev) * c]).astype(BF16)
                kl = k * e_all[(3 + 2 * lev) * c:(4 + 2 * lev) * c]
            else:
                ql, kl = q.astype(BF16), k
            kb = (jnp.concatenate([kl] * GLA_H, axis=0) * kmask).astype(BF16)
            part = _dot_nt(ql, kb) * masks_ref[lev * c:(lev + 1) * c, :]
            att = part if att is None else att + part
        vb = (jnp.concatenate([v] * GLA_H, axis=0) * vmask_ref[...].astype(BF16))
        o = o + _dot(att.astype(BF16), vb)
        kk = (k * e_all[c:2 * c]).astype(BF16)
        eb_hi, eb_lo = _split_bf16(eb)
        last = last_ref[...].astype(BF16)
        decay = _dot_tn(eb_hi, last) + _dot_tn(eb_lo, last)
        s_ref[...] = (decay * s_old + _dot_tn(kk, v)) * smask
        o2_hi, o2_lo = _split_bf16(o * o)
        gsum = gsum_ref[...]
        ms = (_dot(o2_hi, gsum) + _dot(o2_lo, gsum)) * (1.0 / GLA_DV)
        yc = o * lax.rsqrt(ms + EPS) * ggla_ref[...] * _silu(go_ref[sl, :])
        y_ref[sl, :] = yc.astype(BF16)
        return carry

    lax.fori_loop(0, nsub, chunk, 0, unroll=True)

    @pl.when(j == nsteps - 1)
    def _():
        s = s_ref[...]
        acc = s[:, 0:GLA_DV]
        for h in range(1, GLA_H):
            acc = acc + s[:, GLA_DV * h:GLA_DV * (h + 1)]
        sout_ref[0] = acc


def _gla(gq, gk, gv, go, glog, s0, bsz, w, chunk):
    n = gq.shape[0]
    t = n // bsz
    c = min(chunk, t)
    assert t % c == 0 and c & (c - 1) == 0 and c >= 8
    nsub = _pick(t // c, (8, 4, 2, 1))
    tc = c * nsub
    nsteps = t // tc
    consts = [jnp.asarray(a) for a in _gla_constants(c)]
    consts[0] = consts[0].astype(BF16)
    consts[6] = consts[6].astype(BF16)
    has_init = s0 is not None
    kern = functools.partial(_gla_kernel, c=c, nsub=nsub, nsteps=nsteps, has_init=has_init)
    row = lambda b, j: (b * nsteps + j, 0)
    full = lambda b, j: (0, 0)
    ins = [gq, gk, gv, go, glog]
    in_specs = [pl.BlockSpec((tc, 128), row), pl.BlockSpec((tc, 128), row), pl.BlockSpec((tc, 256), row),
                pl.BlockSpec((tc, 256), row), pl.BlockSpec((tc, 128), row)]
    if has_init:
        ins.append(s0)
        in_specs.append(pl.BlockSpec((1, 128, GLA_DV), lambda b, j: (b, 0, 0)))
    ins += consts + [w["g_gla"]]
    in_specs += [pl.BlockSpec(a.shape, full) for a in consts] + [pl.BlockSpec(w["g_gla"].shape, full)]
    return pl.pallas_call(
        kern,
        out_shape=[jax.ShapeDtypeStruct((n, GLA_W), BF16), jax.ShapeDtypeStruct((bsz, 128, GLA_DV), F32)],
        grid=(bsz, nsteps), in_specs=in_specs,
        out_specs=[pl.BlockSpec((tc, GLA_W), row), pl.BlockSpec((1, 128, GLA_DV), lambda b, j: (b, 0, 0))],
        scratch_shapes=[pltpu.VMEM((128, GLA_W), F32)],
        compiler_params=_cparams(2), name="gla")(*ins)


def _memkv_kernel(m_ref, g_ref, wk_ref, wv_ref, mk_ref, mv_ref):
    h = _rms(m_ref[...], g_ref[...]).astype(BF16)
    mk_ref[...] = _dot(h, wk_ref[...])
    mv_ref[...] = _dot(h, wv_ref[...])


def _memkv(mem, w):
    n = mem.shape[0]
    tm = _pick(n, (256, 128, 64, 32, 16, 8))
    row = lambda i: (i, 0)
    full = lambda i: (0, 0)
    return pl.pallas_call(
        _memkv_kernel,
        out_shape=[jax.ShapeDtypeStruct((n, X_W), F32)] * 2, grid=(n // tm,),
        in_specs=[pl.BlockSpec((tm, D_MODEL), row), pl.BlockSpec((1, D_MODEL), full),
                  pl.BlockSpec((D_MODEL, X_W), full), pl.BlockSpec((D_MODEL, X_W), full)],
        out_specs=[pl.BlockSpec((tm, X_W), row)] * 2,
        compiler_params=_cparams(1), name="memkv")(mem, w["g_mem"], w["w_xk"], w["w_xv"])


def _post_pre(x_ref, ya_ref, ol_ref, yc_ref, wuv_ref, woa_ref, wob_ref, woc_ref, gx_ref, wxq_ref):
    yb = _dot(ol_ref[...], wuv_ref[...]).astype(BF16)
    x1 = x_ref[...] + _dot(ya_ref[...], woa_ref[...]) + _dot(yb, wob_ref[...]) + _dot(yc_ref[...], woc_ref[...])
    q = _dot(_rms(x1, gx_ref[...]).astype(BF16), wxq_ref[...])
    return x1, q


def _post_attn(q, mk_ref, mv_ref, nb, rpb):
    def head_kv(ref, h):
        if len(ref.shape) == 4:
            a = ref[:, :, h, :]
        else:
            a = ref[:, :, X_HD * h:X_HD * (h + 1)]
        return a.reshape(nb * N_MEM, X_HD).astype(BF16)

    outs = []
    for h in range(X_H):
        sl = slice(X_HD * h, X_HD * (h + 1))
        s = _dot_nt(q[:, sl].astype(BF16), head_kv(mk_ref, h)) * (X_HD ** -0.5)
        if nb > 1:
            r = lax.broadcasted_iota(I32, s.shape, 0)
            c = lax.broadcasted_iota(I32, s.shape, 1)
            s = jnp.where(r // rpb == c // N_MEM, s, NEG)
        e = jnp.exp(s - jnp.max(s, axis=-1, keepdims=True))
        p = e / jnp.sum(e, axis=-1, keepdims=True)
        outs.append(_dot(p.astype(BF16), head_kv(mv_ref, h)))
    return jnp.concatenate(outs, axis=1)


def _post_tail(x1, o, wxo_ref, gffn_ref, wrh_ref, wrl_ref, br_ref, ltri_ref, ustr_ref, x2_ref, rows_ref, info_ref, nch_ref):
    tm = x1.shape[0]
    lr = rows_ref.shape[0]
    x2 = x1 + _dot(o.astype(BF16), wxo_ref[...])
    x2_ref[...] = x2
    h3 = _rms(x2, gffn_ref[...])
    h_hi, h_lo = _split_bf16(h3)
    logit = _dot(h_hi, wrh_ref[...]) + _dot(h_lo, wrh_ref[...]) + _dot(h_hi, wrl_ref[...]) + br_ref[...]
    lane = lax.broadcasted_iota(I32, logit.shape, 1)
    gl = jnp.where(lane < N_GROUPS, logit, NEG)
    gmax = jnp.max(gl, axis=-1, keepdims=True)
    gsel = jnp.min(jnp.where(gl == gmax, lane, 1 << 20), axis=-1, keepdims=True)
    gw = 1.0 / jnp.sum(jnp.exp(gl - gmax), axis=-1, keepdims=True)
    emask = jnp.logical_and(lane >= EXPERT_LANE0, (lane - EXPERT_LANE0) // EXP_PER_GROUP == gsel)
    el = jnp.where(emask, logit, NEG)
    pe = jnp.where(emask, jnp.exp(el - jnp.max(el, axis=-1, keepdims=True)), 0.0)
    prob = pe / jnp.sum(pe, axis=-1, keepdims=True)
    prob = jnp.where(emask, prob, -1.0)
    p1 = jnp.max(prob, axis=-1, keepdims=True)
    i1 = jnp.min(jnp.where(prob == p1, lane, 1 << 20), axis=-1, keepdims=True)
    prob2 = jnp.where(lane == i1, -1.0, prob)
    p2 = jnp.max(prob2, axis=-1, keepdims=True)
    i2 = jnp.min(jnp.where(prob2 == p2, lane, 1 << 20), axis=-1, keepdims=True)
    gate1 = gw * p1 / (p1 + p2)
    gate2 = gw * p2 / (p1 + p2)
    oh1 = (lane == i1).astype(F32)
    oh2 = (lane == i2).astype(F32)
    both = oh1 + oh2
    before = _dot(ltri_ref[...], both.astype(BF16))
    cnt = jnp.sum(both, axis=0, keepdims=True)
    nch = jnp.floor((cnt + (MOE_CHUNK - 1.0)) * (1.0 / MOE_CHUNK))
    nch8 = jnp.broadcast_to(nch, (8, 128))
    off = _dot(nch8.astype(BF16), ustr_ref[...])[0:1] * float(MOE_CHUNK)
    pos = off + before
    v1 = oh1 * pos
    v2 = oh2 * pos
    loc1 = jnp.sum(v1, axis=-1, keepdims=True)
    loc2 = jnp.sum(v2, axis=-1, keepdims=True)
    ones8 = jnp.ones((8, 128), BF16)

    def as_row(v):
        hi = jnp.floor(v * (1.0 / 32.0))
        lo = v - 32.0 * hi
        return (32.0 * _dot_nt(ones8, hi.astype(BF16)) + _dot_nt(ones8, lo.astype(BF16)))[0:1]

    slot_row = lax.broadcasted_iota(I32, (lr, tm), 0).astype(F32)
    perm = jnp.logical_or(slot_row == as_row(v1), slot_row == as_row(v2))
    perm = jnp.where(perm, 1.0, 0.0).astype(BF16)
    rows_ref[...] = _dot(perm, h3.astype(BF16)).astype(BF16)
    nch_ref[...] = nch8
    info = jnp.where(lane == 2, gate1, 0.0)
    info = jnp.where(lane == 3, gate2, info)
    info = jnp.where(lane == 4, loc1, info)
    info = jnp.where(lane == 5, loc2, info)
    info_ref[...] = info


def _post_kernel(x_ref, ya_ref, ol_ref, yc_ref, mk_ref, mv_ref, wuv_ref, woa_ref, wob_ref, woc_ref, gx_ref, wxq_ref, wxo_ref,
                 gffn_ref, wrh_ref, wrl_ref, br_ref, ltri_ref, ustr_ref, x2_ref, rows_ref, info_ref, nch_ref):
    x1, q = _post_pre(x_ref, ya_ref, ol_ref, yc_ref, wuv_ref, woa_ref, wob_ref, woc_ref, gx_ref, wxq_ref)
    o = _post_attn(q, mk_ref, mv_ref, 1, x_ref.shape[0])
    _post_tail(x1, o, wxo_ref, gffn_ref, wrh_ref, wrl_ref, br_ref, ltri_ref, ustr_ref, x2_ref, rows_ref, info_ref, nch_ref)


def _post_short_kernel(x_ref, ya_ref, ol_ref, yc_ref, mk_ref, mv_ref, wuv_ref, woa_ref, wob_ref, woc_ref, gx_ref, wxq_ref,
                       wxo_ref, gffn_ref, wrh_ref, wrl_ref, br_ref, ltri_ref, ustr_ref, x2_ref, rows_ref, info_ref, nch_ref,
                       x1_s, q_s, o_s, *, nb, rpb):
    g = pl.program_id(0)
    tg = nb * rpb

    @pl.when(g == 0)
    def _():
        x1, q = _post_pre(x_ref, ya_ref, ol_ref, yc_ref, wuv_ref, woa_ref, wob_ref, woc_ref, gx_ref, wxq_ref)
        x1_s[...] = x1
        q_s[...] = q

    r0 = pl.multiple_of(g * tg, tg)
    o_s[pl.ds(r0, tg), :] = _post_attn(q_s[pl.ds(r0, tg), :], mk_ref, mv_ref, nb, rpb)

    @pl.when(g == pl.num_programs(0) - 1)
    def _():
        _post_tail(x1_s[...], o_s[...], wxo_ref, gffn_ref, wrh_ref, wrl_ref, br_ref, ltri_ref, ustr_ref,
                   x2_ref, rows_ref, info_ref, nch_ref)


def _local_rows(tm):
    need = TOP_K * tm + N_EXPERTS * (MOE_CHUNK - 1) + MOE_CHUNK
    return -(-need // 128) * 128


def _post(x, ya, olat, yc, mk, mv, bsz, w, kv_seq0=0):
    n = x.shape[0]
    t = n // bsz
    long_seq = t >= 16
    scratch = []
    if long_seq:
        tm = _pick(t, (512, 256, 128, 64, 32, 16))
        nb = 1
        grid = (bsz, t // tm)
        row = lambda b, j: (b * (t // tm) + j, 0)
        kvm = lambda b, j: (kv_seq0 + b, 0, 0)
        kern = _post_kernel
    else:
        nb = _pick(bsz, (8, 4, 2, 1))
        tm = n
        assert (nb * t) % 16 == 0 and kv_seq0 % nb == 0
        grid = (bsz // nb, 1)
        row = lambda b, j: (0, 0)
        kvm = lambda b, j: (kv_seq0 // nb + b, 0, 0)
        kern = functools.partial(_post_short_kernel, nb=nb, rpb=t)
        scratch = [pltpu.VMEM((n, D_MODEL), F32), pltpu.VMEM((n, X_W), F32), pltpu.VMEM((n, X_W), F32)]
    ltri = jnp.asarray(np.tril(np.ones((tm, tm), np.float32), -1)).astype(BF16)
    ustr = jnp.asarray(np.triu(np.ones((128, 128), np.float32), 1)).astype(BF16)
    lr = _local_rows(tm)
    ntile = n // tm
    tile = lambda b, j: (row(b, j)[0], 0)
    full = lambda b, j: (0, 0)
    wnames = ["w_uvbd", "wo_a", "wo_b", "wo_c", "g_x", "w_xq", "w_xo", "g_ffn", "w_r_hi", "w_r_lo", "b_r"]
    ins = [x, ya, olat, yc, mk, mv] + [w[k] for k in wnames] + [ltri, ustr]
    in_specs = [pl.BlockSpec((tm, D_MODEL), row), pl.BlockSpec((tm, POOL_W), row), pl.BlockSpec((tm, MLA_H * KV_LORA), row),
                pl.BlockSpec((tm, GLA_W), row)]
    if mk.ndim == 4:
        kvm4 = lambda b, j: kvm(b, j) + (0,)
        in_specs += [pl.BlockSpec((nb, N_MEM, X_H, X_HD), kvm4)] * 2
    else:
        in_specs += [pl.BlockSpec((nb, N_MEM, X_W), kvm)] * 2
    in_specs += [pl.BlockSpec(w[k].shape, full) for k in wnames] + [pl.BlockSpec((tm, tm), full), pl.BlockSpec((128, 128), full)]
    x2, rows, info, nch = pl.pallas_call(
        kern,
        out_shape=[jax.ShapeDtypeStruct((n, D_MODEL), F32), jax.ShapeDtypeStruct((ntile * lr, D_MODEL), BF16),
                   jax.ShapeDtypeStruct((n, 128), F32), jax.ShapeDtypeStruct((ntile * 8, 128), F32)],
        grid=grid, in_specs=in_specs,
        out_specs=[pl.BlockSpec((tm, D_MODEL), row), pl.BlockSpec((lr, D_MODEL), tile), pl.BlockSpec((tm, 128), row),
                   pl.BlockSpec((8, 128), tile)],
        scratch_shapes=scratch, compiler_params=_cparams(2), name="post")(*ins)
    return x2, rows, info, nch, tm


def _expert_kernel(sc_ref, be_ref, nu_ref, rows_hbm, wg_ref, wu_ref, wd_ref, yinit_hbm, y_hbm, xbuf, ybuf, gsem, ssem, *, dummy):
    del yinit_hbm
    j = pl.program_id(0)
    nu = nu_ref[0]
    slot = j % 2
    cpb = MOE_BLOCK // MOE_CHUNK

    def gather(blk, s):
        out = []
        for c in range(cpb):
            src = pl.multiple_of(sc_ref[blk * cpb + c] * MOE_CHUNK, MOE_CHUNK)
            out.append(pltpu.make_async_copy(rows_hbm.at[pl.ds(src, MOE_CHUNK)],
                                             xbuf.at[s, pl.ds(c * MOE_CHUNK, MOE_CHUNK)], gsem.at[s]))
        return out

    def writeback(blk, s, act):
        for c in range(cpb):
            chunk = sc_ref[blk * cpb + c]

            @pl.when(chunk != dummy)
            def _():
                dst = pl.multiple_of(chunk * MOE_CHUNK, MOE_CHUNK)
                act(pltpu.make_async_copy(ybuf.at[s, pl.ds(c * MOE_CHUNK, MOE_CHUNK)],
                                          y_hbm.at[pl.ds(dst, MOE_CHUNK)], ssem.at[s]))

    @pl.when(j < nu)
    def _():
        @pl.when(j == 0)
        def _():
            for cp in gather(0, 0):
                cp.start()

        @pl.when(j + 1 < nu)
        def _():
            for cp in gather(j + 1, 1 - slot):
                cp.start()

        for cp in gather(j, slot):
            cp.wait()

        @pl.when(j >= 2)
        def _():
            writeback(j - 2, slot, lambda cp: cp.wait())

        xb = xbuf[slot]
        hid = _silu(_dot(xb, wg_ref[...])) * _dot(xb, wu_ref[...])
        ybuf[slot] = _dot(hid.astype(BF16), wd_ref[...]).astype(BF16)
        writeback(j, slot, lambda cp: cp.start())

        @pl.when(j == nu - 1)
        def _():
            writeback(j, slot, lambda cp: cp.wait())

            @pl.when(j >= 1)
            def _():
                writeback(j - 1, 1 - slot, lambda cp: cp.wait())


def _experts(rows, src_chunk, block_e, n_used, dummy, w):
    n_rows = rows.shape[0]
    nblk = block_e.shape[0]
    y_init = jnp.zeros((n_rows, D_MODEL), BF16)
    wmap = lambda j, sc, be, nu: (be[j], 0, 0)
    grid_spec = pltpu.PrefetchScalarGridSpec(
        num_scalar_prefetch=3, grid=(nblk,),
        in_specs=[pl.BlockSpec(memory_space=pl.ANY),
                  pl.BlockSpec((None, D_MODEL, D_EXPERT), wmap), pl.BlockSpec((None, D_MODEL, D_EXPERT), wmap),
                  pl.BlockSpec((None, D_EXPERT, D_MODEL), wmap),
                  pl.BlockSpec(memory_space=pl.ANY)],
        out_specs=pl.BlockSpec(memory_space=pl.ANY),
        scratch_shapes=[pltpu.VMEM((2, MOE_BLOCK, D_MODEL), BF16), pltpu.VMEM((2, MOE_BLOCK, D_MODEL), BF16),
                        pltpu.SemaphoreType.DMA((2,)), pltpu.SemaphoreType.DMA((2,))])
    return pl.pallas_call(functools.partial(_expert_kernel, dummy=dummy),
                          out_shape=jax.ShapeDtypeStruct((n_rows, D_MODEL), BF16), grid_spec=grid_spec,
                          input_output_aliases={7: 0}, compiler_params=_cparams(1), name="moe_experts")(
        src_chunk, block_e, n_used, rows, w["w_eg"], w["w_eu"], w["w_ed"], y_init)


def _combine_kernel(x_ref, info_ref, y_ref, g_ref, o_ref, *, final):
    tm, lr = x_ref.shape[0], y_ref.shape[0]
    info = info_ref[...]
    col = lax.broadcasted_iota(I32, (tm, lr), 1).astype(F32)
    gmat = jnp.where(col == info[:, 4:5], info[:, 2:3], 0.0) + jnp.where(col == info[:, 5:6], info[:, 3:4], 0.0)
    y = x_ref[...] + _dot(gmat.astype(BF16), y_ref[...])
    if final:
        y = _rms(y, g_ref[...])
    o_ref[...] = y


def _combine(x2, info, y_loc, tm, g_final, final):
    n = x2.shape[0]
    lr = _local_rows(tm)
    return pl.pallas_call(
        functools.partial(_combine_kernel, final=final),
        out_shape=jax.ShapeDtypeStruct((n, D_MODEL), F32), grid=(n // tm,),
        in_specs=[pl.BlockSpec((tm, D_MODEL), lambda i: (i, 0)), pl.BlockSpec((tm, 128), lambda i: (i, 0)),
                  pl.BlockSpec((lr, D_MODEL), lambda i: (i, 0)), pl.BlockSpec((1, D_MODEL), lambda i: (0, 0))],
        out_specs=pl.BlockSpec((tm, D_MODEL), lambda i: (i, 0)),
        compiler_params=_cparams(1), name="moe_combine")(x2, info, y_loc, g_final)


def _moe(x2, rows, info, nch, tm, w, g_final, final):
    n = x2.shape[0]
    ntile = n // tm
    lrc = _local_rows(tm) // MOE_CHUNK
    cpb = MOE_BLOCK // MOE_CHUNK
    dummy = lrc - 1
    nc = nch.reshape(ntile, 8, 128)[:, 0, EXPERT_LANE0:EXPERT_LANE0 + N_EXPERTS].astype(I32)
    src_base = jnp.arange(ntile, dtype=I32)[:, None] * lrc + (jnp.cumsum(nc, axis=1) - nc)
    pe = (jnp.sum(nc, axis=0) + cpb - 1) // cpb * cpb
    pend = jnp.cumsum(pe)
    dst = (pend - pe)[None, :] + (jnp.cumsum(nc, axis=0) - nc)
    dst_f, nc_f, src_f = dst.T.reshape(-1), nc.T.reshape(-1), src_base.T.reshape(-1)
    max_chunks = ntile * ((TOP_K * tm + N_EXPERTS * (MOE_CHUNK - 1)) // MOE_CHUNK) + N_EXPERTS * (cpb - 1)
    nblk = -(-max_chunks // cpb)
    pos = jnp.arange(nblk * cpb, dtype=I32)
    run = jnp.sum((dst_f[None, :] <= pos[:, None]).astype(I32), axis=1) - 1
    at_run = jnp.stack([dst_f, nc_f, src_f], axis=1)[run]
    rel = pos - at_run[:, 0]
    src_chunk = jnp.where(rel < at_run[:, 1], at_run[:, 2] + rel, dummy).astype(I32)
    blk0 = jnp.arange(nblk, dtype=I32) * cpb
    block_e = jnp.minimum(jnp.sum((pend[None, :] <= blk0[:, None]).astype(I32), axis=1), N_EXPERTS - 1).astype(I32)
    n_used = (pend[-1:] // cpb).astype(I32)
    y_loc = _experts(rows, src_chunk, block_e, n_used, dummy, w)
    return _combine(x2, info, y_loc, tm, g_final, final)


def _rope_tables(pos):
    half = ROPE_DIM // 2
    inv = ROPE_BASE ** (-jnp.arange(half, dtype=F32) * 2.0 / ROPE_DIM)
    ang = pos.astype(F32)[:, None] * inv[None, :]
    c, s = jnp.cos(ang), jnp.sin(ang)
    cos32 = jnp.concatenate([c, c], axis=1)
    sin32 = jnp.concatenate([-s, s], axis=1)
    return jnp.tile(cos32, (1, MLA_H)), jnp.tile(sin32, (1, MLA_H))


def _prep_layer(l, p):
    w_in = p["w_in"][l]
    zeros80 = jnp.zeros((D_MODEL, 80), F32)
    w_in_p = jnp.concatenate([w_in[:, 0:640], w_in[:, 640:672], w_in[:, 1184:1200], zeros80,
                              w_in[:, 672:1184], w_in[:, 1200:1456]], axis=1).astype(BF16)
    assert w_in_p.shape[1] == IN_PAD
    w_uq = p["w_uq"][l].reshape(Q_LORA, MLA_H, NOPE + ROPE_DIM)
    w_uq_p = jnp.concatenate([w_uq[:, :, :NOPE].reshape(Q_LORA, MLA_H * NOPE),
                              w_uq[:, :, NOPE:].reshape(Q_LORA, MLA_H * ROPE_DIM)], axis=1).astype(BF16)
    eye_h = jnp.eye(MLA_H, dtype=F32)
    w_ukbd = jnp.einsum("hnc,hg->hngc", p["w_uk"][l].transpose(1, 2, 0), eye_h).reshape(MLA_H * NOPE, MLA_H * KV_LORA).astype(BF16)
    w_uvbd = jnp.einsum("hcv,hg->hcgv", p["w_uv"][l].transpose(1, 0, 2), eye_h).reshape(MLA_H * KV_LORA, MLA_H * V_DIM).astype(BF16)
    w_gk2 = jnp.zeros((128, 128), F32).at[ROPE_DIM:ROPE_DIM + GATE_RANK, :].set(p["w_gk2"][l]).astype(BF16)
    w_poolbd = jnp.einsum("gcd,gk->gckd", p["w_pool"][l], jnp.eye(POOL_GROUPS, dtype=F32)).reshape(POOL_W, POOL_W).astype(BF16)
    w_out = p["w_out"][l].astype(BF16)
    w_r = jnp.zeros((D_MODEL, 128), F32).at[:, 0:N_GROUPS].set(p["w_rg"][l]).at[:, EXPERT_LANE0:EXPERT_LANE0 + N_EXPERTS].set(p["w_re"][l])
    w_r_hi = w_r.astype(BF16)
    b_r = jnp.zeros((1, 128), F32).at[0, 0:N_GROUPS].set(p["b_rg"][l]).at[0, EXPERT_LANE0:EXPERT_LANE0 + N_EXPERTS].set(p["b_re"][l])
    return {
        "g_mix": p["g_mix"][l][None, :], "w_in": w_in_p, "g_qn": p["g_qn"][l][None, :], "w_uq": w_uq_p, "w_ukbd": w_ukbd,
        "g_kvn": p["g_kvn"][l][None, :], "w_gk2": w_gk2, "b_gk": p["b_gk"][l][None, :],
        "w_poolbd": w_poolbd, "pool_scale": p["pool_scale"][l][None, :],
        "g_gla": jnp.tile(p["g_gla"][l], GLA_H)[None, :],
        "w_uvbd": w_uvbd, "wo_a": w_out[0:256], "wo_b": w_out[256:768], "wo_c": w_out[768:1024],
        "g_x": p["g_x"][l][None, :], "w_xq": p["w_xq"][l].astype(BF16), "w_xo": p["w_xo"][l].astype(BF16),
        "g_mem": p["g_mem"][l][None, :], "w_xk": p["w_xk"][l].astype(BF16), "w_xv": p["w_xv"][l].astype(BF16),
        "g_ffn": p["g_ffn"][l][None, :], "w_r_hi": w_r_hi, "w_r_lo": (w_r - w_r_hi.astype(F32)).astype(BF16), "b_r": b_r,
        "w_eg": p["w_eg"][l].astype(BF16), "w_eu": p["w_eu"][l].astype(BF16), "w_ed": p["w_ed"][l].astype(BF16),
    }


GLA_CHUNK = 64


def kernel(x_prompt, x_sample, mem_prompt, cache_ckv, cache_kpe, page_table, state_pool, state_gla, cache_mem_k, cache_mem_v, g_mix, w_in, w_pool, pool_scale, g_qn, w_uq, g_kvn, w_uk, w_uv, w_gk2, b_gk, g_gla, w_out, g_x, g_mem, w_xq, w_xk, w_xv, w_xo, g_ffn, w_rg, b_rg, w_re, b_re, w_eg, w_eu, w_ed, g_final):
    params = dict(g_mix=g_mix, w_in=w_in, w_pool=w_pool, pool_scale=pool_scale, g_qn=g_qn, w_uq=w_uq, g_kvn=g_kvn, w_uk=w_uk,
                  w_uv=w_uv, w_gk2=w_gk2, b_gk=b_gk, g_gla=g_gla, w_out=w_out, g_x=g_x, g_mem=g_mem, w_xq=w_xq, w_xk=w_xk,
                  w_xv=w_xv, w_xo=w_xo, g_ffn=g_ffn, w_rg=w_rg, b_rg=b_rg, w_re=w_re, b_re=b_re, w_eg=w_eg, w_eu=w_eu, w_ed=w_ed)
    depth = w_in.shape[0]
    bsz, seq, _ = x_prompt.shape
    dbsz, dseq, _ = x_sample.shape
    past_len = page_table.shape[1] * cache_ckv.shape[2]
    gfin = g_final[None, :]
    cache_kpe_t = jnp.swapaxes(cache_kpe, 2, 3)

    cos_p, sin_p = _rope_tables(jnp.arange(seq, dtype=I32))
    cos_s, sin_s = _rope_tables(past_len + (jnp.arange(dbsz * dseq, dtype=I32) % dseq))

    xp = x_prompt.reshape(bsz * seq, D_MODEL)
    xs = x_sample.reshape(dbsz * dseq, D_MODEL)
    mem = mem_prompt.reshape(bsz * N_MEM, D_MODEL)
    gpad = 8 - dseq
    assert 0 <= gpad < 8
    assert past_len >= max(POOL_WINDOWS) - 1

    outs = {k: [] for k in ("ckv_p", "kpe_p", "pool_p", "gla_p", "mk_p", "mv_p", "ckv_s", "kpe_s", "pool_s", "gla_s")}
    for l in range(depth):
        w = _prep_layer(l, params)
        final = l == depth - 1
        mk, mv = _memkv(mem, w)
        u, q_hm, kvb, ckv, kpe, gq, gk, gv, go, glog = _inproj(xp, cos_p, sin_p, w)
        ya, pool_new = _pool_prompt(u, bsz, w)
        olat = _mla_prompt(q_hm, kvb, bsz)
        yc, s_new = _gla(gq, gk, gv, go, glog, None, bsz, w, GLA_CHUNK)
        x2, rows, info, nch, tm = _post(xp, ya, olat, yc, mk.reshape(bsz, N_MEM, X_W), mv.reshape(bsz, N_MEM, X_W), bsz, w)
        xp = _moe(x2, rows, info, nch, tm, w, gfin, final)
        outs["ckv_p"].append(ckv.reshape(bsz, seq, KV_LORA))
        outs["kpe_p"].append(kpe.reshape(bsz, seq, ROPE_DIM))
        outs["pool_p"].append(pool_new)
        outs["gla_p"].append(s_new.reshape(bsz, GLA_H, GLA_DK, GLA_DV))
        outs["mk_p"].append(mk.reshape(bsz, N_MEM, X_H, X_HD))
        outs["mv_p"].append(mv.reshape(bsz, N_MEM, X_H, X_HD))
        u, q_hm, kvb, ckv, kpe, gq, gk, gv, go, glog = _inproj(xs, cos_s, sin_s, w)
        ue_tm = jnp.concatenate([state_pool[l], u.reshape(dbsz, dseq, POOL_W)], axis=1).transpose(1, 0, 2)
        ya_tm, st_tm = _pool_sample(ue_tm, w)
        ya = ya_tm.transpose(1, 0, 2).reshape(dbsz * dseq, POOL_W)
        q_s = q_hm.reshape(MLA_H, dbsz, dseq, QK_PAD).transpose(1, 0, 2, 3).reshape(dbsz, MLA_H * dseq, QK_PAD)
        knew = jnp.pad(kvb.reshape(dbsz, dseq, QK_PAD), ((0, 0), (0, 16 - dseq), (0, 0)))
        o_s = _mla_decode(q_s, knew, cache_ckv, cache_kpe_t, page_table, l)
        olat = o_s.reshape(dbsz, MLA_H, dseq, KV_LORA).transpose(0, 2, 1, 3).reshape(dbsz * dseq, MLA_H * KV_LORA).astype(BF16)

        def pad8(a):
            return jnp.pad(a.reshape(dbsz, dseq, -1), ((0, 0), (0, gpad), (0, 0))).reshape(dbsz * 8, -1)

        yc8, s_new = _gla(pad8(gq), pad8(gk), pad8(gv), pad8(go), pad8(glog),
                          state_gla[l].reshape(dbsz, GLA_H * GLA_DK, GLA_DV), dbsz, w, 8)
        yc = yc8.reshape(dbsz, 8, GLA_W)[:, :dseq].reshape(dbsz * dseq, GLA_W)
        x2, rows, info, nch, tm = _post(xs, ya, olat, yc, cache_mem_k.reshape(depth * dbsz, N_MEM, X_H, X_HD),
                                     cache_mem_v.reshape(depth * dbsz, N_MEM, X_H, X_HD), dbsz, w, kv_seq0=l * dbsz)
        xs = _moe(x2, rows, info, nch, tm, w, gfin, final)
        outs["ckv_s"].append(ckv.reshape(dbsz, dseq, KV_LORA))
        outs["kpe_s"].append(kpe.reshape(dbsz, dseq, ROPE_DIM))
        outs["pool_s"].append(st_tm.transpose(1, 0, 2))
        outs["gla_s"].append(s_new.reshape(dbsz, GLA_H, GLA_DK, GLA_DV))

    st = lambda k: jnp.stack(outs[k])
    return (xp.reshape(bsz, seq, D_MODEL), xs.reshape(dbsz, dseq, D_MODEL),
            st("ckv_p"), st("kpe_p"), st("pool_p"), st("gla_p"), st("mk_p"), st("mv_p"),
            st("ckv_s"), st("kpe_s"), st("pool_s"), st("gla_s"))
```

```python
import functools

import numpy as np
import jax
import jax.numpy as jnp
from jax import lax
from jax.experimental import pallas as pl
from jax.experimental.pallas import tpu as pltpu

F32 = jnp.float32
BF16 = jnp.bfloat16
I32 = jnp.int32

EPS = 1e-6
D_MODEL = 1024
POOL_GROUPS, POOL_GC = 4, 64
POOL_W = POOL_GROUPS * POOL_GC
POOL_WINDOWS = (2, 4, 8, 16)
POOL_BUF = 15
MLA_H, Q_LORA, KV_LORA, NOPE, ROPE_DIM, V_DIM = 8, 256, 128, 64, 32, 64
ROPE_BASE = 10000.0
MLA_SCALE = (NOPE + ROPE_DIM) ** -0.5
Q_PRESCALE = MLA_SCALE * 1.4426950408889634
GLA_H, GLA_DK, GLA_DV, GATE_RANK, GATE_TAU = 4, 32, 64, 16, 16.0
GLA_W = GLA_H * GLA_DV
N_MEM, X_H, X_HD = 256, 4, 128
X_W = X_H * X_HD
N_GROUPS, EXP_PER_GROUP, N_EXPERTS, TOP_K, D_EXPERT = 4, 8, 32, 2, 256
QK_PAD = 256
ONES_LANE = KV_LORA + ROPE_DIM
DECODE_SLOTS = 3
MLA_HEAD_GROUPS = 4
IN_PAD = 1536
EXPERT_LANE0 = 32
MOE_BLOCK = 512
MOE_CHUNK = 16
NEG = -1e30
VMEM_LIMIT = 56 * 1024 * 1024


def _pick(n, prefs):
    for p in prefs:
        if n % p == 0:
            return p
    raise ValueError(f"no tile in {prefs} divides {n}")


def _cparams(n_axes, **flags):
    return pltpu.CompilerParams(dimension_semantics=("arbitrary",) * n_axes, vmem_limit_bytes=VMEM_LIMIT,
                                flags=flags or None)


def _rms(x, g):
    ms = jnp.mean(x * x, axis=-1, keepdims=True)
    return x * lax.rsqrt(ms + EPS) * g


def _dot(a, b):
    return jnp.dot(a, b, preferred_element_type=F32)


def _dot_nt(a, b):
    return lax.dot_general(a, b, (((1,), (1,)), ((), ())), preferred_element_type=F32)


def _dot_tn(a, b):
    return lax.dot_general(a, b, (((0,), (0,)), ((), ())), preferred_element_type=F32)


def _split_bf16(x):
    hi = x.astype(BF16)
    lo = (x - hi.astype(F32)).astype(BF16)
    return hi, lo


def _swap16(x):
    w = x.shape[-1]
    lane = lax.broadcasted_iota(I32, x.shape, x.ndim - 1)
    first = (lane & 31) < 16
    return jnp.where(first, pltpu.roll(x, w - 16, x.ndim - 1), pltpu.roll(x, 16, x.ndim - 1))


def _silu(x):
    return x / (1.0 + jnp.exp(-x))


def _inproj_kernel(x_ref, gmix_ref, win_ref, cos_ref, sin_ref, gqn_ref, wuq_ref, wuk_ref, gkvn_ref, wgk_ref, bgk_ref,
                   u_ref, q_ref, kvb_ref, ckv_ref, kpe_ref, gq_ref, gk_ref, gv_ref, go_ref, glog_ref):
    h = _rms(x_ref[...], gmix_ref[...]).astype(BF16)
    y = _dot(h, win_ref[...])
    u_ref[...] = y[:, 0:256]
    gq_ref[...] = y[:, 768:896]
    gk_ref[...] = y[:, 896:1024]
    gv_ref[...] = y[:, 1024:1280]
    go_ref[...] = y[:, 1280:1536]
    cos = cos_ref[...]
    sin = sin_ref[...]
    ckv = _rms(y[:, 512:640], gkvn_ref[...])
    grp = y[:, 640:768]
    grp_r = grp * cos[:, 0:128] + _swap16(grp) * sin[:, 0:128]
    ckv_ref[...] = ckv
    kpe_ref[...] = grp_r[:, 0:ROPE_DIM]
    lane = lax.broadcasted_iota(I32, grp_r.shape, 1)
    kvb_ref[:, 0:128] = ckv.astype(BF16)
    kvb_ref[:, 128:256] = jnp.where(lane < ROPE_DIM, grp_r, jnp.where(lane == ROPE_DIM, 1.0, 0.0)).astype(BF16)
    gl = _dot(grp.astype(BF16), wgk_ref[...]) + bgk_ref[...]
    glog_ref[...] = (jnp.minimum(gl, 0.0) - jnp.log(1.0 + jnp.exp(-jnp.abs(gl)))) * (1.0 / GATE_TAU)
    cqn = _rms(y[:, 256:512], gqn_ref[...]).astype(BF16)
    q = _dot(cqn, wuq_ref[...])
    qlat = _dot(q[:, 0:512].astype(BF16), wuk_ref[...]) * Q_PRESCALE
    qr = q[:, 512:768]
    qr = (qr * cos + _swap16(qr) * sin) * Q_PRESCALE
    for hh in range(MLA_H):
        q_ref[hh, :, 0:128] = qlat[:, 128 * hh:128 * hh + 128].astype(BF16)
        col = qr[:, 128 * (hh // 4):128 * (hh // 4) + 128]
        sh = (128 - 32 * (hh % 4)) % 128
        if sh:
            col = pltpu.roll(col, sh, 1)
        q_ref[hh, :, 128:256] = jnp.where(lane < ROPE_DIM, col, 0.0).astype(BF16)


def _inproj(x, cos_t, sin_t, w):
    n = x.shape[0]
    tm = _pick(n, (512, 256, 128, 64, 32, 16))
    tab_tiles = cos_t.shape[0] // tm
    assert cos_t.shape[0] % tm == 0
    row = lambda i: (i, 0)
    tab = lambda i: (i % tab_tiles, 0)
    full = lambda i: (0, 0)
    wspec = lambda a: pl.BlockSpec(a.shape, full)
    outs = [
        jax.ShapeDtypeStruct((n, 256), F32),
        jax.ShapeDtypeStruct((MLA_H, n, QK_PAD), BF16),
        jax.ShapeDtypeStruct((n, QK_PAD), BF16),
        jax.ShapeDtypeStruct((n, KV_LORA), F32),
        jax.ShapeDtypeStruct((n, ROPE_DIM), F32),
        jax.ShapeDtypeStruct((n, 128), F32),
        jax.ShapeDtypeStruct((n, 128), F32),
        jax.ShapeDtypeStruct((n, 256), F32),
        jax.ShapeDtypeStruct((n, 256), F32),
        jax.ShapeDtypeStruct((n, 128), F32),
    ]
    out_specs = [
        pl.BlockSpec((tm, 256), row),
        pl.BlockSpec((MLA_H, tm, QK_PAD), lambda i: (0, i, 0)),
        pl.BlockSpec((tm, QK_PAD), row),
        pl.BlockSpec((tm, KV_LORA), row),
        pl.BlockSpec((tm, ROPE_DIM), row),
        pl.BlockSpec((tm, 128), row),
        pl.BlockSpec((tm, 128), row),
        pl.BlockSpec((tm, 256), row),
        pl.BlockSpec((tm, 256), row),
        pl.BlockSpec((tm, 128), row),
    ]
    ins = [x, w["g_mix"], w["w_in"], cos_t, sin_t, w["g_qn"], w["w_uq"], w["w_ukbd"], w["g_kvn"], w["w_gk2"], w["b_gk"]]
    in_specs = [pl.BlockSpec((tm, D_MODEL), row), wspec(w["g_mix"]), wspec(w["w_in"]),
                pl.BlockSpec((tm, 256), tab), pl.BlockSpec((tm, 256), tab),
                wspec(w["g_qn"]), wspec(w["w_uq"]), wspec(w["w_ukbd"]), wspec(w["g_kvn"]), wspec(w["w_gk2"]), wspec(w["b_gk"])]
    return pl.pallas_call(_inproj_kernel, out_shape=outs, grid=(n // tm,), in_specs=in_specs, out_specs=out_specs,
                          compiler_params=_cparams(1), name="inproj")(*ins)


def _pool_mix(sums, u, cnts, wbd, scale):
    cols = []
    for c in range(2):
        wa, wb = POOL_WINDOWS[2 * c], POOL_WINDOWS[2 * c + 1]
        sa, sb = sums[(c, wa)], sums[(c, wb)]
        lane = lax.broadcasted_iota(I32, sa.shape, 1)
        pooled = jnp.where(lane < POOL_GC, sa / cnts[wa], sb / cnts[wb]) - u[:, 128 * c:128 * c + 128]
        cols.append(pooled)
    pooled = jnp.concatenate(cols, axis=1).astype(BF16)
    return (_dot(pooled, wbd) * scale).astype(BF16)


def _pool_kernel(u_ref, wbd_ref, scale_ref, ya_ref, st_ref, ue_ref, *, tp, nt):
    j = pl.program_id(1)

    @pl.when(j == 0)
    def _():
        ue_ref[0:16, :] = jnp.zeros((16, POOL_W), F32)

    @pl.when(j > 0)
    def _():
        ue_ref[0:16, :] = ue_ref[tp:tp + 16, :]

    u = u_ref[...]
    ue_ref[16:16 + tp, :] = u
    t = j * tp + lax.broadcasted_iota(I32, (tp, 1), 0)
    cnts = {w: jnp.minimum(t + 1, w).astype(F32) for w in POOL_WINDOWS}
    sums = {}
    for c in range(2):
        wa, wb = POOL_WINDOWS[2 * c], POOL_WINDOWS[2 * c + 1]
        acc = None
        for k in range(wb):
            sl = ue_ref[16 - k:16 - k + tp, 128 * c:128 * c + 128]
            acc = sl if acc is None else acc + sl
            if k + 1 == wa:
                sums[(c, wa)] = acc
        sums[(c, wb)] = acc
    ya_ref[...] = _pool_mix(sums, u, cnts, wbd_ref[...], scale_ref[...])

    @pl.when(j == nt - 1)
    def _():
        st_ref[0] = ue_ref[tp + 1:tp + 16, :]


def _pool_prompt(u, bsz, w):
    n = u.shape[0]
    t = n // bsz
    tp = _pick(t, (512, 256, 128, 64, 32, 16))
    nt = t // tp
    kern = functools.partial(_pool_kernel, tp=tp, nt=nt)
    return pl.pallas_call(
        kern,
        out_shape=[jax.ShapeDtypeStruct((n, POOL_W), BF16), jax.ShapeDtypeStruct((bsz, POOL_BUF, POOL_W), F32)],
        grid=(bsz, nt),
        in_specs=[pl.BlockSpec((tp, POOL_W), lambda b, j: (b * nt + j, 0)),
                  pl.BlockSpec((POOL_W, POOL_W), lambda b, j: (0, 0)),
                  pl.BlockSpec((1, POOL_W), lambda b, j: (0, 0))],
        out_specs=[pl.BlockSpec((tp, POOL_W), lambda b, j: (b * nt + j, 0)),
                   pl.BlockSpec((1, POOL_BUF, POOL_W), lambda b, j: (b, 0, 0))],
        scratch_shapes=[pltpu.VMEM((tp + 16, POOL_W), F32)],
        compiler_params=_cparams(2), name="pool_prompt")(u, w["w_poolbd"], w["pool_scale"])


def _pool_step_kernel(ue_ref, wbd_ref, scale_ref, ya_ref, st_ref, *, tt):
    cnts = {w: jnp.float32(w) for w in POOL_WINDOWS}
    for t in range(tt):
        sums = {}
        for c in range(2):
            wa, wb = POOL_WINDOWS[2 * c], POOL_WINDOWS[2 * c + 1]
            acc = None
            for k in range(wb):
                sl = ue_ref[POOL_BUF + t - k, :, 128 * c:128 * c + 128]
                acc = sl if acc is None else acc + sl
                if k + 1 == wa:
                    sums[(c, wa)] = acc
            sums[(c, wb)] = acc
        ya_ref[t] = _pool_mix(sums, ue_ref[POOL_BUF + t], cnts, wbd_ref[...], scale_ref[...])
    for r in range(POOL_BUF):
        st_ref[r] = ue_ref[tt + r]


def _pool_sample(ue_tm, w):
    rows, bsz, _ = ue_tm.shape
    tt = rows - POOL_BUF
    kern = functools.partial(_pool_step_kernel, tt=tt)
    full3 = lambda i: (0, 0, 0)
    return pl.pallas_call(
        kern,
        out_shape=[jax.ShapeDtypeStruct((tt, bsz, POOL_W), BF16), jax.ShapeDtypeStruct((POOL_BUF, bsz, POOL_W), F32)],
        grid=(1,),
        in_specs=[pl.BlockSpec(ue_tm.shape, full3), pl.BlockSpec((POOL_W, POOL_W), lambda i: (0, 0)),
                  pl.BlockSpec((1, POOL_W), lambda i: (0, 0))],
        out_specs=[pl.BlockSpec((tt, bsz, POOL_W), full3), pl.BlockSpec((POOL_BUF, bsz, POOL_W), full3)],
        compiler_params=_cparams(1), name="pool_sample")(ue_tm, w["w_poolbd"], w["pool_scale"])


def _mla_prompt_kernel(qi_ref, kj_ref, q_ref, k_ref, o_ref, m_ref, acc_ref, *, tq, tk):
    p = pl.program_id(1)
    qi = qi_ref[p]
    kj = kj_ref[p]
    rows = MLA_H * tq

    @pl.when(kj == 0)
    def _():
        m_ref[...] = jnp.full((rows, 128), NEG, F32)
        acc_ref[...] = jnp.zeros((rows, QK_PAD), F32)

    def step(masked, nk):
        k = k_ref[0:nk, :]
        hg = MLA_H // MLA_HEAD_GROUPS
        gr = hg * tq
        for g in range(MLA_HEAD_GROUPS):
            rs = slice(g * gr, (g + 1) * gr)
            q = q_ref[g * hg:(g + 1) * hg].reshape(gr, QK_PAD)
            s = _dot_nt(q, k)
            if masked:
                r = lax.broadcasted_iota(I32, (gr, nk), 0)
                c = lax.broadcasted_iota(I32, (gr, nk), 1)
                qpos = qi * tq + (r & (tq - 1))
                s = jnp.where(kj * tk + c <= qpos, s, NEG)
            m_old = m_ref[rs, :]
            m_new = jnp.maximum(m_old, jnp.max(s, axis=-1, keepdims=True))
            alpha = jnp.exp2(m_old - m_new)
            pr = jnp.exp2(s - jnp.concatenate([m_new] * (nk // 128), axis=1))
            acc_ref[rs, :] = jnp.concatenate([alpha, alpha], axis=1) * acc_ref[rs, :] + _dot(pr.astype(BF16), k)
            m_ref[rs, :] = m_new

    visible = (qi + 1) * tq - kj * tk
    half = tk // 2 if (tk % 256 == 0 and tq <= tk // 2) else tk
    crosses = visible < tk + tq

    if half < tk:
        @pl.when(jnp.logical_and(crosses, visible <= half))
        def _():
            step(True, half)

    @pl.when(jnp.logical_and(crosses, visible > half) if half < tk else crosses)
    def _():
        step(True, tk)

    @pl.when(jnp.logical_not(crosses))
    def _():
        step(False, tk)

    @pl.when(kj == ((qi + 1) * tq - 1) // tk)
    def _():
        acc = acc_ref[...]
        o = acc[:, 0:KV_LORA] / acc[:, ONES_LANE:ONES_LANE + 1]
        for h in range(MLA_H):
            o_ref[:, KV_LORA * h:KV_LORA * (h + 1)] = o[h * tq:(h + 1) * tq].astype(BF16)


def _mla_prompt(q_hm, kvb, bsz):
    n = kvb.shape[0]
    t = n // bsz
    tq = _pick(t, (512, 256, 128, 64, 32, 16))
    tk = _pick(t, (512, 256, 128, 64, 32, 16))
    assert tq & (tq - 1) == 0
    nq, nk = t // tq, t // tk
    pairs = [(i, j) for i in range(nq) for j in range(((i + 1) * tq - 1) // tk + 1)]
    qi = jnp.asarray(np.array([p[0] for p in pairs], np.int32))
    kj = jnp.asarray(np.array([p[1] for p in pairs], np.int32))
    rows = MLA_H * tq
    kern = functools.partial(_mla_prompt_kernel, tq=tq, tk=tk)
    grid_spec = pltpu.PrefetchScalarGridSpec(
        num_scalar_prefetch=2, grid=(bsz, len(pairs)),
        in_specs=[pl.BlockSpec((MLA_H, tq, QK_PAD), lambda b, p, qi, kj: (0, b * nq + qi[p], 0)),
                  pl.BlockSpec((tk, QK_PAD), lambda b, p, qi, kj: (b * nk + kj[p], 0))],
        out_specs=pl.BlockSpec((tq, MLA_H * KV_LORA), lambda b, p, qi, kj: (b * nq + qi[p], 0)),
        scratch_shapes=[pltpu.VMEM((rows, 128), F32), pltpu.VMEM((rows, QK_PAD), F32)])
    return pl.pallas_call(kern, out_shape=jax.ShapeDtypeStruct((n, MLA_H * KV_LORA), BF16), grid_spec=grid_spec,
                          compiler_params=_cparams(2), name="mla_prompt")(qi, kj, q_hm, kvb)


def _mla_decode_kernel(pt_ref, q_ref, knew_ref, cckv_ref, ckpe_ref, o_ref,
                       cbuf, pbuf, csem, psem, m_ref, l_ref, acc_ref, *, layer, ng, gp, page, tt, total):
    s_idx = pl.program_id(0)
    grp = s_idx % ng
    slot = s_idx % DECODE_SLOTS

    def copies(step, slot_):
        out = []
        for g in range(gp):
            pg = pt_ref[step * gp + g]
            out.append(pltpu.make_async_copy(cckv_ref.at[layer, pg], cbuf.at[slot_, pl.ds(g * page, page)], csem.at[slot_]))
            out.append(pltpu.make_async_copy(ckpe_ref.at[layer, pg], pbuf.at[slot_, :, pl.ds(g * page, page)], psem.at[slot_]))
        return out

    @pl.when(s_idx == 0)
    def _():
        for ahead in range(min(DECODE_SLOTS - 1, total)):
            for c in copies(ahead, ahead):
                c.start()

    @pl.when(s_idx + (DECODE_SLOTS - 1) < total)
    def _():
        for c in copies(s_idx + (DECODE_SLOTS - 1), (s_idx + (DECODE_SLOTS - 1)) % DECODE_SLOTS):
            c.start()

    @pl.when(grp == 0)
    def _():
        m_ref[...] = jnp.full(m_ref.shape, NEG, F32)
        l_ref[...] = jnp.zeros(l_ref.shape, F32)
        acc_ref[...] = jnp.zeros(acc_ref.shape, F32)

    for c in copies(s_idx, slot):
        c.wait()

    q = q_ref[0]

    def update(s, v):
        m_old = m_ref[...]
        m_new = jnp.maximum(m_old, jnp.max(s, axis=-1, keepdims=True))
        alpha = jnp.exp2(m_old - m_new)
        pr = jnp.exp2(s - m_new[:, 0:1])
        l_ref[...] = alpha * l_ref[...] + jnp.sum(pr, axis=-1, keepdims=True)
        acc_ref[...] = alpha * acc_ref[...] + _dot(pr.astype(BF16), v)
        m_ref[...] = m_new

    kc = cbuf[slot].astype(BF16)
    kr_t = pbuf[slot].astype(BF16)
    s = _dot_nt(q[:, 0:KV_LORA], kc) + _dot(q[:, KV_LORA:KV_LORA + ROPE_DIM], kr_t)
    update(s, kc)

    @pl.when(grp == ng - 1)
    def _():
        kn = knew_ref[0]
        sn = _dot_nt(q, kn)
        r = lax.broadcasted_iota(I32, sn.shape, 0)
        c = lax.broadcasted_iota(I32, sn.shape, 1)
        sn = jnp.where(c <= (r % tt), sn, NEG)
        update(sn, kn[:, 0:KV_LORA])
        o_ref[0] = acc_ref[...] / l_ref[...]


def _mla_decode(q_s, knew, cache_ckv, cache_kpe_t, page_table, layer):
    dbsz, rows, _ = q_s.shape
    tt = rows // MLA_H
    n_pages = page_table.shape[1]
    page = cache_ckv.shape[2]
    gp = _pick(n_pages, (64, 32, 16, 8, 4, 2, 1))
    ng = n_pages // gp
    total = dbsz * ng
    kern = functools.partial(_mla_decode_kernel, layer=layer, ng=ng, gp=gp, page=page, tt=tt, total=total)
    grid_spec = pltpu.PrefetchScalarGridSpec(
        num_scalar_prefetch=1, grid=(total,),
        in_specs=[pl.BlockSpec((1, rows, QK_PAD), lambda s, pt: (s // ng, 0, 0)),
                  pl.BlockSpec((1, 16, QK_PAD), lambda s, pt: (s // ng, 0, 0)),
                  pl.BlockSpec(memory_space=pl.ANY), pl.BlockSpec(memory_space=pl.ANY)],
        out_specs=pl.BlockSpec((1, rows, KV_LORA), lambda s, pt: (s // ng, 0, 0)),
        scratch_shapes=[pltpu.VMEM((DECODE_SLOTS, gp * page, KV_LORA), F32), pltpu.VMEM((DECODE_SLOTS, ROPE_DIM, gp * page), F32),
                        pltpu.SemaphoreType.DMA((DECODE_SLOTS,)), pltpu.SemaphoreType.DMA((DECODE_SLOTS,)),
                        pltpu.VMEM((rows, 128), F32), pltpu.VMEM((rows, 128), F32), pltpu.VMEM((rows, KV_LORA), F32)])
    return pl.pallas_call(kern, out_shape=jax.ShapeDtypeStruct((dbsz, rows, KV_LORA), F32), grid_spec=grid_spec,
                          compiler_params=_cparams(1), name="mla_decode")(
        page_table.reshape(-1), q_s, knew, cache_ckv, cache_kpe_t)


def _gla_constants(c):
    t = np.arange(c)
    blocks = [(t[None, :] <= t[:, None]), (t[None, :] > t[:, None])]
    masks = []
    m = c // 2
    while m >= 1:
        bd = (t // (2 * m)) * 2 * m + m
        upper = t >= bd
        a = upper[:, None] & (t[None, :] >= bd[:, None]) & (t[None, :] <= t[:, None])
        b = (~upper)[:, None] & (t[None, :] > t[:, None]) & (t[None, :] <= bd[:, None] - 1)
        blocks += [a, b]
        same = (t[:, None] // (2 * m)) == (t[None, :] // (2 * m))
        masks.append(same & upper[:, None] & (~upper)[None, :])
        m //= 2
    masks.append(t[:, None] == t[None, :])
    sel = np.concatenate(blocks, axis=0).astype(np.float32)
    masks = np.concatenate([np.tile(mm, (1, GLA_H)) for mm in masks], axis=0).astype(np.float32)
    r = np.arange(GLA_H * c)
    kmask = ((r[:, None] // c) == (np.arange(128)[None, :] // GLA_DK)).astype(np.float32)
    vmask = ((r[:, None] // c) == (np.arange(GLA_W)[None, :] // GLA_DV)).astype(np.float32)
    smask = ((np.arange(128)[:, None] // GLA_DK) == (np.arange(GLA_W)[None, :] // GLA_DV)).astype(np.float32)
    last = np.zeros((c, GLA_W), np.float32)
    last[c - 1, :] = 1.0
    gsum = ((np.arange(GLA_W)[:, None] // GLA_DV) == (np.arange(GLA_W)[None, :] // GLA_DV)).astype(np.float32)
    return sel, masks, kmask, vmask, smask, last, gsum


def _gla_kernel(*refs, c, nsub, nsteps, has_init, sps):
    if has_init:
        (q_ref, k_ref, v_ref, go_ref, g_ref, s0_ref, sel_ref, masks_ref, kmask_ref, vmask_ref, smask_ref, last_ref,
         gsum_ref, ggla_ref, y_ref, sout_ref, s_ref) = refs
    else:
        (q_ref, k_ref, v_ref, go_ref, g_ref, sel_ref, masks_ref, kmask_ref, vmask_ref, smask_ref, last_ref,
         gsum_ref, ggla_ref, y_ref, sout_ref, s_ref) = refs
        s0_ref = None
    j = pl.program_id(1)
    nlev = int(np.log2(c))
    smask = smask_ref[...]

    def load_state(sq):
        if has_init:
            s0 = s0_ref[sq]
            s_ref[...] = jnp.concatenate([s0] * GLA_H, axis=1) * smask
        else:
            s_ref[...] = jnp.zeros(s_ref.shape, F32)

    def store_state(sq):
        s = s_ref[...]
        acc = s[:, 0:GLA_DV]
        for h in range(1, GLA_H):
            acc = acc + s[:, GLA_DV * h:GLA_DV * (h + 1)]
        sout_ref[sq] = acc

    def chunk(i, carry):
        sl = pl.ds(i * c if isinstance(i, int) else pl.multiple_of(i * c, c), c)
        q = q_ref[sl, :] * (GLA_DK ** -0.5)
        k = k_ref[sl, :]
        v = v_ref[sl, :].astype(BF16)
        g_hi, g_lo = _split_bf16(g_ref[sl, :])
        sel = sel_ref[...]
        ee = _dot(sel, jnp.concatenate([g_hi, g_lo], axis=1))
        e_all = jnp.exp(ee[:, 0:128] + ee[:, 128:256])
        eb = e_all[0:c]
        s_old = s_ref[...]
        o = _dot((q * eb).astype(BF16), s_old.astype(BF16))
        kmask = kmask_ref[...]
        att = None
        for lev in range(nlev + 1):
            if lev < nlev:
                ql = (q * e_all[(2 + 2 * lev) * c:(3 + 2 * lev) * c]).astype(BF16)
                kl = k * e_all[(3 + 2 * lev) * c:(4 + 2 * lev) * c]
            else:
                ql, kl = q.astype(BF16), k
            kb = (jnp.concatenate([kl] * GLA_H, axis=0) * kmask).astype(BF16)
            part = _dot_nt(ql, kb) * masks_ref[lev * c:(lev + 1) * c, :]
            att = part if att is None else att + part
        vb = (jnp.concatenate([v] * GLA_H, axis=0) * vmask_ref[...].astype(BF16))
        o = o + _dot(att.astype(BF16), vb)
        kk = (k * e_all[c:2 * c]).astype(BF16)
        eb_hi, eb_lo = _split_bf16(eb)
        last = last_ref[...].astype(BF16)
        decay = _dot_tn(eb_hi, last) + _dot_tn(eb_lo, last)
        s_ref[...] = (decay * s_old + _dot_tn(kk, v)) * smask
        o2_hi, o2_lo = _split_bf16(o * o)
        gsum = gsum_ref[...]
        ms = (_dot(o2_hi, gsum) + _dot(o2_lo, gsum)) * (1.0 / GLA_DV)
        yc = o * lax.rsqrt(ms + EPS) * ggla_ref[...] * _silu(go_ref[sl, :])
        y_ref[sl, :] = yc.astype(BF16)
        return carry

    if sps > 1:
        for sq in range(sps):
            load_state(sq)
            for i in range(nsub):
                chunk(sq * nsub + i, 0)
            store_state(sq)
    else:
        pl.when(j == 0)(lambda: load_state(0))
        lax.fori_loop(0, nsub, chunk, 0, unroll=True)
        pl.when(j == nsteps - 1)(lambda: store_state(0))


def _gla(gq, gk, gv, go, glog, s0, bsz, w, chunk):
    n = gq.shape[0]
    t = n // bsz
    c = min(chunk, t)
    assert t % c == 0 and c & (c - 1) == 0 and c >= 8
    nsub = _pick(t // c, (8, 4, 2, 1))
    tc = c * nsub
    nsteps = t // tc
    consts = [jnp.asarray(a) for a in _gla_constants(c)]
    consts[0] = consts[0].astype(BF16)
    consts[6] = consts[6].astype(BF16)
    has_init = s0 is not None
    sps = _pick(bsz, (16, 8, 4, 2, 1)) if (nsteps == 1 and tc <= 16) else 1
    kern = functools.partial(_gla_kernel, c=c, nsub=nsub, nsteps=nsteps, has_init=has_init, sps=sps)
    rows = sps * tc
    row = lambda b, j: (b * nsteps + j, 0)
    full = lambda b, j: (0, 0)
    ins = [gq, gk, gv, go, glog]
    in_specs = [pl.BlockSpec((rows, 128), row), pl.BlockSpec((rows, 128), row), pl.BlockSpec((rows, 256), row),
                pl.BlockSpec((rows, 256), row), pl.BlockSpec((rows, 128), row)]
    if has_init:
        ins.append(s0)
        in_specs.append(pl.BlockSpec((sps, 128, GLA_DV), lambda b, j: (b, 0, 0)))
    ins += consts + [w["g_gla"]]
    in_specs += [pl.BlockSpec(a.shape, full) for a in consts] + [pl.BlockSpec(w["g_gla"].shape, full)]
    return pl.pallas_call(
        kern,
        out_shape=[jax.ShapeDtypeStruct((n, GLA_W), BF16), jax.ShapeDtypeStruct((bsz, 128, GLA_DV), F32)],
        grid=(bsz // sps, nsteps), in_specs=in_specs,
        out_specs=[pl.BlockSpec((rows, GLA_W), row), pl.BlockSpec((sps, 128, GLA_DV), lambda b, j: (b, 0, 0))],
        scratch_shapes=[pltpu.VMEM((128, GLA_W), F32)],
        compiler_params=_cparams(2), name="gla")(*ins)


def _memkv_kernel(m_ref, g_ref, wk_ref, wv_ref, mk_ref, mv_ref):
    h = _rms(m_ref[...], g_ref[...]).astype(BF16)
    mk_ref[...] = _dot(h, wk_ref[...])
    mv_ref[...] = _dot(h, wv_ref[...])


def _memkv(mem, w):
    n = mem.shape[0]
    tm = _pick(n, (256, 128, 64, 32, 16, 8))
    row = lambda i: (i, 0)
    full = lambda i: (0, 0)
    return pl.pallas_call(
        _memkv_kernel,
        out_shape=[jax.ShapeDtypeStruct((n, X_W), F32)] * 2, grid=(n // tm,),
        in_specs=[pl.BlockSpec((tm, D_MODEL), row), pl.BlockSpec((1, D_MODEL), full),
                  pl.BlockSpec((D_MODEL, X_W), full), pl.BlockSpec((D_MODEL, X_W), full)],
        out_specs=[pl.BlockSpec((tm, X_W), row)] * 2,
        compiler_params=_cparams(1), name="memkv")(mem, w["g_mem"], w["w_xk"], w["w_xv"])


def _post_pre(x_ref, ya_ref, ol_ref, yc_ref, wuv_ref, woa_ref, wob_ref, woc_ref, gx_ref, wxq_ref):
    yb = _dot(ol_ref[...], wuv_ref[...]).astype(BF16)
    x1 = x_ref[...] + _dot(ya_ref[...], woa_ref[...]) + _dot(yb, wob_ref[...]) + _dot(yc_ref[...], woc_ref[...])
    q = _dot(_rms(x1, gx_ref[...]).astype(BF16), wxq_ref[...])
    return x1, q


def _post_attn(q, mk_ref, mv_ref, nb, rpb):
    def head_kv(ref, h):
        if len(ref.shape) == 4:
            a = ref[:, :, h, :]
        else:
            a = ref[:, :, X_HD * h:X_HD * (h + 1)]
        return a.reshape(nb * N_MEM, X_HD).astype(BF16)

    outs = []
    for h in range(X_H):
        sl = slice(X_HD * h, X_HD * (h + 1))
        s = _dot_nt(q[:, sl].astype(BF16), head_kv(mk_ref, h)) * (X_HD ** -0.5)
        if nb > 1:
            r = lax.broadcasted_iota(I32, s.shape, 0)
            c = lax.broadcasted_iota(I32, s.shape, 1)
            s = jnp.where(r // rpb == c // N_MEM, s, NEG)
        e = jnp.exp(s - jnp.max(s, axis=-1, keepdims=True))
        p = e / jnp.sum(e, axis=-1, keepdims=True)
        outs.append(_dot(p.astype(BF16), head_kv(mv_ref, h)))
    return jnp.concatenate(outs, axis=1)


def _post_tail(x1, o, wxo_ref, gffn_ref, wrh_ref, wrl_ref, br_ref, ltri_ref, ustr_ref, x2_ref, rows_ref, info_ref, nch_ref):
    tm = x1.shape[0]
    lr = rows_ref.shape[0]
    x2 = x1 + _dot(o.astype(BF16), wxo_ref[...])
    x2_ref[...] = x2
    h3 = _rms(x2, gffn_ref[...])
    h_hi, h_lo = _split_bf16(h3)
    logit = _dot(h_hi, wrh_ref[...]) + _dot(h_lo, wrh_ref[...]) + _dot(h_hi, wrl_ref[...]) + br_ref[...]
    lane = lax.broadcasted_iota(I32, logit.shape, 1)
    gl = jnp.where(lane < N_GROUPS, logit, NEG)
    gmax = jnp.max(gl, axis=-1, keepdims=True)
    gsel = jnp.min(jnp.where(gl == gmax, lane, 1 << 20), axis=-1, keepdims=True)
    gw = 1.0 / jnp.sum(jnp.exp(gl - gmax), axis=-1, keepdims=True)
    emask = jnp.logical_and(lane >= EXPERT_LANE0, (lane - EXPERT_LANE0) // EXP_PER_GROUP == gsel)
    el = jnp.where(emask, logit, NEG)
    pe = jnp.where(emask, jnp.exp(el - jnp.max(el, axis=-1, keepdims=True)), 0.0)
    prob = pe / jnp.sum(pe, axis=-1, keepdims=True)
    prob = jnp.where(emask, prob, -1.0)
    p1 = jnp.max(prob, axis=-1, keepdims=True)
    i1 = jnp.min(jnp.where(prob == p1, lane, 1 << 20), axis=-1, keepdims=True)
    prob2 = jnp.where(lane == i1, -1.0, prob)
    p2 = jnp.max(prob2, axis=-1, keepdims=True)
    i2 = jnp.min(jnp.where(prob2 == p2, lane, 1 << 20), axis=-1, keepdims=True)
    gate1 = gw * p1 / (p1 + p2)
    gate2 = gw * p2 / (p1 + p2)
    oh1 = (lane == i1).astype(F32)
    oh2 = (lane == i2).astype(F32)
    both = oh1 + oh2
    before = _dot(ltri_ref[...], both.astype(BF16))
    cnt = jnp.sum(both, axis=0, keepdims=True)
    nch = jnp.floor((cnt + (MOE_CHUNK - 1.0)) * (1.0 / MOE_CHUNK))
    nch8 = jnp.broadcast_to(nch, (8, 128))
    off = _dot(nch8.astype(BF16), ustr_ref[...])[0:1] * float(MOE_CHUNK)
    pos = off + before
    v1 = oh1 * pos
    v2 = oh2 * pos
    loc1 = jnp.sum(v1, axis=-1, keepdims=True)
    loc2 = jnp.sum(v2, axis=-1, keepdims=True)
    ones8 = jnp.ones((8, 128), BF16)

    def as_row(v):
        hi = jnp.floor(v * (1.0 / 32.0))
        lo = v - 32.0 * hi
        return (32.0 * _dot_nt(ones8, hi.astype(BF16)) + _dot_nt(ones8, lo.astype(BF16)))[0:1]

    slot_row = lax.broadcasted_iota(I32, (lr, tm), 0).astype(F32)
    perm = jnp.logical_or(slot_row == as_row(v1), slot_row == as_row(v2))
    perm = jnp.where(perm, 1.0, 0.0).astype(BF16)
    rows_ref[...] = _dot(perm, h3.astype(BF16)).astype(BF16)
    nch_ref[...] = nch8
    info = jnp.where(lane == 2, gate1, 0.0)
    info = jnp.where(lane == 3, gate2, info)
    info = jnp.where(lane == 4, loc1, info)
    info = jnp.where(lane == 5, loc2, info)
    info_ref[...] = info


def _post_kernel(x_ref, ya_ref, ol_ref, yc_ref, mk_ref, mv_ref, wuv_ref, woa_ref, wob_ref, woc_ref, gx_ref, wxq_ref, wxo_ref,
                 gffn_ref, wrh_ref, wrl_ref, br_ref, ltri_ref, ustr_ref, x2_ref, rows_ref, info_ref, nch_ref):
    x1, q = _post_pre(x_ref, ya_ref, ol_ref, yc_ref, wuv_ref, woa_ref, wob_ref, woc_ref, gx_ref, wxq_ref)
    o = _post_attn(q, mk_ref, mv_ref, 1, x_ref.shape[0])
    _post_tail(x1, o, wxo_ref, gffn_ref, wrh_ref, wrl_ref, br_ref, ltri_ref, ustr_ref, x2_ref, rows_ref, info_ref, nch_ref)


def _post_short_kernel(x_ref, ya_ref, ol_ref, yc_ref, mk_ref, mv_ref, wuv_ref, woa_ref, wob_ref, woc_ref, gx_ref, wxq_ref,
                       wxo_ref, gffn_ref, wrh_ref, wrl_ref, br_ref, ltri_ref, ustr_ref, x2_ref, rows_ref, info_ref, nch_ref,
                       x1_s, q_s, o_s, *, nb, rpb):
    g = pl.program_id(0)
    tg = nb * rpb

    @pl.when(g == 0)
    def _():
        x1, q = _post_pre(x_ref, ya_ref, ol_ref, yc_ref, wuv_ref, woa_ref, wob_ref, woc_ref, gx_ref, wxq_ref)
        x1_s[...] = x1
        q_s[...] = q

    r0 = pl.multiple_of(g * tg, tg)
    o_s[pl.ds(r0, tg), :] = _post_attn(q_s[pl.ds(r0, tg), :], mk_ref, mv_ref, nb, rpb)

    @pl.when(g == pl.num_programs(0) - 1)
    def _():
        _post_tail(x1_s[...], o_s[...], wxo_ref, gffn_ref, wrh_ref, wrl_ref, br_ref, ltri_ref, ustr_ref,
                   x2_ref, rows_ref, info_ref, nch_ref)


def _local_rows(tm):
    need = TOP_K * tm + N_EXPERTS * (MOE_CHUNK - 1) + MOE_CHUNK
    return -(-need // 128) * 128


def _post(x, ya, olat, yc, mk, mv, bsz, w, kv_seq0=0):
    n = x.shape[0]
    t = n // bsz
    long_seq = t >= 16
    scratch = []
    if long_seq:
        tm = _pick(t, (512, 256, 128, 64, 32, 16))
        nb = 1
        grid = (bsz, t // tm)
        row = lambda b, j: (b * (t // tm) + j, 0)
        kvm = lambda b, j: (kv_seq0 + b, 0, 0)
        kern = _post_kernel
    else:
        nb = _pick(bsz, (8, 4, 2, 1))
        tm = n
        assert (nb * t) % 16 == 0 and kv_seq0 % nb == 0
        grid = (bsz // nb, 1)
        row = lambda b, j: (0, 0)
        kvm = lambda b, j: (kv_seq0 // nb + b, 0, 0)
        kern = functools.partial(_post_short_kernel, nb=nb, rpb=t)
        scratch = [pltpu.VMEM((n, D_MODEL), F32), pltpu.VMEM((n, X_W), F32), pltpu.VMEM((n, X_W), F32)]
    ltri = jnp.asarray(np.tril(np.ones((tm, tm), np.float32), -1)).astype(BF16)
    ustr = jnp.asarray(np.triu(np.ones((128, 128), np.float32), 1)).astype(BF16)
    lr = _local_rows(tm)
    ntile = n // tm
    tile = lambda b, j: (row(b, j)[0], 0)
    full = lambda b, j: (0, 0)
    wnames = ["w_uvbd", "wo_a", "wo_b", "wo_c", "g_x", "w_xq", "w_xo", "g_ffn", "w_r_hi", "w_r_lo", "b_r"]
    ins = [x, ya, olat, yc, mk, mv] + [w[k] for k in wnames] + [ltri, ustr]
    in_specs = [pl.BlockSpec((tm, D_MODEL), row), pl.BlockSpec((tm, POOL_W), row), pl.BlockSpec((tm, MLA_H * KV_LORA), row),
                pl.BlockSpec((tm, GLA_W), row)]
    if mk.ndim == 4:
        kvm4 = lambda b, j: kvm(b, j) + (0,)
        in_specs += [pl.BlockSpec((nb, N_MEM, X_H, X_HD), kvm4)] * 2
    else:
        in_specs += [pl.BlockSpec((nb, N_MEM, X_W), kvm)] * 2
    in_specs += [pl.BlockSpec(w[k].shape, full) for k in wnames] + [pl.BlockSpec((tm, tm), full), pl.BlockSpec((128, 128), full)]
    x2, rows, info, nch = pl.pallas_call(
        kern,
        out_shape=[jax.ShapeDtypeStruct((n, D_MODEL), F32), jax.ShapeDtypeStruct((ntile * lr, D_MODEL), BF16),
                   jax.ShapeDtypeStruct((n, 128), F32), jax.ShapeDtypeStruct((ntile * 8, 128), F32)],
        grid=grid, in_specs=in_specs,
        out_specs=[pl.BlockSpec((tm, D_MODEL), row), pl.BlockSpec((lr, D_MODEL), tile), pl.BlockSpec((tm, 128), row),
                   pl.BlockSpec((8, 128), tile)],
        scratch_shapes=scratch, compiler_params=_cparams(2), name="post")(*ins)
    return x2, rows, info, nch, tm


def _expert_kernel(sc_ref, be_ref, nu_ref, *refs, dummy, splits):
    nsrc = len(splits) - 1
    rows_hbm = refs[:nsrc]
    wg_ref, wu_ref, wd_ref, _, y_hbm, xbuf, ybuf, gsem, ssem = refs[nsrc:]
    j = pl.program_id(0)
    nu = nu_ref[0]
    slot = j % 2
    cpb = MOE_BLOCK // MOE_CHUNK

    def gather_start(blk, s):
        for c in range(cpb):
            chunk = sc_ref[blk * cpb + c]
            dst = xbuf.at[s, pl.ds(c * MOE_CHUNK, MOE_CHUNK)]
            for i in range(nsrc):
                def issue(i=i):
                    src = pl.multiple_of((chunk - splits[i]) * MOE_CHUNK, MOE_CHUNK)
                    pltpu.make_async_copy(rows_hbm[i].at[pl.ds(src, MOE_CHUNK)], dst, gsem.at[s]).start()
                if nsrc == 1:
                    issue()
                else:
                    pl.when(jnp.logical_and(chunk >= splits[i], chunk < splits[i + 1]))(issue)

    def gather_wait(s):
        for c in range(cpb):
            pltpu.make_async_copy(rows_hbm[0].at[pl.ds(0, MOE_CHUNK)], xbuf.at[s, pl.ds(c * MOE_CHUNK, MOE_CHUNK)],
                                  gsem.at[s]).wait()

    def writeback(blk, s, act):
        for c in range(cpb):
            chunk = sc_ref[blk * cpb + c]

            @pl.when(chunk != dummy)
            def _():
                dst = pl.multiple_of(chunk * MOE_CHUNK, MOE_CHUNK)
                act(pltpu.make_async_copy(ybuf.at[s, pl.ds(c * MOE_CHUNK, MOE_CHUNK)],
                                          y_hbm.at[pl.ds(dst, MOE_CHUNK)], ssem.at[s]))

    @pl.when(j < nu)
    def _():
        pl.when(j == 0)(lambda: gather_start(0, 0))
        pl.when(j + 1 < nu)(lambda: gather_start(j + 1, 1 - slot))
        gather_wait(slot)

        @pl.when(j >= 2)
        def _():
            writeback(j - 2, slot, lambda cp: cp.wait())

        xb = xbuf[slot]
        hid = _silu(_dot(xb, wg_ref[...])) * _dot(xb, wu_ref[...])
        ybuf[slot] = _dot(hid.astype(BF16), wd_ref[...]).astype(BF16)
        writeback(j, slot, lambda cp: cp.start())

        @pl.when(j == nu - 1)
        def _():
            writeback(j, slot, lambda cp: cp.wait())

            @pl.when(j >= 1)
            def _():
                writeback(j - 1, 1 - slot, lambda cp: cp.wait())


def _experts(rows_list, src_chunk, block_e, n_used, dummy, w):
    n_rows = sum(r.shape[0] for r in rows_list)
    splits = tuple(int(v) for v in np.cumsum([0] + [r.shape[0] // MOE_CHUNK for r in rows_list]))
    nsrc = len(rows_list)
    nblk = block_e.shape[0]
    y_init = jnp.zeros((n_rows, D_MODEL), BF16)
    wmap = lambda j, sc, be, nu: (be[j], 0, 0)
    grid_spec = pltpu.PrefetchScalarGridSpec(
        num_scalar_prefetch=3, grid=(nblk,),
        in_specs=[pl.BlockSpec(memory_space=pl.ANY)] * nsrc +
                 [pl.BlockSpec((None, D_MODEL, D_EXPERT), wmap), pl.BlockSpec((None, D_MODEL, D_EXPERT), wmap),
                  pl.BlockSpec((None, D_EXPERT, D_MODEL), wmap),
                  pl.BlockSpec(memory_space=pl.ANY)],
        out_specs=pl.BlockSpec(memory_space=pl.ANY),
        scratch_shapes=[pltpu.VMEM((2, MOE_BLOCK, D_MODEL), BF16), pltpu.VMEM((2, MOE_BLOCK, D_MODEL), BF16),
                        pltpu.SemaphoreType.DMA((2,)), pltpu.SemaphoreType.DMA((2,))])
    return pl.pallas_call(functools.partial(_expert_kernel, dummy=dummy, splits=splits),
                          out_shape=jax.ShapeDtypeStruct((n_rows, D_MODEL), BF16), grid_spec=grid_spec,
                          input_output_aliases={6 + nsrc: 0}, compiler_params=_cparams(1), name="moe_experts")(
        src_chunk, block_e, n_used, *rows_list, w["w_eg"], w["w_eu"], w["w_ed"], y_init)


def _combine_kernel(x_ref, info_ref, y_ref, g_ref, o_ref, *, final):
    tm, lr = x_ref.shape[0], y_ref.shape[0]
    info = info_ref[...]
    col = lax.broadcasted_iota(I32, (tm, lr), 1).astype(F32)
    gmat = jnp.where(col == info[:, 4:5], info[:, 2:3], 0.0) + jnp.where(col == info[:, 5:6], info[:, 3:4], 0.0)
    y = x_ref[...] + _dot(gmat.astype(BF16), y_ref[...])
    if final:
        y = _rms(y, g_ref[...])
    o_ref[...] = y


def _combine(x2, info, y_loc, tm, g_final, final, tile0=0):
    n = x2.shape[0]
    lr = _local_rows(tm)
    return pl.pallas_call(
        functools.partial(_combine_kernel, final=final),
        out_shape=jax.ShapeDtypeStruct((n, D_MODEL), F32), grid=(n // tm,),
        in_specs=[pl.BlockSpec((tm, D_MODEL), lambda i: (i, 0)), pl.BlockSpec((tm, 128), lambda i: (i, 0)),
                  pl.BlockSpec((lr, D_MODEL), lambda i: (tile0 + i, 0)), pl.BlockSpec((1, D_MODEL), lambda i: (0, 0))],
        out_specs=pl.BlockSpec((tm, D_MODEL), lambda i: (i, 0)),
        compiler_params=_cparams(1), name="moe_combine")(x2, info, y_loc, g_final)


def _moe_experts(rows_list, nch, tm, w):
    ntile = sum(r.shape[0] for r in rows_list) // _local_rows(tm)
    lrc = _local_rows(tm) // MOE_CHUNK
    cpb = MOE_BLOCK // MOE_CHUNK
    dummy = lrc - 1
    nc = nch.reshape(ntile, 8, 128)[:, 0, EXPERT_LANE0:EXPERT_LANE0 + N_EXPERTS].astype(I32)
    src_base = jnp.arange(ntile, dtype=I32)[:, None] * lrc + (jnp.cumsum(nc, axis=1) - nc)
    pe = (jnp.sum(nc, axis=0) + cpb - 1) // cpb * cpb
    pend = jnp.cumsum(pe)
    dst = (pend - pe)[None, :] + (jnp.cumsum(nc, axis=0) - nc)
    dst_f, nc_f, src_f = dst.T.reshape(-1), nc.T.reshape(-1), src_base.T.reshape(-1)
    max_chunks = ntile * ((TOP_K * tm + N_EXPERTS * (MOE_CHUNK - 1)) // MOE_CHUNK) + N_EXPERTS * (cpb - 1)
    nblk = -(-max_chunks // cpb)
    pos = jnp.arange(nblk * cpb, dtype=I32)
    run = jnp.sum((dst_f[None, :] <= pos[:, None]).astype(I32), axis=1) - 1
    at_run = jnp.stack([dst_f, nc_f, src_f], axis=1)[run]
    rel = pos - at_run[:, 0]
    src_chunk = jnp.where(rel < at_run[:, 1], at_run[:, 2] + rel, dummy).astype(I32)
    blk0 = jnp.arange(nblk, dtype=I32) * cpb
    block_e = jnp.minimum(jnp.sum((pend[None, :] <= blk0[:, None]).astype(I32), axis=1), N_EXPERTS - 1).astype(I32)
    n_used = (pend[-1:] // cpb).astype(I32)
    return _experts(rows_list, src_chunk, block_e, n_used, dummy, w)


def _rope_tables(pos):
    half = ROPE_DIM // 2
    inv = ROPE_BASE ** (-jnp.arange(half, dtype=F32) * 2.0 / ROPE_DIM)
    ang = pos.astype(F32)[:, None] * inv[None, :]
    c, s = jnp.cos(ang), jnp.sin(ang)
    cos32 = jnp.concatenate([c, c], axis=1)
    sin32 = jnp.concatenate([-s, s], axis=1)
    return jnp.tile(cos32, (1, MLA_H)), jnp.tile(sin32, (1, MLA_H))


def _prep_layer(l, p):
    w_in = p["w_in"][l]
    zeros80 = jnp.zeros((D_MODEL, 80), F32)
    w_in_p = jnp.concatenate([w_in[:, 0:640], w_in[:, 640:672], w_in[:, 1184:1200], zeros80,
                              w_in[:, 672:1184], w_in[:, 1200:1456]], axis=1).astype(BF16)
    assert w_in_p.shape[1] == IN_PAD
    w_uq = p["w_uq"][l].reshape(Q_LORA, MLA_H, NOPE + ROPE_DIM)
    w_uq_p = jnp.concatenate([w_uq[:, :, :NOPE].reshape(Q_LORA, MLA_H * NOPE),
                              w_uq[:, :, NOPE:].reshape(Q_LORA, MLA_H * ROPE_DIM)], axis=1).astype(BF16)
    eye_h = jnp.eye(MLA_H, dtype=F32)
    w_ukbd = jnp.einsum("hnc,hg->hngc", p["w_uk"][l].transpose(1, 2, 0), eye_h).reshape(MLA_H * NOPE, MLA_H * KV_LORA).astype(BF16)
    w_uvbd = jnp.einsum("hcv,hg->hcgv", p["w_uv"][l].transpose(1, 0, 2), eye_h).reshape(MLA_H * KV_LORA, MLA_H * V_DIM).astype(BF16)
    w_gk2 = jnp.zeros((128, 128), F32).at[ROPE_DIM:ROPE_DIM + GATE_RANK, :].set(p["w_gk2"][l]).astype(BF16)
    w_poolbd = jnp.einsum("gcd,gk->gckd", p["w_pool"][l], jnp.eye(POOL_GROUPS, dtype=F32)).reshape(POOL_W, POOL_W).astype(BF16)
    w_out = p["w_out"][l].astype(BF16)
    w_r = jnp.zeros((D_MODEL, 128), F32).at[:, 0:N_GROUPS].set(p["w_rg"][l]).at[:, EXPERT_LANE0:EXPERT_LANE0 + N_EXPERTS].set(p["w_re"][l])
    w_r_hi = w_r.astype(BF16)
    b_r = jnp.zeros((1, 128), F32).at[0, 0:N_GROUPS].set(p["b_rg"][l]).at[0, EXPERT_LANE0:EXPERT_LANE0 + N_EXPERTS].set(p["b_re"][l])
    return {
        "g_mix": p["g_mix"][l][None, :], "w_in": w_in_p, "g_qn": p["g_qn"][l][None, :], "w_uq": w_uq_p, "w_ukbd": w_ukbd,
        "g_kvn": p["g_kvn"][l][None, :], "w_gk2": w_gk2, "b_gk": p["b_gk"][l][None, :],
        "w_poolbd": w_poolbd, "pool_scale": p["pool_scale"][l][None, :],
        "g_gla": jnp.tile(p["g_gla"][l], GLA_H)[None, :],
        "w_uvbd": w_uvbd, "wo_a": w_out[0:256], "wo_b": w_out[256:768], "wo_c": w_out[768:1024],
        "g_x": p["g_x"][l][None, :], "w_xq": p["w_xq"][l].astype(BF16), "w_xo": p["w_xo"][l].astype(BF16),
        "g_mem": p["g_mem"][l][None, :], "w_xk": p["w_xk"][l].astype(BF16), "w_xv": p["w_xv"][l].astype(BF16),
        "g_ffn": p["g_ffn"][l][None, :], "w_r_hi": w_r_hi, "w_r_lo": (w_r - w_r_hi.astype(F32)).astype(BF16), "b_r": b_r,
        "w_eg": p["w_eg"][l].astype(BF16), "w_eu": p["w_eu"][l].astype(BF16), "w_ed": p["w_ed"][l].astype(BF16),
    }


GLA_CHUNK = 64


def kernel(x_prompt, x_sample, mem_prompt, cache_ckv, cache_kpe, page_table, state_pool, state_gla, cache_mem_k, cache_mem_v, g_mix, w_in, w_pool, pool_scale, g_qn, w_uq, g_kvn, w_uk, w_uv, w_gk2, b_gk, g_gla, w_out, g_x, g_mem, w_xq, w_xk, w_xv, w_xo, g_ffn, w_rg, b_rg, w_re, b_re, w_eg, w_eu, w_ed, g_final):
    params = dict(g_mix=g_mix, w_in=w_in, w_pool=w_pool, pool_scale=pool_scale, g_qn=g_qn, w_uq=w_uq, g_kvn=g_kvn, w_uk=w_uk,
                  w_uv=w_uv, w_gk2=w_gk2, b_gk=b_gk, g_gla=g_gla, w_out=w_out, g_x=g_x, g_mem=g_mem, w_xq=w_xq, w_xk=w_xk,
                  w_xv=w_xv, w_xo=w_xo, g_ffn=g_ffn, w_rg=w_rg, b_rg=b_rg, w_re=w_re, b_re=b_re, w_eg=w_eg, w_eu=w_eu, w_ed=w_ed)
    depth = w_in.shape[0]
    bsz, seq, _ = x_prompt.shape
    dbsz, dseq, _ = x_sample.shape
    past_len = page_table.shape[1] * cache_ckv.shape[2]
    gfin = g_final[None, :]
    cache_kpe_t = jnp.swapaxes(cache_kpe, 2, 3)

    cos_p, sin_p = _rope_tables(jnp.arange(seq, dtype=I32))
    cos_s, sin_s = _rope_tables(past_len + (jnp.arange(dbsz * dseq, dtype=I32) % dseq))

    xp = x_prompt.reshape(bsz * seq, D_MODEL)
    xs = x_sample.reshape(dbsz * dseq, D_MODEL)
    mem = mem_prompt.reshape(bsz * N_MEM, D_MODEL)
    gpad = 8 - dseq
    assert 0 <= gpad < 8
    assert past_len >= max(POOL_WINDOWS) - 1

    share_moe = _pick(seq, (512, 256, 128, 64, 32, 16)) == dbsz * dseq if seq >= 16 else False
    outs = {k: [] for k in ("ckv_p", "kpe_p", "pool_p", "gla_p", "mk_p", "mv_p", "ckv_s", "kpe_s", "pool_s", "gla_s")}
    for l in range(depth):
        w = _prep_layer(l, params)
        final = l == depth - 1
        mk, mv = _memkv(mem, w)
        u, q_hm, kvb, ckv, kpe, gq, gk, gv, go, glog = _inproj(xp, cos_p, sin_p, w)
        ya, pool_new = _pool_prompt(u, bsz, w)
        olat = _mla_prompt(q_hm, kvb, bsz)
        yc, s_new = _gla(gq, gk, gv, go, glog, None, bsz, w, GLA_CHUNK)
        x2p, rows_p, info_p, nch_p, tm_p = _post(xp, ya, olat, yc, mk.reshape(bsz, N_MEM, X_W), mv.reshape(bsz, N_MEM, X_W),
                                                 bsz, w)
        if not share_moe:
            xp = _combine(x2p, info_p, _moe_experts([rows_p], nch_p, tm_p, w), tm_p, gfin, final)
        outs["ckv_p"].append(ckv.reshape(bsz, seq, KV_LORA))
        outs["kpe_p"].append(kpe.reshape(bsz, seq, ROPE_DIM))
        outs["pool_p"].append(pool_new)
        outs["gla_p"].append(s_new.reshape(bsz, GLA_H, GLA_DK, GLA_DV))
        outs["mk_p"].append(mk.reshape(bsz, N_MEM, X_H, X_HD))
        outs["mv_p"].append(mv.reshape(bsz, N_MEM, X_H, X_HD))
        u, q_hm, kvb, ckv, kpe, gq, gk, gv, go, glog = _inproj(xs, cos_s, sin_s, w)
        ue_tm = jnp.concatenate([state_pool[l], u.reshape(dbsz, dseq, POOL_W)], axis=1).transpose(1, 0, 2)
        ya_tm, st_tm = _pool_sample(ue_tm, w)
        ya = ya_tm.transpose(1, 0, 2).reshape(dbsz * dseq, POOL_W)
        q_s = q_hm.reshape(MLA_H, dbsz, dseq, QK_PAD).transpose(1, 0, 2, 3).reshape(dbsz, MLA_H * dseq, QK_PAD)
        knew = jnp.pad(kvb.reshape(dbsz, dseq, QK_PAD), ((0, 0), (0, 16 - dseq), (0, 0)))
        o_s = _mla_decode(q_s, knew, cache_ckv, cache_kpe_t, page_table, l)
        olat = o_s.reshape(dbsz, MLA_H, dseq, KV_LORA).transpose(0, 2, 1, 3).reshape(dbsz * dseq, MLA_H * KV_LORA).astype(BF16)

        def pad8(a):
            return jnp.pad(a.reshape(dbsz, dseq, -1), ((0, 0), (0, gpad), (0, 0))).reshape(dbsz * 8, -1)

        yc8, s_new = _gla(pad8(gq), pad8(gk), pad8(gv), pad8(go), pad8(glog),
                          state_gla[l].reshape(dbsz, GLA_H * GLA_DK, GLA_DV), dbsz, w, 8)
        yc = yc8.reshape(dbsz, 8, GLA_W)[:, :dseq].reshape(dbsz * dseq, GLA_W)
        mem_k4 = cache_mem_k.reshape(depth * dbsz, N_MEM, X_H, X_HD)
        mem_v4 = cache_mem_v.reshape(depth * dbsz, N_MEM, X_H, X_HD)
        x2s, rows_s, info_s, nch_s, tm_s = _post(xs, ya, olat, yc, mem_k4, mem_v4, dbsz, w, kv_seq0=l * dbsz)
        if share_moe:
            ntile_p = (bsz * seq) // tm_p
            y_loc = _moe_experts([rows_p, rows_s], jnp.concatenate([nch_p, nch_s], axis=0), tm_p, w)
            xp = _combine(x2p, info_p, y_loc, tm_p, gfin, final)
            xs = _combine(x2s, info_s, y_loc, tm_p, gfin, final, tile0=ntile_p)
        else:
            xs = _combine(x2s, info_s, _moe_experts([rows_s], nch_s, tm_s, w), tm_s, gfin, final)
        outs["ckv_s"].append(ckv.reshape(dbsz, dseq, KV_LORA))
        outs["kpe_s"].append(kpe.reshape(dbsz, dseq, ROPE_DIM))
        outs["pool_s"].append(st_tm.transpose(1, 0, 2))
        outs["gla_s"].append(s_new.reshape(dbsz, GLA_H, GLA_DK, GLA_DV))

    st = lambda k: jnp.stack(outs[k])
    return (xp.reshape(bsz, seq, D_MODEL), xs.reshape(dbsz, dseq, D_MODEL),
            st("ckv_p"), st("kpe_p"), st("pool_p"), st("gla_p"), st("mk_p"), st("mv_p"),
            st("ckv_s"), st("kpe_s"), st("pool_s"), st("gla_s"))
```

```python
import functools

import numpy as np
import jax
import jax.numpy as jnp
from jax import lax
from jax.experimental import pallas as pl
from jax.experimental.pallas import tpu as pltpu

F32 = jnp.float32
BF16 = jnp.bfloat16
I32 = jnp.int32

EPS = 1e-6
D_MODEL = 1024
POOL_GROUPS, POOL_GC = 4, 64
POOL_W = POOL_GROUPS * POOL_GC
POOL_WINDOWS = (2, 4, 8, 16)
POOL_BUF = 15
MLA_H, Q_LORA, KV_LORA, NOPE, ROPE_DIM, V_DIM = 8, 256, 128, 64, 32, 64
ROPE_BASE = 10000.0
MLA_SCALE = (NOPE + ROPE_DIM) ** -0.5
Q_PRESCALE = MLA_SCALE * 1.4426950408889634
GLA_H, GLA_DK, GLA_DV, GATE_RANK, GATE_TAU = 4, 32, 64, 16, 16.0
GLA_W = GLA_H * GLA_DV
N_MEM, X_H, X_HD = 256, 4, 128
X_W = X_H * X_HD
N_GROUPS, EXP_PER_GROUP, N_EXPERTS, TOP_K, D_EXPERT = 4, 8, 32, 2, 256
QK_PAD = 256
ONES_LANE = KV_LORA + ROPE_DIM
DECODE_SLOTS = 3
MLA_HEAD_GROUPS = 4
IN_PAD = 1536
EXPERT_LANE0 = 32
MOE_BLOCK = 512
MOE_CHUNK = 16
NEG = -1e30
VMEM_LIMIT = 56 * 1024 * 1024


def _pick(n, prefs):
    for p in prefs:
        if n % p == 0:
            return p
    raise ValueError(f"no tile in {prefs} divides {n}")


def _cparams(n_axes, **flags):
    return pltpu.CompilerParams(dimension_semantics=("arbitrary",) * n_axes, vmem_limit_bytes=VMEM_LIMIT,
                                flags=flags or None)


def _rms(x, g):
    ms = jnp.mean(x * x, axis=-1, keepdims=True)
    return x * lax.rsqrt(ms + EPS) * g


def _dot(a, b):
    return jnp.dot(a, b, preferred_element_type=F32)


def _dot_nt(a, b):
    return lax.dot_general(a, b, (((1,), (1,)), ((), ())), preferred_element_type=F32)


def _dot_tn(a, b):
    return lax.dot_general(a, b, (((0,), (0,)), ((), ())), preferred_element_type=F32)


def _split_bf16(x):
    hi = x.astype(BF16)
    lo = (x - hi.astype(F32)).astype(BF16)
    return hi, lo


def _swap16(x):
    w = x.shape[-1]
    lane = lax.broadcasted_iota(I32, x.shape, x.ndim - 1)
    first = (lane & 31) < 16
    return jnp.where(first, pltpu.roll(x, w - 16, x.ndim - 1), pltpu.roll(x, 16, x.ndim - 1))


def _silu(x):
    return x / (1.0 + jnp.exp(-x))


def _inproj_kernel(x_ref, gmix_ref, win_ref, cos_ref, sin_ref, gqn_ref, wuq_ref, wuk_ref, gkvn_ref, wgk_ref, bgk_ref,
                   u_ref, q_ref, kvb_ref, ckv_ref, kpe_ref, gq_ref, gk_ref, gv_ref, go_ref, glog_ref):
    h = _rms(x_ref[...], gmix_ref[...]).astype(BF16)
    y = _dot(h, win_ref[...])
    u_ref[...] = y[:, 0:256]
    gq_ref[...] = y[:, 768:896]
    gk_ref[...] = y[:, 896:1024]
    gv_ref[...] = y[:, 1024:1280]
    go_ref[...] = y[:, 1280:1536]
    cos = cos_ref[...]
    sin = sin_ref[...]
    ckv = _rms(y[:, 512:640], gkvn_ref[...])
    grp = y[:, 640:768]
    grp_r = grp * cos[:, 0:128] + _swap16(grp) * sin[:, 0:128]
    ckv_ref[...] = ckv
    kpe_ref[...] = grp_r[:, 0:ROPE_DIM]
    lane = lax.broadcasted_iota(I32, grp_r.shape, 1)
    kvb_ref[:, 0:128] = ckv.astype(BF16)
    kvb_ref[:, 128:256] = jnp.where(lane < ROPE_DIM, grp_r, jnp.where(lane == ROPE_DIM, 1.0, 0.0)).astype(BF16)
    gl = _dot(grp.astype(BF16), wgk_ref[...]) + bgk_ref[...]
    glog_ref[...] = (jnp.minimum(gl, 0.0) - jnp.log(1.0 + jnp.exp(-jnp.abs(gl)))) * (1.0 / GATE_TAU)
    cqn = _rms(y[:, 256:512], gqn_ref[...]).astype(BF16)
    q = _dot(cqn, wuq_ref[...])
    qlat = _dot(q[:, 0:512].astype(BF16), wuk_ref[...]) * Q_PRESCALE
    qr = q[:, 512:768]
    qr = (qr * cos + _swap16(qr) * sin) * Q_PRESCALE
    for hh in range(MLA_H):
        q_ref[hh, :, 0:128] = qlat[:, 128 * hh:128 * hh + 128].astype(BF16)
        col = qr[:, 128 * (hh // 4):128 * (hh // 4) + 128]
        sh = (128 - 32 * (hh % 4)) % 128
        if sh:
            col = pltpu.roll(col, sh, 1)
        q_ref[hh, :, 128:256] = jnp.where(lane < ROPE_DIM, col, 0.0).astype(BF16)


def _inproj(x, cos_t, sin_t, w):
    n = x.shape[0]
    tm = _pick(n, (512, 256, 128, 64, 32, 16))
    tab_tiles = cos_t.shape[0] // tm
    assert cos_t.shape[0] % tm == 0
    row = lambda i: (i, 0)
    tab = lambda i: (i % tab_tiles, 0)
    full = lambda i: (0, 0)
    wspec = lambda a: pl.BlockSpec(a.shape, full)
    outs = [
        jax.ShapeDtypeStruct((n, 256), F32),
        jax.ShapeDtypeStruct((MLA_H, n, QK_PAD), BF16),
        jax.ShapeDtypeStruct((n, QK_PAD), BF16),
        jax.ShapeDtypeStruct((n, KV_LORA), F32),
        jax.ShapeDtypeStruct((n, ROPE_DIM), F32),
        jax.ShapeDtypeStruct((n, 128), F32),
        jax.ShapeDtypeStruct((n, 128), F32),
        jax.ShapeDtypeStruct((n, 256), F32),
        jax.ShapeDtypeStruct((n, 256), F32),
        jax.ShapeDtypeStruct((n, 128), F32),
    ]
    out_specs = [
        pl.BlockSpec((tm, 256), row),
        pl.BlockSpec((MLA_H, tm, QK_PAD), lambda i: (0, i, 0)),
        pl.BlockSpec((tm, QK_PAD), row),
        pl.BlockSpec((tm, KV_LORA), row),
        pl.BlockSpec((tm, ROPE_DIM), row),
        pl.BlockSpec((tm, 128), row),
        pl.BlockSpec((tm, 128), row),
        pl.BlockSpec((tm, 256), row),
        pl.BlockSpec((tm, 256), row),
        pl.BlockSpec((tm, 128), row),
    ]
    ins = [x, w["g_mix"], w["w_in"], cos_t, sin_t, w["g_qn"], w["w_uq"], w["w_ukbd"], w["g_kvn"], w["w_gk2"], w["b_gk"]]
    in_specs = [pl.BlockSpec((tm, D_MODEL), row), wspec(w["g_mix"]), wspec(w["w_in"]),
                pl.BlockSpec((tm, 256), tab), pl.BlockSpec((tm, 256), tab),
                wspec(w["g_qn"]), wspec(w["w_uq"]), wspec(w["w_ukbd"]), wspec(w["g_kvn"]), wspec(w["w_gk2"]), wspec(w["b_gk"])]
    return pl.pallas_call(_inproj_kernel, out_shape=outs, grid=(n // tm,), in_specs=in_specs, out_specs=out_specs,
                          compiler_params=_cparams(1), name="inproj")(*ins)


def _pool_mix(sums, u, cnts, wbd, scale):
    cols = []
    for c in range(2):
        wa, wb = POOL_WINDOWS[2 * c], POOL_WINDOWS[2 * c + 1]
        sa, sb = sums[(c, wa)], sums[(c, wb)]
        lane = lax.broadcasted_iota(I32, sa.shape, 1)
        pooled = jnp.where(lane < POOL_GC, sa / cnts[wa], sb / cnts[wb]) - u[:, 128 * c:128 * c + 128]
        cols.append(pooled)
    pooled = jnp.concatenate(cols, axis=1).astype(BF16)
    return (_dot(pooled, wbd) * scale).astype(BF16)


def _pool_kernel(u_ref, wbd_ref, scale_ref, ya_ref, st_ref, ue_ref, *, tp, nt):
    j = pl.program_id(1)

    @pl.when(j == 0)
    def _():
        ue_ref[0:16, :] = jnp.zeros((16, POOL_W), F32)

    @pl.when(j > 0)
    def _():
        ue_ref[0:16, :] = ue_ref[tp:tp + 16, :]

    u = u_ref[...]
    ue_ref[16:16 + tp, :] = u
    t = j * tp + lax.broadcasted_iota(I32, (tp, 1), 0)
    cnts = {w: jnp.minimum(t + 1, w).astype(F32) for w in POOL_WINDOWS}
    sums = {}
    for c in range(2):
        wa, wb = POOL_WINDOWS[2 * c], POOL_WINDOWS[2 * c + 1]
        acc = None
        for k in range(wb):
            sl = ue_ref[16 - k:16 - k + tp, 128 * c:128 * c + 128]
            acc = sl if acc is None else acc + sl
            if k + 1 == wa:
                sums[(c, wa)] = acc
        sums[(c, wb)] = acc
    ya_ref[...] = _pool_mix(sums, u, cnts, wbd_ref[...], scale_ref[...])

    @pl.when(j == nt - 1)
    def _():
        st_ref[0] = ue_ref[tp + 1:tp + 16, :]


def _pool_prompt(u, bsz, w):
    n = u.shape[0]
    t = n // bsz
    tp = _pick(t, (512, 256, 128, 64, 32, 16))
    nt = t // tp
    kern = functools.partial(_pool_kernel, tp=tp, nt=nt)
    return pl.pallas_call(
        kern,
        out_shape=[jax.ShapeDtypeStruct((n, POOL_W), BF16), jax.ShapeDtypeStruct((bsz, POOL_BUF, POOL_W), F32)],
        grid=(bsz, nt),
        in_specs=[pl.BlockSpec((tp, POOL_W), lambda b, j: (b * nt + j, 0)),
                  pl.BlockSpec((POOL_W, POOL_W), lambda b, j: (0, 0)),
                  pl.BlockSpec((1, POOL_W), lambda b, j: (0, 0))],
        out_specs=[pl.BlockSpec((tp, POOL_W), lambda b, j: (b * nt + j, 0)),
                   pl.BlockSpec((1, POOL_BUF, POOL_W), lambda b, j: (b, 0, 0))],
        scratch_shapes=[pltpu.VMEM((tp + 16, POOL_W), F32)],
        compiler_params=_cparams(2), name="pool_prompt")(u, w["w_poolbd"], w["pool_scale"])


def _pool_step_kernel(ue_ref, wbd_ref, scale_ref, ya_ref, st_ref, *, tt):
    cnts = {w: jnp.float32(w) for w in POOL_WINDOWS}
    for t in range(tt):
        sums = {}
        for c in range(2):
            wa, wb = POOL_WINDOWS[2 * c], POOL_WINDOWS[2 * c + 1]
            acc = None
            for k in range(wb):
                sl = ue_ref[POOL_BUF + t - k, :, 128 * c:128 * c + 128]
                acc = sl if acc is None else acc + sl
                if k + 1 == wa:
                    sums[(c, wa)] = acc
            sums[(c, wb)] = acc
        ya_ref[t] = _pool_mix(sums, ue_ref[POOL_BUF + t], cnts, wbd_ref[...], scale_ref[...])
    for r in range(POOL_BUF):
        st_ref[r] = ue_ref[tt + r]


def _pool_sample(ue_tm, w):
    rows, bsz, _ = ue_tm.shape
    tt = rows - POOL_BUF
    kern = functools.partial(_pool_step_kernel, tt=tt)
    full3 = lambda i: (0, 0, 0)
    return pl.pallas_call(
        kern,
        out_shape=[jax.ShapeDtypeStruct((tt, bsz, POOL_W), BF16), jax.ShapeDtypeStruct((POOL_BUF, bsz, POOL_W), F32)],
        grid=(1,),
        in_specs=[pl.BlockSpec(ue_tm.shape, full3), pl.BlockSpec((POOL_W, POOL_W), lambda i: (0, 0)),
                  pl.BlockSpec((1, POOL_W), lambda i: (0, 0))],
        out_specs=[pl.BlockSpec((tt, bsz, POOL_W), full3), pl.BlockSpec((POOL_BUF, bsz, POOL_W), full3)],
        compiler_params=_cparams(1), name="pool_sample")(ue_tm, w["w_poolbd"], w["pool_scale"])


def _mla_prompt_kernel(qi_ref, kj_ref, q_ref, k_ref, o_ref, m_ref, acc_ref, *, tq, tk):
    p = pl.program_id(1)
    qi = qi_ref[p]
    kj = kj_ref[p]
    rows = MLA_H * tq

    @pl.when(kj == 0)
    def _():
        m_ref[...] = jnp.full((rows, 128), NEG, F32)
        acc_ref[...] = jnp.zeros((rows, QK_PAD), F32)

    def step(masked, nk):
        k = k_ref[0:nk, :]
        hg = MLA_H // MLA_HEAD_GROUPS
        gr = hg * tq
        for g in range(MLA_HEAD_GROUPS):
            rs = slice(g * gr, (g + 1) * gr)
            q = q_ref[g * hg:(g + 1) * hg].reshape(gr, QK_PAD)
            s = _dot_nt(q, k)
            if masked:
                r = lax.broadcasted_iota(I32, (gr, nk), 0)
                c = lax.broadcasted_iota(I32, (gr, nk), 1)
                qpos = qi * tq + (r & (tq - 1))
                s = jnp.where(kj * tk + c <= qpos, s, NEG)
            m_old = m_ref[rs, :]
            m_new = jnp.maximum(m_old, jnp.max(s, axis=-1, keepdims=True))
            alpha = jnp.exp2(m_old - m_new)
            pr = jnp.exp2(s - jnp.concatenate([m_new] * (nk // 128), axis=1))
            acc_ref[rs, :] = jnp.concatenate([alpha, alpha], axis=1) * acc_ref[rs, :] + _dot(pr.astype(BF16), k)
            m_ref[rs, :] = m_new

    visible = (qi + 1) * tq - kj * tk
    half = tk // 2 if (tk % 256 == 0 and tq <= tk // 2) else tk
    crosses = visible < tk + tq

    if half < tk:
        @pl.when(jnp.logical_and(crosses, visible <= half))
        def _():
            step(True, half)

    @pl.when(jnp.logical_and(crosses, visible > half) if half < tk else crosses)
    def _():
        step(True, tk)

    @pl.when(jnp.logical_not(crosses))
    def _():
        step(False, tk)

    @pl.when(kj == ((qi + 1) * tq - 1) // tk)
    def _():
        acc = acc_ref[...]
        o = acc[:, 0:KV_LORA] / acc[:, ONES_LANE:ONES_LANE + 1]
        for h in range(MLA_H):
            o_ref[:, KV_LORA * h:KV_LORA * (h + 1)] = o[h * tq:(h + 1) * tq].astype(BF16)


def _mla_prompt(q_hm, kvb, bsz):
    n = kvb.shape[0]
    t = n // bsz
    tq = _pick(t, (512, 256, 128, 64, 32, 16))
    tk = _pick(t, (512, 256, 128, 64, 32, 16))
    assert tq & (tq - 1) == 0
    nq, nk = t // tq, t // tk
    pairs = [(i, j) for i in range(nq) for j in range(((i + 1) * tq - 1) // tk + 1)]
    qi = jnp.asarray(np.array([p[0] for p in pairs], np.int32))
    kj = jnp.asarray(np.array([p[1] for p in pairs], np.int32))
    rows = MLA_H * tq
    kern = functools.partial(_mla_prompt_kernel, tq=tq, tk=tk)
    grid_spec = pltpu.PrefetchScalarGridSpec(
        num_scalar_prefetch=2, grid=(bsz, len(pairs)),
        in_specs=[pl.BlockSpec((MLA_H, tq, QK_PAD), lambda b, p, qi, kj: (0, b * nq + qi[p], 0)),
                  pl.BlockSpec((tk, QK_PAD), lambda b, p, qi, kj: (b * nk + kj[p], 0))],
        out_specs=pl.BlockSpec((tq, MLA_H * KV_LORA), lambda b, p, qi, kj: (b * nq + qi[p], 0)),
        scratch_shapes=[pltpu.VMEM((rows, 128), F32), pltpu.VMEM((rows, QK_PAD), F32)])
    return pl.pallas_call(kern, out_shape=jax.ShapeDtypeStruct((n, MLA_H * KV_LORA), BF16), grid_spec=grid_spec,
                          compiler_params=_cparams(2), name="mla_prompt")(qi, kj, q_hm, kvb)


def _mla_decode_kernel(pt_ref, q_ref, knew_ref, cckv_ref, ckpe_ref, o_ref,
                       cbuf, pbuf, csem, psem, m_ref, l_ref, acc_ref, *, layer, ng, gp, page, tt, total):
    s_idx = pl.program_id(0)
    grp = s_idx % ng
    slot = s_idx % DECODE_SLOTS

    def copies(step, slot_):
        out = []
        for g in range(gp):
            pg = pt_ref[step * gp + g]
            out.append(pltpu.make_async_copy(cckv_ref.at[layer, pg], cbuf.at[slot_, pl.ds(g * page, page)], csem.at[slot_]))
            out.append(pltpu.make_async_copy(ckpe_ref.at[layer, pg], pbuf.at[slot_, :, pl.ds(g * page, page)], psem.at[slot_]))
        return out

    @pl.when(s_idx == 0)
    def _():
        for ahead in range(min(DECODE_SLOTS - 1, total)):
            for c in copies(ahead, ahead):
                c.start()

    @pl.when(s_idx + (DECODE_SLOTS - 1) < total)
    def _():
        for c in copies(s_idx + (DECODE_SLOTS - 1), (s_idx + (DECODE_SLOTS - 1)) % DECODE_SLOTS):
            c.start()

    @pl.when(grp == 0)
    def _():
        m_ref[...] = jnp.full(m_ref.shape, NEG, F32)
        l_ref[...] = jnp.zeros(l_ref.shape, F32)
        acc_ref[...] = jnp.zeros(acc_ref.shape, F32)

    for c in copies(s_idx, slot):
        c.wait()

    q = q_ref[0]

    def update(s, v):
        m_old = m_ref[...]
        m_new = jnp.maximum(m_old, jnp.max(s, axis=-1, keepdims=True))
        alpha = jnp.exp2(m_old - m_new)
        pr = jnp.exp2(s - m_new[:, 0:1])
        l_ref[...] = alpha * l_ref[...] + jnp.sum(pr, axis=-1, keepdims=True)
        acc_ref[...] = alpha * acc_ref[...] + _dot(pr.astype(BF16), v)
        m_ref[...] = m_new

    kc = cbuf[slot].astype(BF16)
    kr_t = pbuf[slot].astype(BF16)
    s = _dot_nt(q[:, 0:KV_LORA], kc) + _dot(q[:, KV_LORA:KV_LORA + ROPE_DIM], kr_t)
    update(s, kc)

    @pl.when(grp == ng - 1)
    def _():
        kn = knew_ref[0]
        sn = _dot_nt(q, kn)
        r = lax.broadcasted_iota(I32, sn.shape, 0)
        c = lax.broadcasted_iota(I32, sn.shape, 1)
        sn = jnp.where(c <= (r % tt), sn, NEG)
        update(sn, kn[:, 0:KV_LORA])
        o_ref[0] = acc_ref[...] / l_ref[...]


def _mla_decode(q_s, knew, cache_ckv, cache_kpe_t, page_table, layer):
    dbsz, rows, _ = q_s.shape
    tt = rows // MLA_H
    n_pages = page_table.shape[1]
    page = cache_ckv.shape[2]
    gp = _pick(n_pages, (64, 32, 16, 8, 4, 2, 1))
    ng = n_pages // gp
    total = dbsz * ng
    kern = functools.partial(_mla_decode_kernel, layer=layer, ng=ng, gp=gp, page=page, tt=tt, total=total)
    grid_spec = pltpu.PrefetchScalarGridSpec(
        num_scalar_prefetch=1, grid=(total,),
        in_specs=[pl.BlockSpec((1, rows, QK_PAD), lambda s, pt: (s // ng, 0, 0)),
                  pl.BlockSpec((1, 16, QK_PAD), lambda s, pt: (s // ng, 0, 0)),
                  pl.BlockSpec(memory_space=pl.ANY), pl.BlockSpec(memory_space=pl.ANY)],
        out_specs=pl.BlockSpec((1, rows, KV_LORA), lambda s, pt: (s // ng, 0, 0)),
        scratch_shapes=[pltpu.VMEM((DECODE_SLOTS, gp * page, KV_LORA), F32), pltpu.VMEM((DECODE_SLOTS, ROPE_DIM, gp * page), F32),
                        pltpu.SemaphoreType.DMA((DECODE_SLOTS,)), pltpu.SemaphoreType.DMA((DECODE_SLOTS,)),
                        pltpu.VMEM((rows, 128), F32), pltpu.VMEM((rows, 128), F32), pltpu.VMEM((rows, KV_LORA), F32)])
    return pl.pallas_call(kern, out_shape=jax.ShapeDtypeStruct((dbsz, rows, KV_LORA), F32), grid_spec=grid_spec,
                          compiler_params=_cparams(1), name="mla_decode")(
        page_table.reshape(-1), q_s, knew, cache_ckv, cache_kpe_t)


def _gla_constants(c):
    t = np.arange(c)
    blocks = [(t[None, :] <= t[:, None]), (t[None, :] > t[:, None])]
    masks = []
    m = c // 2
    while m >= 1:
        bd = (t // (2 * m)) * 2 * m + m
        upper = t >= bd
        a = upper[:, None] & (t[None, :] >= bd[:, None]) & (t[None, :] <= t[:, None])
        b = (~upper)[:, None] & (t[None, :] > t[:, None]) & (t[None, :] <= bd[:, None] - 1)
        blocks += [a, b]
        same = (t[:, None] // (2 * m)) == (t[None, :] // (2 * m))
        masks.append(same & upper[:, None] & (~upper)[None, :])
        m //= 2
    masks.append(t[:, None] == t[None, :])
    sel = np.concatenate(blocks, axis=0).astype(np.float32)
    masks = np.concatenate([np.tile(mm, (1, GLA_H)) for mm in masks], axis=0).astype(np.float32)
    r = np.arange(GLA_H * c)
    kmask = ((r[:, None] // c) == (np.arange(128)[None, :] // GLA_DK)).astype(np.float32)
    vmask = ((r[:, None] // c) == (np.arange(GLA_W)[None, :] // GLA_DV)).astype(np.float32)
    smask = ((np.arange(128)[:, None] // GLA_DK) == (np.arange(GLA_W)[None, :] // GLA_DV)).astype(np.float32)
    last = np.zeros((c, GLA_W), np.float32)
    last[c - 1, :] = 1.0
    gsum = ((np.arange(GLA_W)[:, None] // GLA_DV) == (np.arange(GLA_W)[None, :] // GLA_DV)).astype(np.float32)
    return sel, masks, kmask, vmask, smask, last, gsum


def _gla_kernel(*refs, c, nsub, nsteps, has_init, sps):
    if has_init:
        (q_ref, k_ref, v_ref, go_ref, g_ref, s0_ref, sel_ref, masks_ref, kmask_ref, vmask_ref, smask_ref, last_ref,
         gsum_ref, ggla_ref, y_ref, sout_ref, s_ref) = refs
    else:
        (q_ref, k_ref, v_ref, go_ref, g_ref, sel_ref, masks_ref, kmask_ref, vmask_ref, smask_ref, last_ref,
         gsum_ref, ggla_ref, y_ref, sout_ref, s_ref) = refs
        s0_ref = None
    j = pl.program_id(1)
    nlev = int(np.log2(c))
    smask = smask_ref[...]

    def load_state(sq):
        if has_init:
            s0 = s0_ref[sq]
            s_ref[...] = jnp.concatenate([s0] * GLA_H, axis=1) * smask
        else:
            s_ref[...] = jnp.zeros(s_ref.shape, F32)

    def store_state(sq):
        s = s_ref[...]
        acc = s[:, 0:GLA_DV]
        for h in range(1, GLA_H):
            acc = acc + s[:, GLA_DV * h:GLA_DV * (h + 1)]
        sout_ref[sq] = acc

    def chunk(i, carry):
        sl = pl.ds(i * c if isinstance(i, int) else pl.multiple_of(i * c, c), c)
        q = q_ref[sl, :] * (GLA_DK ** -0.5)
        k = k_ref[sl, :]
        v = v_ref[sl, :].astype(BF16)
        g_hi, g_lo = _split_bf16(g_ref[sl, :])
        sel = sel_ref[...]
        ee = _dot(sel, jnp.concatenate([g_hi, g_lo], axis=1))
        e_all = jnp.exp(ee[:, 0:128] + ee[:, 128:256])
        eb = e_all[0:c]
        s_old = s_ref[...]
        o = _dot((q * eb).astype(BF16), s_old.astype(BF16))
        kmask = kmask_ref[...]
        att = None
        for lev in range(nlev + 1):
            if lev < nlev:
                ql = (q * e_all[(2 + 2 * lev) * c:(3 + 2 * lev) * c]).astype(BF16)
                kl = k * e_all[(3 + 2 * lev) * c:(4 + 2 * lev) * c]
            else:
                ql, kl = q.astype(BF16), k
            kb = (jnp.concatenate([kl] * GLA_H, axis=0) * kmask).astype(BF16)
            part = _dot_nt(ql, kb) * masks_ref[lev * c:(lev + 1) * c, :]
            att = part if att is None else att + part
        vb = (jnp.concatenate([v] * GLA_H, axis=0) * vmask_ref[...].astype(BF16))
        o = o + _dot(att.astype(BF16), vb)
        kk = (k * e_all[c:2 * c]).astype(BF16)
        eb_hi, eb_lo = _split_bf16(eb)
        last = last_ref[...].astype(BF16)
        decay = _dot_tn(eb_hi, last) + _dot_tn(eb_lo, last)
        s_ref[...] = (decay * s_old + _dot_tn(kk, v)) * smask
        o2_hi, o2_lo = _split_bf16(o * o)
        gsum = gsum_ref[...]
        ms = (_dot(o2_hi, gsum) + _dot(o2_lo, gsum)) * (1.0 / GLA_DV)
        yc = o * lax.rsqrt(ms + EPS) * ggla_ref[...] * _silu(go_ref[sl, :])
        y_ref[sl, :] = yc.astype(BF16)
        return carry

    if sps > 1:
        for sq in range(sps):
            load_state(sq)
            for i in range(nsub):
                chunk(sq * nsub + i, 0)
            store_state(sq)
    else:
        pl.when(j == 0)(lambda: load_state(0))
        lax.fori_loop(0, nsub, chunk, 0, unroll=True)
        pl.when(j == nsteps - 1)(lambda: store_state(0))


def _gla(gq, gk, gv, go, glog, s0, bsz, w, chunk):
    n = gq.shape[0]
    t = n // bsz
    c = min(chunk, t)
    assert t % c == 0 and c & (c - 1) == 0 and c >= 8
    nsub = _pick(t // c, (8, 4, 2, 1))
    tc = c * nsub
    nsteps = t // tc
    consts = [jnp.asarray(a) for a in _gla_constants(c)]
    consts[0] = consts[0].astype(BF16)
    consts[6] = consts[6].astype(BF16)
    has_init = s0 is not None
    sps = _pick(bsz, (16, 8, 4, 2, 1)) if (nsteps == 1 and tc <= 16) else 1
    kern = functools.partial(_gla_kernel, c=c, nsub=nsub, nsteps=nsteps, has_init=has_init, sps=sps)
    rows = sps * tc
    row = lambda b, j: (b * nsteps + j, 0)
    full = lambda b, j: (0, 0)
    ins = [gq, gk, gv, go, glog]
    in_specs = [pl.BlockSpec((rows, 128), row), pl.BlockSpec((rows, 128), row), pl.BlockSpec((rows, 256), row),
                pl.BlockSpec((rows, 256), row), pl.BlockSpec((rows, 128), row)]
    if has_init:
        ins.append(s0)
        in_specs.append(pl.BlockSpec((sps, 128, GLA_DV), lambda b, j: (b, 0, 0)))
    ins += consts + [w["g_gla"]]
    in_specs += [pl.BlockSpec(a.shape, full) for a in consts] + [pl.BlockSpec(w["g_gla"].shape, full)]
    return pl.pallas_call(
        kern,
        out_shape=[jax.ShapeDtypeStruct((n, GLA_W), BF16), jax.ShapeDtypeStruct((bsz, 128, GLA_DV), F32)],
        grid=(bsz // sps, nsteps), in_specs=in_specs,
        out_specs=[pl.BlockSpec((rows, GLA_W), row), pl.BlockSpec((sps, 128, GLA_DV), lambda b, j: (b, 0, 0))],
        scratch_shapes=[pltpu.VMEM((128, GLA_W), F32)],
        compiler_params=_cparams(2), name="gla")(*ins)


def _memkv_kernel(m_ref, g_ref, wk_ref, wv_ref, mk_ref, mv_ref):
    h = _rms(m_ref[...], g_ref[...]).astype(BF16)
    mk_ref[...] = _dot(h, wk_ref[...])
    mv_ref[...] = _dot(h, wv_ref[...])


def _memkv(mem, w):
    n = mem.shape[0]
    tm = _pick(n, (256, 128, 64, 32, 16, 8))
    row = lambda i: (i, 0)
    full = lambda i: (0, 0)
    return pl.pallas_call(
        _memkv_kernel,
        out_shape=[jax.ShapeDtypeStruct((n, X_W), F32)] * 2, grid=(n // tm,),
        in_specs=[pl.BlockSpec((tm, D_MODEL), row), pl.BlockSpec((1, D_MODEL), full),
                  pl.BlockSpec((D_MODEL, X_W), full), pl.BlockSpec((D_MODEL, X_W), full)],
        out_specs=[pl.BlockSpec((tm, X_W), row)] * 2,
        compiler_params=_cparams(1), name="memkv")(mem, w["g_mem"], w["w_xk"], w["w_xv"])


def _post_pre(x_ref, ya_ref, ol_ref, yc_ref, wuv_ref, woa_ref, wob_ref, woc_ref, gx_ref, wxq_ref):
    yb = _dot(ol_ref[...], wuv_ref[...]).astype(BF16)
    x1 = x_ref[...] + _dot(ya_ref[...], woa_ref[...]) + _dot(yb, wob_ref[...]) + _dot(yc_ref[...], woc_ref[...])
    q = _dot(_rms(x1, gx_ref[...]).astype(BF16), wxq_ref[...])
    return x1, q


def _post_attn(q, mk_ref, mv_ref, nb, rpb):
    def head_kv(ref, h):
        if len(ref.shape) == 4:
            a = ref[:, :, h, :]
        else:
            a = ref[:, :, X_HD * h:X_HD * (h + 1)]
        return a.reshape(nb * N_MEM, X_HD).astype(BF16)

    outs = []
    for h in range(X_H):
        sl = slice(X_HD * h, X_HD * (h + 1))
        s = _dot_nt(q[:, sl].astype(BF16), head_kv(mk_ref, h)) * (X_HD ** -0.5)
        if nb > 1:
            r = lax.broadcasted_iota(I32, s.shape, 0)
            c = lax.broadcasted_iota(I32, s.shape, 1)
            s = jnp.where(r // rpb == c // N_MEM, s, NEG)
        e = jnp.exp(s - jnp.max(s, axis=-1, keepdims=True))
        p = e / jnp.sum(e, axis=-1, keepdims=True)
        outs.append(_dot(p.astype(BF16), head_kv(mv_ref, h)))
    return jnp.concatenate(outs, axis=1)


def _post_tail(x1, o, wxo_ref, gffn_ref, wrh_ref, wrl_ref, br_ref, ltri_ref, ustr_ref, x2_ref, rows_ref, info_ref, nch_ref):
    tm = x1.shape[0]
    lr = rows_ref.shape[0]
    x2 = x1 + _dot(o.astype(BF16), wxo_ref[...])
    x2_ref[...] = x2
    h3 = _rms(x2, gffn_ref[...])
    h_hi, h_lo = _split_bf16(h3)
    logit = _dot(h_hi, wrh_ref[...]) + _dot(h_lo, wrh_ref[...]) + _dot(h_hi, wrl_ref[...]) + br_ref[...]
    lane = lax.broadcasted_iota(I32, logit.shape, 1)
    gl = jnp.where(lane < N_GROUPS, logit, NEG)
    gmax = jnp.max(gl, axis=-1, keepdims=True)
    gsel = jnp.min(jnp.where(gl == gmax, lane, 1 << 20), axis=-1, keepdims=True)
    gw = 1.0 / jnp.sum(jnp.exp(gl - gmax), axis=-1, keepdims=True)
    emask = jnp.logical_and(lane >= EXPERT_LANE0, (lane - EXPERT_LANE0) // EXP_PER_GROUP == gsel)
    el = jnp.where(emask, logit, NEG)
    pe = jnp.where(emask, jnp.exp(el - jnp.max(el, axis=-1, keepdims=True)), 0.0)
    prob = pe / jnp.sum(pe, axis=-1, keepdims=True)
    prob = jnp.where(emask, prob, -1.0)
    p1 = jnp.max(prob, axis=-1, keepdims=True)
    i1 = jnp.min(jnp.where(prob == p1, lane, 1 << 20), axis=-1, keepdims=True)
    prob2 = jnp.where(lane == i1, -1.0, prob)
    p2 = jnp.max(prob2, axis=-1, keepdims=True)
    i2 = jnp.min(jnp.where(prob2 == p2, lane, 1 << 20), axis=-1, keepdims=True)
    gate1 = gw * p1 / (p1 + p2)
    gate2 = gw * p2 / (p1 + p2)
    oh1 = (lane == i1).astype(F32)
    oh2 = (lane == i2).astype(F32)
    both = oh1 + oh2
    before = _dot(ltri_ref[...], both.astype(BF16))
    cnt = jnp.sum(both, axis=0, keepdims=True)
    nch = jnp.floor((cnt + (MOE_CHUNK - 1.0)) * (1.0 / MOE_CHUNK))
    nch8 = jnp.broadcast_to(nch, (8, 128))
    off = _dot(nch8.astype(BF16), ustr_ref[...])[0:1] * float(MOE_CHUNK)
    pos = off + before
    v1 = oh1 * pos
    v2 = oh2 * pos
    loc1 = jnp.sum(v1, axis=-1, keepdims=True)
    loc2 = jnp.sum(v2, axis=-1, keepdims=True)
    ones8 = jnp.ones((8, 128), BF16)

    def as_row(v):
        hi = jnp.floor(v * (1.0 / 32.0))
        lo = v - 32.0 * hi
        return (32.0 * _dot_nt(ones8, hi.astype(BF16)) + _dot_nt(ones8, lo.astype(BF16)))[0:1]

    slot_row = lax.broadcasted_iota(I32, (lr, tm), 0).astype(F32)
    perm = jnp.logical_or(slot_row == as_row(v1), slot_row == as_row(v2))
    perm = jnp.where(perm, 1.0, 0.0).astype(BF16)
    rows_ref[...] = _dot(perm, h3.astype(BF16)).astype(BF16)
    nch_ref[...] = nch8
    info = jnp.where(lane == 2, gate1, 0.0)
    info = jnp.where(lane == 3, gate2, info)
    info = jnp.where(lane == 4, loc1, info)
    info = jnp.where(lane == 5, loc2, info)
    info_ref[...] = info


def _post_kernel(x_ref, ya_ref, ol_ref, yc_ref, mk_ref, mv_ref, wuv_ref, woa_ref, wob_ref, woc_ref, gx_ref, wxq_ref, wxo_ref,
                 gffn_ref, wrh_ref, wrl_ref, br_ref, ltri_ref, ustr_ref, x2_ref, rows_ref, info_ref, nch_ref):
    x1, q = _post_pre(x_ref, ya_ref, ol_ref, yc_ref, wuv_ref, woa_ref, wob_ref, woc_ref, gx_ref, wxq_ref)
    o = _post_attn(q, mk_ref, mv_ref, 1, x_ref.shape[0])
    _post_tail(x1, o, wxo_ref, gffn_ref, wrh_ref, wrl_ref, br_ref, ltri_ref, ustr_ref, x2_ref, rows_ref, info_ref, nch_ref)


def _post_short_kernel(x_ref, ya_ref, ol_ref, yc_ref, mk_ref, mv_ref, wuv_ref, woa_ref, wob_ref, woc_ref, gx_ref, wxq_ref,
                       wxo_ref, gffn_ref, wrh_ref, wrl_ref, br_ref, ltri_ref, ustr_ref, x2_ref, rows_ref, info_ref, nch_ref,
                       x1_s, q_s, o_s, *, nb, rpb):
    g = pl.program_id(0)
    tg = nb * rpb

    @pl.when(g == 0)
    def _():
        x1, q = _post_pre(x_ref, ya_ref, ol_ref, yc_ref, wuv_ref, woa_ref, wob_ref, woc_ref, gx_ref, wxq_ref)
        x1_s[...] = x1
        q_s[...] = q

    r0 = pl.multiple_of(g * tg, tg)
    o_s[pl.ds(r0, tg), :] = _post_attn(q_s[pl.ds(r0, tg), :], mk_ref, mv_ref, nb, rpb)

    @pl.when(g == pl.num_programs(0) - 1)
    def _():
        _post_tail(x1_s[...], o_s[...], wxo_ref, gffn_ref, wrh_ref, wrl_ref, br_ref, ltri_ref, ustr_ref,
                   x2_ref, rows_ref, info_ref, nch_ref)


def _local_rows(tm):
    need = TOP_K * tm + N_EXPERTS * (MOE_CHUNK - 1) + MOE_CHUNK
    return -(-need // 128) * 128


def _post(x, ya, olat, yc, mk, mv, bsz, w, kv_seq0=0):
    n = x.shape[0]
    t = n // bsz
    long_seq = t >= 16
    scratch = []
    if long_seq:
        tm = _pick(t, (512, 256, 128, 64, 32, 16))
        nb = 1
        grid = (bsz, t // tm)
        row = lambda b, j: (b * (t // tm) + j, 0)
        kvm = lambda b, j: (kv_seq0 + b, 0, 0)
        kern = _post_kernel
    else:
        nb = _pick(bsz, (8, 4, 2, 1))
        tm = n
        assert (nb * t) % 16 == 0 and kv_seq0 % nb == 0
        grid = (bsz // nb, 1)
        row = lambda b, j: (0, 0)
        kvm = lambda b, j: (kv_seq0 // nb + b, 0, 0)
        kern = functools.partial(_post_short_kernel, nb=nb, rpb=t)
        scratch = [pltpu.VMEM((n, D_MODEL), F32), pltpu.VMEM((n, X_W), F32), pltpu.VMEM((n, X_W), F32)]
    ltri = jnp.asarray(np.tril(np.ones((tm, tm), np.float32), -1)).astype(BF16)
    ustr = jnp.asarray(np.triu(np.ones((128, 128), np.float32), 1)).astype(BF16)
    lr = _local_rows(tm)
    ntile = n // tm
    tile = lambda b, j: (row(b, j)[0], 0)
    full = lambda b, j: (0, 0)
    wnames = ["w_uvbd", "wo_a", "wo_b", "wo_c", "g_x", "w_xq", "w_xo", "g_ffn", "w_r_hi", "w_r_lo", "b_r"]
    ins = [x, ya, olat, yc, mk, mv] + [w[k] for k in wnames] + [ltri, ustr]
    in_specs = [pl.BlockSpec((tm, D_MODEL), row), pl.BlockSpec((tm, POOL_W), row), pl.BlockSpec((tm, MLA_H * KV_LORA), row),
                pl.BlockSpec((tm, GLA_W), row)]
    if mk.ndim == 4:
        kvm4 = lambda b, j: kvm(b, j) + (0,)
        in_specs += [pl.BlockSpec((nb, N_MEM, X_H, X_HD), kvm4)] * 2
    else:
        in_specs += [pl.BlockSpec((nb, N_MEM, X_W), kvm)] * 2
    in_specs += [pl.BlockSpec(w[k].shape, full) for k in wnames] + [pl.BlockSpec((tm, tm), full), pl.BlockSpec((128, 128), full)]
    x2, rows, info, nch = pl.pallas_call(
        kern,
        out_shape=[jax.ShapeDtypeStruct((n, D_MODEL), F32), jax.ShapeDtypeStruct((ntile * lr, D_MODEL), BF16),
                   jax.ShapeDtypeStruct((n, 128), F32), jax.ShapeDtypeStruct((ntile * 8, 128), F32)],
        grid=grid, in_specs=in_specs,
        out_specs=[pl.BlockSpec((tm, D_MODEL), row), pl.BlockSpec((lr, D_MODEL), tile), pl.BlockSpec((tm, 128), row),
                   pl.BlockSpec((8, 128), tile)],
        scratch_shapes=scratch, compiler_params=_cparams(2), name="post")(*ins)
    return x2, rows, info, nch, tm


def _expert_kernel(sc_ref, be_ref, nu_ref, *refs, dummy, splits):
    nsrc = len(splits) - 1
    rows_hbm = refs[:nsrc]
    wg_ref, wu_ref, wd_ref, _, y_hbm, xbuf, ybuf, gsem, ssem = refs[nsrc:]
    j = pl.program_id(0)
    nu = nu_ref[0]
    slot = j % 2
    cpb = MOE_BLOCK // MOE_CHUNK

    def gather_start(blk, s):
        for c in range(cpb):
            chunk = sc_ref[blk * cpb + c]
            dst = xbuf.at[s, pl.ds(c * MOE_CHUNK, MOE_CHUNK)]
            for i in range(nsrc):
                def issue(i=i):
                    src = pl.multiple_of((chunk - splits[i]) * MOE_CHUNK, MOE_CHUNK)
                    pltpu.make_async_copy(rows_hbm[i].at[pl.ds(src, MOE_CHUNK)], dst, gsem.at[s]).start(priority=c % 2)
                if nsrc == 1:
                    issue()
                else:
                    pl.when(jnp.logical_and(chunk >= splits[i], chunk < splits[i + 1]))(issue)

    def gather_wait(s):
        for c in range(cpb):
            pltpu.make_async_copy(rows_hbm[0].at[pl.ds(0, MOE_CHUNK)], xbuf.at[s, pl.ds(c * MOE_CHUNK, MOE_CHUNK)],
                                  gsem.at[s]).wait()

    def writeback(blk, s, act):
        for c in range(cpb):
            chunk = sc_ref[blk * cpb + c]

            @pl.when(chunk != dummy)
            def _():
                dst = pl.multiple_of(chunk * MOE_CHUNK, MOE_CHUNK)
                act(pltpu.make_async_copy(ybuf.at[s, pl.ds(c * MOE_CHUNK, MOE_CHUNK)],
                                          y_hbm.at[pl.ds(dst, MOE_CHUNK)], ssem.at[s]), c % 2)

    @pl.when(j < nu)
    def _():
        pl.when(j == 0)(lambda: gather_start(0, 0))
        pl.when(j + 1 < nu)(lambda: gather_start(j + 1, 1 - slot))
        gather_wait(slot)

        @pl.when(j >= 2)
        def _():
            writeback(j - 2, slot, lambda cp, pr: cp.wait())

        xb = xbuf[slot]
        hid = _silu(_dot(xb, wg_ref[...])) * _dot(xb, wu_ref[...])
        ybuf[slot] = _dot(hid.astype(BF16), wd_ref[...]).astype(BF16)
        writeback(j, slot, lambda cp, pr: cp.start(priority=pr))

        @pl.when(j == nu - 1)
        def _():
            writeback(j, slot, lambda cp, pr: cp.wait())

            @pl.when(j >= 1)
            def _():
                writeback(j - 1, 1 - slot, lambda cp, pr: cp.wait())


def _experts(rows_list, src_chunk, block_e, n_used, dummy, w):
    n_rows = sum(r.shape[0] for r in rows_list)
    splits = tuple(int(v) for v in np.cumsum([0] + [r.shape[0] // MOE_CHUNK for r in rows_list]))
    nsrc = len(rows_list)
    nblk = block_e.shape[0]
    y_init = jnp.zeros((n_rows, D_MODEL), BF16)
    wmap = lambda j, sc, be, nu: (be[j], 0, 0)
    grid_spec = pltpu.PrefetchScalarGridSpec(
        num_scalar_prefetch=3, grid=(nblk,),
        in_specs=[pl.BlockSpec(memory_space=pl.ANY)] * nsrc +
                 [pl.BlockSpec((None, D_MODEL, D_EXPERT), wmap), pl.BlockSpec((None, D_MODEL, D_EXPERT), wmap),
                  pl.BlockSpec((None, D_EXPERT, D_MODEL), wmap),
                  pl.BlockSpec(memory_space=pl.ANY)],
        out_specs=pl.BlockSpec(memory_space=pl.ANY),
        scratch_shapes=[pltpu.VMEM((2, MOE_BLOCK, D_MODEL), BF16), pltpu.VMEM((2, MOE_BLOCK, D_MODEL), BF16),
                        pltpu.SemaphoreType.DMA((2,)), pltpu.SemaphoreType.DMA((2,))])
    return pl.pallas_call(functools.partial(_expert_kernel, dummy=dummy, splits=splits),
                          out_shape=jax.ShapeDtypeStruct((n_rows, D_MODEL), BF16), grid_spec=grid_spec,
                          input_output_aliases={6 + nsrc: 0}, compiler_params=_cparams(1), name="moe_experts")(
        src_chunk, block_e, n_used, *rows_list, w["w_eg"], w["w_eu"], w["w_ed"], y_init)


def _combine_kernel(x_ref, info_ref, y_ref, g_ref, o_ref, *, final):
    tm, lr = x_ref.shape[0], y_ref.shape[0]
    info = info_ref[...]
    col = lax.broadcasted_iota(I32, (tm, lr), 1).astype(F32)
    gmat = jnp.where(col == info[:, 4:5], info[:, 2:3], 0.0) + jnp.where(col == info[:, 5:6], info[:, 3:4], 0.0)
    y = x_ref[...] + _dot(gmat.astype(BF16), y_ref[...])
    if final:
        y = _rms(y, g_ref[...])
    o_ref[...] = y


def _combine(x2, info, y_loc, tm, g_final, final, tile0=0):
    n = x2.shape[0]
    lr = _local_rows(tm)
    return pl.pallas_call(
        functools.partial(_combine_kernel, final=final),
        out_shape=jax.ShapeDtypeStruct((n, D_MODEL), F32), grid=(n // tm,),
        in_specs=[pl.BlockSpec((tm, D_MODEL), lambda i: (i, 0)), pl.BlockSpec((tm, 128), lambda i: (i, 0)),
                  pl.BlockSpec((lr, D_MODEL), lambda i: (tile0 + i, 0)), pl.BlockSpec((1, D_MODEL), lambda i: (0, 0))],
        out_specs=pl.BlockSpec((tm, D_MODEL), lambda i: (i, 0)),
        compiler_params=_cparams(1), name="moe_combine")(x2, info, y_loc, g_final)


def _moe_experts(rows_list, nch, tm, w):
    ntile = sum(r.shape[0] for r in rows_list) // _local_rows(tm)
    lrc = _local_rows(tm) // MOE_CHUNK
    cpb = MOE_BLOCK // MOE_CHUNK
    dummy = lrc - 1
    nc = nch.reshape(ntile, 8, 128)[:, 0, EXPERT_LANE0:EXPERT_LANE0 + N_EXPERTS].astype(I32)
    src_base = jnp.arange(ntile, dtype=I32)[:, None] * lrc + (jnp.cumsum(nc, axis=1) - nc)
    pe = (jnp.sum(nc, axis=0) + cpb - 1) // cpb * cpb
    pend = jnp.cumsum(pe)
    dst = (pend - pe)[None, :] + (jnp.cumsum(nc, axis=0) - nc)
    dst_f, nc_f, src_f = dst.T.reshape(-1), nc.T.reshape(-1), src_base.T.reshape(-1)
    max_chunks = ntile * ((TOP_K * tm + N_EXPERTS * (MOE_CHUNK - 1)) // MOE_CHUNK) + N_EXPERTS * (cpb - 1)
    nblk = -(-max_chunks // cpb)
    pos = jnp.arange(nblk * cpb, dtype=I32)
    run = jnp.sum((dst_f[None, :] <= pos[:, None]).astype(I32), axis=1) - 1
    at_run = jnp.stack([dst_f, nc_f, src_f], axis=1)[run]
    rel = pos - at_run[:, 0]
    src_chunk = jnp.where(rel < at_run[:, 1], at_run[:, 2] + rel, dummy).astype(I32)
    blk0 = jnp.arange(nblk, dtype=I32) * cpb
    block_e = jnp.minimum(jnp.sum((pend[None, :] <= blk0[:, None]).astype(I32), axis=1), N_EXPERTS - 1).astype(I32)
    n_used = (pend[-1:] // cpb).astype(I32)
    return _experts(rows_list, src_chunk, block_e, n_used, dummy, w)


def _rope_tables(pos):
    half = ROPE_DIM // 2
    inv = ROPE_BASE ** (-jnp.arange(half, dtype=F32) * 2.0 / ROPE_DIM)
    ang = pos.astype(F32)[:, None] * inv[None, :]
    c, s = jnp.cos(ang), jnp.sin(ang)
    cos32 = jnp.concatenate([c, c], axis=1)
    sin32 = jnp.concatenate([-s, s], axis=1)
    return jnp.tile(cos32, (1, MLA_H)), jnp.tile(sin32, (1, MLA_H))


def _prep_layer(l, p):
    w_in = p["w_in"][l]
    zeros80 = jnp.zeros((D_MODEL, 80), F32)
    w_in_p = jnp.concatenate([w_in[:, 0:640], w_in[:, 640:672], w_in[:, 1184:1200], zeros80,
                              w_in[:, 672:1184], w_in[:, 1200:1456]], axis=1).astype(BF16)
    assert w_in_p.shape[1] == IN_PAD
    w_uq = p["w_uq"][l].reshape(Q_LORA, MLA_H, NOPE + ROPE_DIM)
    w_uq_p = jnp.concatenate([w_uq[:, :, :NOPE].reshape(Q_LORA, MLA_H * NOPE),
                              w_uq[:, :, NOPE:].reshape(Q_LORA, MLA_H * ROPE_DIM)], axis=1).astype(BF16)
    eye_h = jnp.eye(MLA_H, dtype=F32)
    w_ukbd = jnp.einsum("hnc,hg->hngc", p["w_uk"][l].transpose(1, 2, 0), eye_h).reshape(MLA_H * NOPE, MLA_H * KV_LORA).astype(BF16)
    w_uvbd = jnp.einsum("hcv,hg->hcgv", p["w_uv"][l].transpose(1, 0, 2), eye_h).reshape(MLA_H * KV_LORA, MLA_H * V_DIM).astype(BF16)
    w_gk2 = jnp.zeros((128, 128), F32).at[ROPE_DIM:ROPE_DIM + GATE_RANK, :].set(p["w_gk2"][l]).astype(BF16)
    w_poolbd = jnp.einsum("gcd,gk->gckd", p["w_pool"][l], jnp.eye(POOL_GROUPS, dtype=F32)).reshape(POOL_W, POOL_W).astype(BF16)
    w_out = p["w_out"][l].astype(BF16)
    w_r = jnp.zeros((D_MODEL, 128), F32).at[:, 0:N_GROUPS].set(p["w_rg"][l]).at[:, EXPERT_LANE0:EXPERT_LANE0 + N_EXPERTS].set(p["w_re"][l])
    w_r_hi = w_r.astype(BF16)
    b_r = jnp.zeros((1, 128), F32).at[0, 0:N_GROUPS].set(p["b_rg"][l]).at[0, EXPERT_LANE0:EXPERT_LANE0 + N_EXPERTS].set(p["b_re"][l])
    return {
        "g_mix": p["g_mix"][l][None, :], "w_in": w_in_p, "g_qn": p["g_qn"][l][None, :], "w_uq": w_uq_p, "w_ukbd": w_ukbd,
        "g_kvn": p["g_kvn"][l][None, :], "w_gk2": w_gk2, "b_gk": p["b_gk"][l][None, :],
        "w_poolbd": w_poolbd, "pool_scale": p["pool_scale"][l][None, :],
        "g_gla": jnp.tile(p["g_gla"][l], GLA_H)[None, :],
        "w_uvbd": w_uvbd, "wo_a": w_out[0:256], "wo_b": w_out[256:768], "wo_c": w_out[768:1024],
        "g_x": p["g_x"][l][None, :], "w_xq": p["w_xq"][l].astype(BF16), "w_xo": p["w_xo"][l].astype(BF16),
        "g_mem": p["g_mem"][l][None, :], "w_xk": p["w_xk"][l].astype(BF16), "w_xv": p["w_xv"][l].astype(BF16),
        "g_ffn": p["g_ffn"][l][None, :], "w_r_hi": w_r_hi, "w_r_lo": (w_r - w_r_hi.astype(F32)).astype(BF16), "b_r": b_r,
        "w_eg": p["w_eg"][l].astype(BF16), "w_eu": p["w_eu"][l].astype(BF16), "w_ed": p["w_ed"][l].astype(BF16),
    }


GLA_CHUNK = 64


def kernel(x_prompt, x_sample, mem_prompt, cache_ckv, cache_kpe, page_table, state_pool, state_gla, cache_mem_k, cache_mem_v, g_mix, w_in, w_pool, pool_scale, g_qn, w_uq, g_kvn, w_uk, w_uv, w_gk2, b_gk, g_gla, w_out, g_x, g_mem, w_xq, w_xk, w_xv, w_xo, g_ffn, w_rg, b_rg, w_re, b_re, w_eg, w_eu, w_ed, g_final):
    params = dict(g_mix=g_mix, w_in=w_in, w_pool=w_pool, pool_scale=pool_scale, g_qn=g_qn, w_uq=w_uq, g_kvn=g_kvn, w_uk=w_uk,
                  w_uv=w_uv, w_gk2=w_gk2, b_gk=b_gk, g_gla=g_gla, w_out=w_out, g_x=g_x, g_mem=g_mem, w_xq=w_xq, w_xk=w_xk,
                  w_xv=w_xv, w_xo=w_xo, g_ffn=g_ffn, w_rg=w_rg, b_rg=b_rg, w_re=w_re, b_re=b_re, w_eg=w_eg, w_eu=w_eu, w_ed=w_ed)
    depth = w_in.shape[0]
    bsz, seq, _ = x_prompt.shape
    dbsz, dseq, _ = x_sample.shape
    past_len = page_table.shape[1] * cache_ckv.shape[2]
    gfin = g_final[None, :]
    cache_kpe_t = jnp.swapaxes(cache_kpe, 2, 3)

    cos_p, sin_p = _rope_tables(jnp.arange(seq, dtype=I32))
    cos_s, sin_s = _rope_tables(past_len + (jnp.arange(dbsz * dseq, dtype=I32) % dseq))

    xp = x_prompt.reshape(bsz * seq, D_MODEL)
    xs = x_sample.reshape(dbsz * dseq, D_MODEL)
    mem = mem_prompt.reshape(bsz * N_MEM, D_MODEL)
    gpad = 8 - dseq
    assert 0 <= gpad < 8
    assert past_len >= max(POOL_WINDOWS) - 1

    share_moe = _pick(seq, (512, 256, 128, 64, 32, 16)) == dbsz * dseq if seq >= 16 else False
    outs = {k: [] for k in ("ckv_p", "kpe_p", "pool_p", "gla_p", "mk_p", "mv_p", "ckv_s", "kpe_s", "pool_s", "gla_s")}
    for l in range(depth):
        w = _prep_layer(l, params)
        final = l == depth - 1
        mk, mv = _memkv(mem, w)
        u, q_hm, kvb, ckv, kpe, gq, gk, gv, go, glog = _inproj(xp, cos_p, sin_p, w)
        ya, pool_new = _pool_prompt(u, bsz, w)
        olat = _mla_prompt(q_hm, kvb, bsz)
        yc, s_new = _gla(gq, gk, gv, go, glog, None, bsz, w, GLA_CHUNK)
        x2p, rows_p, info_p, nch_p, tm_p = _post(xp, ya, olat, yc, mk.reshape(bsz, N_MEM, X_W), mv.reshape(bsz, N_MEM, X_W),
                                                 bsz, w)
        if not share_moe:
            xp = _combine(x2p, info_p, _moe_experts([rows_p], nch_p, tm_p, w), tm_p, gfin, final)
        outs["ckv_p"].append(ckv.reshape(bsz, seq, KV_LORA))
        outs["kpe_p"].append(kpe.reshape(bsz, seq, ROPE_DIM))
        outs["pool_p"].append(pool_new)
        outs["gla_p"].append(s_new.reshape(bsz, GLA_H, GLA_DK, GLA_DV))
        outs["mk_p"].append(mk.reshape(bsz, N_MEM, X_H, X_HD))
        outs["mv_p"].append(mv.reshape(bsz, N_MEM, X_H, X_HD))
        u, q_hm, kvb, ckv, kpe, gq, gk, gv, go, glog = _inproj(xs, cos_s, sin_s, w)
        ue_tm = jnp.concatenate([state_pool[l], u.reshape(dbsz, dseq, POOL_W)], axis=1).transpose(1, 0, 2)
        ya_tm, st_tm = _pool_sample(ue_tm, w)
        ya = ya_tm.transpose(1, 0, 2).reshape(dbsz * dseq, POOL_W)
        q_s = q_hm.reshape(MLA_H, dbsz, dseq, QK_PAD).transpose(1, 0, 2, 3).reshape(dbsz, MLA_H * dseq, QK_PAD)
        knew = jnp.pad(kvb.reshape(dbsz, dseq, QK_PAD), ((0, 0), (0, 16 - dseq), (0, 0)))
        o_s = _mla_decode(q_s, knew, cache_ckv, cache_kpe_t, page_table, l)
        olat = o_s.reshape(dbsz, MLA_H, dseq, KV_LORA).transpose(0, 2, 1, 3).reshape(dbsz * dseq, MLA_H * KV_LORA).astype(BF16)

        def pad8(a):
            return jnp.pad(a.reshape(dbsz, dseq, -1), ((0, 0), (0, gpad), (0, 0))).reshape(dbsz * 8, -1)

        yc8, s_new = _gla(pad8(gq), pad8(gk), pad8(gv), pad8(go), pad8(glog),
                          state_gla[l].reshape(dbsz, GLA_H * GLA_DK, GLA_DV), dbsz, w, 8)
        yc = yc8.reshape(dbsz, 8, GLA_W)[:, :dseq].reshape(dbsz * dseq, GLA_W)
        mem_k4 = cache_mem_k.reshape(depth * dbsz, N_MEM, X_H, X_HD)
        mem_v4 = cache_mem_v.reshape(depth * dbsz, N_MEM, X_H, X_HD)
        x2s, rows_s, info_s, nch_s, tm_s = _post(xs, ya, olat, yc, mem_k4, mem_v4, dbsz, w, kv_seq0=l * dbsz)
        if share_moe:
            ntile_p = (bsz * seq) // tm_p
            y_loc = _moe_experts([rows_p, rows_s], jnp.concatenate([nch_p, nch_s], axis=0), tm_p, w)
            xp = _combine(x2p, info_p, y_loc, tm_p, gfin, final)
            xs = _combine(x2s, info_s, y_loc, tm_p, gfin, final, tile0=ntile_p)
        else:
            xs = _combine(x2s, info_s, _moe_experts([rows_s], nch_s, tm_s, w), tm_s, gfin, final)
        outs["ckv_s"].append(ckv.reshape(dbsz, dseq, KV_LORA))
        outs["kpe_s"].append(kpe.reshape(dbsz, dseq, ROPE_DIM))
        outs["pool_s"].append(st_tm.transpose(1, 0, 2))
        outs["gla_s"].append(s_new.reshape(dbsz, GLA_H, GLA_DK, GLA_DV))

    st = lambda k: jnp.stack(outs[k])
    return (xp.reshape(bsz, seq, D_MODEL), xs.reshape(dbsz, dseq, D_MODEL),
            st("ckv_p"), st("kpe_p"), st("pool_p"), st("gla_p"), st("mk_p"), st("mv_p"),
            st("ckv_s"), st("kpe_s"), st("pool_s"), st("gla_s"))
```
